```python
import jax, jax.numpy as jnp
from jax import lax
import numpy as np

D_MODEL = 1024
BATCH = 8
SEQ = 4096
DEPTH = 1

HEAD_DIM = 64
N_HEADS_A = 8
DILATED_PATTERNS = ((128, 1), (512, 4), (2048, 16))
N_HEADS_B = 8
Q_LORA_RANK = 256
KV_LORA_RANK = 128
QK_NOPE_DIM = 64
QK_ROPE_DIM = 32
V_HEAD_DIM = 64
ROPE_THETA = 10000.0
Q_BLOCK = 128
N_BUCKETS = 32
MAX_DISTANCE = 1024
D_FF = 4 * D_MODEL
NORM_EPS = 1e-6
NEG_INF = -1e30

WIDTH_A = N_HEADS_A * HEAD_DIM
WIDTH_B = N_HEADS_B * V_HEAD_DIM
MIX_WIDTH = WIDTH_A + WIDTH_B
IN_SPLITS = (WIDTH_A, WIDTH_A, WIDTH_A, Q_LORA_RANK, KV_LORA_RANK, QK_ROPE_DIM)
IN_COLS = sum(IN_SPLITS)

kernel_name = "hybrid_dilated_mla_encoder_block"


def rms_norm(x, g):
    xf = x.astype(jnp.float32)
    y = xf * lax.rsqrt(jnp.mean(xf * xf, axis=-1, keepdims=True) + NORM_EPS)
    return (y * g.astype(jnp.float32)).astype(x.dtype)


def t5_buckets(rel):
    nb = N_BUCKETS // 2
    max_exact = nb // 2
    ret = (rel > 0).astype(np.int32) * nb
    n = np.abs(rel)
    large = max_exact + (np.log(np.maximum(n, 1) / max_exact)
                         / np.log(MAX_DISTANCE / max_exact) * (nb - max_exact)).astype(np.int32)
    large = np.minimum(large, nb - 1)
    return (ret + np.where(n < max_exact, n, large)).astype(np.int32)


def dilated_window_pattern(q, k, v, rel_bias, window, dilation):
    B, S, H, Dh = q.shape
    radius = window // (2 * dilation)
    blk = radius
    L = S // dilation
    nblk = -(-L // blk)
    pad = nblk * blk - L
    scale = HEAD_DIM ** -0.5

    def strided(t):
        return t.reshape(B, L, dilation, H, Dh).transpose(0, 3, 2, 1, 4)

    qb = jnp.pad(strided(q), ((0, 0), (0, 0), (0, 0), (0, pad), (0, 0)))
    qb = qb.reshape(B, H, dilation, nblk, blk, Dh)

    def key_windows(t):
        t = jnp.pad(strided(t), ((0, 0), (0, 0), (0, 0), (blk, pad + blk), (0, 0)))
        t = t.reshape(B, H, dilation, nblk + 2, blk, Dh)
        return jnp.concatenate([t[:, :, :, :-2], t[:, :, :, 1:-1], t[:, :, :, 2:]], axis=4)

    kw = key_windows(k)
    vw = key_windows(v)

    qi = np.arange(blk)[:, None]
    ki = np.arange(3 * blk)[None, :]
    rel = ki - blk - qi
    kidx = np.arange(nblk)[:, None, None] * blk - blk + ki[None]
    valid = (np.abs(rel) <= radius)[None] & (kidx >= 0) & (kidx < L)
    bias = rel_bias[jnp.asarray(t5_buckets(rel * dilation))]
    bias = jnp.transpose(bias, (2, 0, 1)).astype(jnp.float32)[:, None, None]

    s = jnp.einsum('bhrnqd,bhrnkd->bhrnqk', qb, kw).astype(jnp.float32) * scale + bias
    s = jnp.where(jnp.asarray(valid)[None, None, None], s, NEG_INF)
    m = jnp.max(s, axis=-1, keepdims=True)
    p = jnp.exp(s - m)
    den = jnp.sum(p, axis=-1, keepdims=True)
    num = jnp.einsum('bhrnqk,bhrnkd->bhrnqd', p, vw.astype(jnp.float32))

    def unstride(t):
        c = t.shape[-1]
        t = t.reshape(B, H, dilation, nblk * blk, c)[:, :, :, :L]
        return t.transpose(0, 3, 2, 1, 4).reshape(B, S, H, c)

    return unstride(num), unstride(m), unstride(den)


def dilated_attention(q, k, v, rel_bias):
    parts = [dilated_window_pattern(q, k, v, rel_bias, w, d) for (w, d) in DILATED_PATTERNS]
    m_all = parts[0][1]
    for _, m_i, _ in parts[1:]:
        m_all = jnp.maximum(m_all, m_i)
    num = 0.0
    den = 0.0
    for num_i, m_i, den_i in parts:
        w_i = jnp.exp(m_i - m_all)
        num = num + w_i * num_i
        den = den + w_i * den_i
    return (num / den).astype(q.dtype)


def rope_tables(S):
    inv_freq = ROPE_THETA ** (-jnp.arange(0, QK_ROPE_DIM, 2, dtype=jnp.float32) / QK_ROPE_DIM)
    pos = jnp.arange(S, dtype=jnp.float32)
    freqs = pos[:, None] * inv_freq[None, :]
    return jnp.cos(freqs), jnp.sin(freqs)


def apply_rope(t, cos, sin):
    tf = t.astype(jnp.float32)
    half = tf.shape[-1] // 2
    t1, t2 = tf[..., :half], tf[..., half:]
    return jnp.concatenate([t1 * cos - t2 * sin, t2 * cos + t1 * sin], axis=-1).astype(t.dtype)


def latent_attention(c_q, c_kv, k_rope, q_norm_g, w_q_b, kv_norm_g, w_kv_b, cos, sin):
    B, S, _ = c_q.shape
    H = N_HEADS_B
    dqk = QK_NOPE_DIM + QK_ROPE_DIM
    q = (rms_norm(c_q, q_norm_g) @ w_q_b).reshape(B, S, H, dqk)
    q_nope, q_pe = q[..., :QK_NOPE_DIM], q[..., QK_NOPE_DIM:]
    q_pe = apply_rope(q_pe, cos[:, None, :], sin[:, None, :])
    kv = (rms_norm(c_kv, kv_norm_g) @ w_kv_b).reshape(B, S, H, QK_NOPE_DIM + V_HEAD_DIM)
    k_nope, v = kv[..., :QK_NOPE_DIM], kv[..., QK_NOPE_DIM:]
    k_pe = apply_rope(k_rope, cos, sin)
    k = jnp.concatenate([k_nope, jnp.broadcast_to(k_pe[:, :, None, :], (B, S, H, QK_ROPE_DIM))], axis=-1)
    q = jnp.concatenate([q_nope, q_pe], axis=-1)

    scale = dqk ** -0.5
    qb = q.reshape(B, S // Q_BLOCK, Q_BLOCK, H, dqk).transpose(1, 0, 3, 2, 4)
    kh = k.transpose(0, 2, 1, 3)
    vh = v.transpose(0, 2, 1, 3).astype(jnp.float32)

    def attend(q_blk):
        s = jnp.einsum('bhqd,bhkd->bhqk', q_blk, kh).astype(jnp.float32) * scale
        p = jax.nn.softmax(s, axis=-1)
        return jnp.einsum('bhqk,bhkd->bhqd', p, vh).astype(v.dtype)

    o = lax.map(attend, qb)
    return o.transpose(1, 0, 3, 2, 4).reshape(B, S, H * V_HEAD_DIM)


def setup_inputs(seed: int = 0) -> dict:
    key = jax.random.key(seed)
    ks = jax.random.split(key, 16)
    f32 = jnp.float32

    def nrm(k, shape, fan_in):
        return jax.random.normal(k, shape, f32) * (fan_in ** -0.5)

    def gain(k, shape):
        return 1.0 + 0.02 * jax.random.normal(k, shape, f32)

    return {
        "x": jax.random.normal(ks[0], (BATCH, SEQ, D_MODEL), f32),
        "mix_norm_g": gain(ks[1], (DEPTH, D_MODEL)),
        "w_in": nrm(ks[2], (DEPTH, D_MODEL, IN_COLS), D_MODEL),
        "q_norm_g": gain(ks[3], (DEPTH, Q_LORA_RANK)),
        "w_q_b": nrm(ks[4], (DEPTH, Q_LORA_RANK, N_HEADS_B * (QK_NOPE_DIM + QK_ROPE_DIM)), Q_LORA_RANK),
        "kv_norm_g": gain(ks[5], (DEPTH, KV_LORA_RANK)),
        "w_kv_b": nrm(ks[6], (DEPTH, KV_LORA_RANK, N_HEADS_B * (QK_NOPE_DIM + V_HEAD_DIM)), KV_LORA_RANK),
        "w_out": nrm(ks[7], (DEPTH, MIX_WIDTH, D_MODEL), MIX_WIDTH),
        "mlp_norm_g": gain(ks[8], (DEPTH, D_MODEL)),
        "w_up": nrm(ks[9], (DEPTH, D_MODEL, D_FF), D_MODEL),
        "w_down": nrm(ks[10], (DEPTH, D_FF, D_MODEL), D_FF),
        "rel_bias": 0.5 * jax.random.normal(ks[11], (N_BUCKETS, N_HEADS_A), f32),
        "final_norm_g": gain(ks[12], (D_MODEL,)),
    }


def reference(x, mix_norm_g, w_in, q_norm_g, w_q_b, kv_norm_g, w_kv_b, w_out,
              mlp_norm_g, w_up, w_down, rel_bias, final_norm_g):
    B, S, _ = x.shape
    cos, sin = rope_tables(S)
    split_at = [int(v) for v in np.cumsum(IN_SPLITS)[:-1]]
    h = x
    for layer in range(DEPTH):
        u = rms_norm(h, mix_norm_g[layer])
        proj = u @ w_in[layer]
        q_a, k_a, v_a, c_q, c_kv, k_rope = jnp.split(proj, split_at, axis=-1)
        o_a = dilated_attention(q_a.reshape(B, S, N_HEADS_A, HEAD_DIM),
                                k_a.reshape(B, S, N_HEADS_A, HEAD_DIM),
                                v_a.reshape(B, S, N_HEADS_A, HEAD_DIM),
                                rel_bias).reshape(B, S, WIDTH_A)
        o_b = latent_attention(c_q, c_kv, k_rope, q_norm_g[layer], w_q_b[layer],
                               kv_norm_g[layer], w_kv_b[layer], cos, sin)
        h = h + jnp.concatenate([o_a, o_b], axis=-1) @ w_out[layer]
        u = rms_norm(h, mlp_norm_g[layer])
        h = h + jnp.square(jax.nn.relu(u @ w_up[layer])) @ w_down[layer]
    return rms_norm(h, final_norm_g)
```

```python
import functools
import math

import jax
import jax.numpy as jnp
import numpy as np
from jax import lax
from jax.experimental import pallas as pl
from jax.experimental.pallas import tpu as pltpu

D_MODEL = 1024
HEAD_DIM = 64
N_HEADS_A = 8
DILATED_PATTERNS = ((128, 1), (512, 4), (2048, 16))
N_HEADS_B = 8
Q_LORA_RANK = 256
KV_LORA_RANK = 128
QK_NOPE_DIM = 64
QK_ROPE_DIM = 32
V_HEAD_DIM = 64
ROPE_THETA = 10000.0
N_BUCKETS = 32
MAX_DISTANCE = 1024
D_FF = 4 * D_MODEL
NORM_EPS = 1e-6
NEG_INF = -1e30
WIDTH_A = N_HEADS_A * HEAD_DIM
WIDTH_B = N_HEADS_B * V_HEAD_DIM

LANES = 128
VMEM_LIMIT_BYTES = 56 * 1024 * 1024

HEAD_SLOT = LANES
QK_WIDTH_B = N_HEADS_B * HEAD_SLOT
VT_ROWS = 80
VT_ONES_ROW = V_HEAD_DIM
IN_COLS_PAD = 2048
CQ_OFF = 3 * WIDTH_A
CKV_OFF = CQ_OFF + Q_LORA_RANK
LAT_WIDTH = 2 * LANES
RADIUS = 64
Q_BLK_A = 2 * RADIUS
K_WIN_A = 4 * RADIUS

PROJ_TM = 512
MLP_TM = 512
MLA_TQ = 256
FF_CHUNK = 1024


def _rms(xf, g):
    return xf * lax.rsqrt(jnp.mean(xf * xf, axis=-1, keepdims=True) + NORM_EPS) * g


def _proj_kernel(x_ref, g_ref, win_ref, qg_ref, wq_ref, kvg_ref, wk_ref, wvt_ref,
                 cq_ref, sq_ref, ck_ref, sk_ref,
                 qa_ref, ka_ref, va_ref, qm_ref, km_ref, vt_ref):
    x = x_ref[...]
    u = _rms(x, g_ref[...]).astype(jnp.bfloat16)
    proj = jnp.dot(u, win_ref[...], preferred_element_type=jnp.float32)

    qa_ref[...] = (proj[:, 0:WIDTH_A] * (HEAD_DIM ** -0.5)).astype(jnp.bfloat16)
    ka_ref[...] = proj[:, WIDTH_A:2 * WIDTH_A].astype(jnp.bfloat16)
    va_ref[...] = proj[:, 2 * WIDTH_A:3 * WIDTH_A].astype(jnp.bfloat16)

    cq = _rms(proj[:, CQ_OFF:CQ_OFF + Q_LORA_RANK], qg_ref[...]).astype(jnp.bfloat16)
    q2 = jnp.dot(cq, wq_ref[...], preferred_element_type=jnp.float32)
    cq_t = jnp.tile(cq_ref[...], (1, N_HEADS_B))
    sq_t = jnp.tile(sq_ref[...], (1, N_HEADS_B))
    qm_ref[...] = (q2[:, :QK_WIDTH_B] * cq_t + q2[:, QK_WIDTH_B:] * sq_t).astype(jnp.bfloat16)

    ckv = _rms(proj[:, CKV_OFF:CKV_OFF + KV_LORA_RANK], kvg_ref[...])
    hi = proj[:, CKV_OFF + KV_LORA_RANK:IN_COLS_PAD]
    lane = lax.broadcasted_iota(jnp.int32, hi.shape, 1)
    hi = jnp.where(lane == LANES - 1, 1.0, hi)
    lat = jnp.concatenate([ckv, hi], axis=-1).astype(jnp.bfloat16)
    k2 = jnp.dot(lat, wk_ref[...], preferred_element_type=jnp.float32)
    ck_t = jnp.tile(ck_ref[...], (1, N_HEADS_B))
    sk_t = jnp.tile(sk_ref[...], (1, N_HEADS_B))
    km_ref[...] = (k2[:, :QK_WIDTH_B] * ck_t + k2[:, QK_WIDTH_B:] * sk_t).astype(jnp.bfloat16)
    vt = lax.dot_general(wvt_ref[...], lat, (((1,), (1,)), ((), ())),
                         preferred_element_type=jnp.float32)
    vt_ref[0] = vt.astype(jnp.bfloat16)


def _proj_call(x2, g, win, qg, wq, kvg, wk, wvt, cq, sq, ck, sk, batch, seq):
    tokens = x2.shape[0]
    tm = PROJ_TM
    sblk = seq // tm
    row = lambda i: (i, 0)
    const = lambda i: (0, 0)
    pos = lambda i: (i % sblk, 0)

    def wspec(shape):
        return pl.BlockSpec(shape, const, pipeline_mode=pl.Buffered(1))

    bf = jnp.bfloat16
    return pl.pallas_call(
        _proj_kernel,
        grid=(tokens // tm,),
        in_specs=[
            pl.BlockSpec((tm, D_MODEL), row),
            wspec((1, D_MODEL)),
            wspec((D_MODEL, IN_COLS_PAD)),
            wspec((1, Q_LORA_RANK)),
            wspec((Q_LORA_RANK, 2 * QK_WIDTH_B)),
            wspec((1, KV_LORA_RANK)),
            wspec((LAT_WIDTH, 2 * QK_WIDTH_B)),
            wspec((N_HEADS_B * VT_ROWS, LAT_WIDTH)),
            pl.BlockSpec((tm, HEAD_SLOT), pos),
            pl.BlockSpec((tm, HEAD_SLOT), pos),
            pl.BlockSpec((tm, HEAD_SLOT), pos),
            pl.BlockSpec((tm, HEAD_SLOT), pos),
        ],
        out_specs=[
            pl.BlockSpec((tm, WIDTH_A), row),
            pl.BlockSpec((tm, WIDTH_A), row),
            pl.BlockSpec((tm, WIDTH_A), row),
            pl.BlockSpec((tm, QK_WIDTH_B), row),
            pl.BlockSpec((tm, QK_WIDTH_B), row),
            pl.BlockSpec((1, N_HEADS_B * VT_ROWS, tm), lambda i: (i // sblk, 0, i % sblk)),
        ],
        out_shape=[
            jax.ShapeDtypeStruct((tokens, WIDTH_A), bf),
            jax.ShapeDtypeStruct((tokens, WIDTH_A), bf),
            jax.ShapeDtypeStruct((tokens, WIDTH_A), bf),
            jax.ShapeDtypeStruct((tokens, QK_WIDTH_B), bf),
            jax.ShapeDtypeStruct((tokens, QK_WIDTH_B), bf),
            jax.ShapeDtypeStruct((batch, N_HEADS_B * VT_ROWS, seq), bf),
        ],
        compiler_params=pltpu.CompilerParams(
            dimension_semantics=("arbitrary",), vmem_limit_bytes=VMEM_LIMIT_BYTES),
        name="proj",
    )(x2, g, win, qg, wq, kvg, wk, wvt, cq, sq, ck, sk)


def _dilated_kernel(*refs, sub_len, final):
    if final:
        (q_ref, k_ref, v_ref, bias_ref, n1_ref, s1_ref, n2_ref, s2_ref, o_ref) = refs
    else:
        (q_ref, k_ref, v_ref, bias_ref, num_ref, st_ref) = refs
    j = pl.program_id(2)
    nblk = sub_len // Q_BLK_A
    w0 = jnp.clip(j * Q_BLK_A - RADIUS, 0, sub_len - K_WIN_A)
    w0 = pl.multiple_of(w0, RADIUS)
    variant = jnp.where(j == 0, 0, jnp.where(j == nblk - 1, 2, 1))

    lane = lax.broadcasted_iota(jnp.int32, (Q_BLK_A, LANES), 1)
    low_half = lane < HEAD_DIM
    st = jnp.zeros((Q_BLK_A, LANES), jnp.float32)
    if final:
        st1, st2 = s1_ref[0], s2_ref[0]

    for g in range(N_HEADS_A // 2):
        cols = slice(g * LANES, (g + 1) * LANES)
        qg = q_ref[0, :, cols]
        kg = k_ref[0, pl.ds(w0, K_WIN_A), cols]
        vg = v_ref[0, pl.ds(w0, K_WIN_A), cols]
        halves = []
        for half in range(2):
            h = 2 * g + half
            keep = low_half if half == 0 else jnp.logical_not(low_half)
            qh = jnp.where(keep, qg, jnp.zeros_like(qg))
            s = lax.dot_general(qh, kg, (((1,), (1,)), ((), ())),
                                preferred_element_type=jnp.float32)
            s = s + bias_ref[variant, h]
            m = jnp.max(s, axis=-1, keepdims=True)
            p = jnp.exp(s - m)
            den = jnp.sum(p, axis=-1, keepdims=True)
            pv = jnp.dot(p.astype(jnp.bfloat16), vg,
                         preferred_element_type=jnp.float32)
            if final:
                m1 = st1[:, h:h + 1]
                d1 = st1[:, N_HEADS_A + h:N_HEADS_A + h + 1]
                m2 = st2[:, h:h + 1]
                d2 = st2[:, N_HEADS_A + h:N_HEADS_A + h + 1]
                m_all = jnp.maximum(jnp.maximum(m1, m2), m)
                w1 = jnp.exp(m1 - m_all)
                w2 = jnp.exp(m2 - m_all)
                w3 = jnp.exp(m - m_all)
                num = (w1 * n1_ref[0, :, cols].astype(jnp.float32)
                       + w2 * n2_ref[0, :, cols].astype(jnp.float32) + w3 * pv)
                den_all = w1 * d1 + w2 * d2 + w3 * den
                halves.append(num / den_all)
            else:
                halves.append(pv)
                st = jnp.where(lane == h, m, st)
                st = jnp.where(lane == N_HEADS_A + h, den, st)
        pair = jnp.where(low_half, halves[0], halves[1])
        if final:
            o_ref[0, :, cols] = pair.astype(o_ref.dtype)
        else:
            num_ref[0, :, cols] = pair.astype(num_ref.dtype)
    if not final:
        st_ref[0] = st


def _dilated_call(qa, ka, va, bias, dilation, seq, prev=None):
    batch = qa.shape[0]
    sub_len = seq // dilation
    view = lambda t, c: t.reshape(batch, sub_len, dilation * c)
    qv, kv, vv = view(qa, WIDTH_A), view(ka, WIDTH_A), view(va, WIDTH_A)
    final = prev is not None
    blk = lambda b, r, j: (b, j, r)
    whole = lambda b, r, j: (b, 0, r)
    in_specs = [
        pl.BlockSpec((1, Q_BLK_A, WIDTH_A), blk),
        pl.BlockSpec((1, sub_len, WIDTH_A), whole),
        pl.BlockSpec((1, sub_len, WIDTH_A), whole),
        pl.BlockSpec((3, N_HEADS_A, Q_BLK_A, K_WIN_A), lambda b, r, j: (0, 0, 0, 0),
                     pipeline_mode=pl.Buffered(1)),
    ]
    args = [qv, kv, vv, bias]
    num_spec = pl.BlockSpec((1, Q_BLK_A, WIDTH_A), blk)
    st_spec = pl.BlockSpec((1, Q_BLK_A, LANES), blk)
    if final:
        n1, s1, n2, s2 = prev
        in_specs += [num_spec, st_spec, num_spec, st_spec]
        args += [view(n1, WIDTH_A), view(s1, LANES), view(n2, WIDTH_A), view(s2, LANES)]
        out_specs = num_spec
        out_shape = jax.ShapeDtypeStruct((batch, sub_len, dilation * WIDTH_A), jnp.bfloat16)
    else:
        out_specs = [num_spec, st_spec]
        out_shape = [
            jax.ShapeDtypeStruct((batch, sub_len, dilation * WIDTH_A), jnp.bfloat16),
            jax.ShapeDtypeStruct((batch, sub_len, dilation * LANES), jnp.float32),
        ]
    out = pl.pallas_call(
        functools.partial(_dilated_kernel, sub_len=sub_len, final=final),
        grid=(batch, dilation, sub_len // Q_BLK_A),
        in_specs=in_specs,
        out_specs=out_specs,
        out_shape=out_shape,
        compiler_params=pltpu.CompilerParams(
            dimension_semantics=("arbitrary", "arbitrary", "arbitrary"),
            vmem_limit_bytes=VMEM_LIMIT_BYTES),
        name=f"dilated_d{dilation}",
    )(*args)
    if final:
        return out.reshape(batch, seq, WIDTH_A)
    num, st = out
    return num.reshape(batch, seq, WIDTH_A), st.reshape(batch, seq, LANES)


def _t5_buckets(rel):
    nb = N_BUCKETS // 2
    max_exact = nb // 2
    ret = (rel > 0).astype(np.int32) * nb
    n = np.abs(rel)
    large = max_exact + (np.log(np.maximum(n, 1) / max_exact)
                         / np.log(MAX_DISTANCE / max_exact) * (nb - max_exact)).astype(np.int32)
    large = np.minimum(large, nb - 1)
    return (ret + np.where(n < max_exact, n, large)).astype(np.int32)


def _dilated_bias(rel_bias, dilation):
    r = np.arange(Q_BLK_A)[:, None]
    c = np.arange(K_WIN_A)[None, :]
    tables = []
    for shift in (0, RADIUS, 2 * RADIUS):
        rel = c - shift - r
        valid = np.abs(rel) <= RADIUS
        b = rel_bias[jnp.asarray(_t5_buckets(rel * dilation))]
        b = jnp.transpose(b, (2, 0, 1)).astype(jnp.float32)
        tables.append(jnp.where(jnp.asarray(valid)[None], b, NEG_INF))
    return jnp.stack(tables)


def _mla_kernel(q_ref, k_ref, vt_ref, o_ref):
    outs = []
    for hh in range(2):
        q = q_ref[0, :, hh * HEAD_SLOT:(hh + 1) * HEAD_SLOT]
        k = k_ref[0, :, hh * HEAD_SLOT:(hh + 1) * HEAD_SLOT]
        st = lax.dot_general(k, q, (((1,), (1,)), ((), ())),
                             preferred_element_type=jnp.float32)
        m = jnp.max(st, axis=0, keepdims=True)
        pt = jnp.exp2(st - m).astype(jnp.bfloat16)
        ot = jnp.dot(vt_ref[0, hh * VT_ROWS:(hh + 1) * VT_ROWS, :], pt,
                     preferred_element_type=jnp.float32)
        outs.append(ot[:V_HEAD_DIM] / ot[VT_ONES_ROW:VT_ONES_ROW + 1])
    o_ref[0] = jnp.concatenate(outs, axis=0).T.astype(o_ref.dtype)


def _mla_call(qm, km, vt, seq):
    batch = vt.shape[0]
    tq = MLA_TQ
    return pl.pallas_call(
        _mla_kernel,
        grid=(batch, N_HEADS_B // 2, seq // tq),
        in_specs=[
            pl.BlockSpec((1, tq, 2 * HEAD_SLOT), lambda b, g, j: (b, j, g)),
            pl.BlockSpec((1, seq, 2 * HEAD_SLOT), lambda b, g, j: (b, 0, g)),
            pl.BlockSpec((1, 2 * VT_ROWS, seq), lambda b, g, j: (b, g, 0)),
        ],
        out_specs=pl.BlockSpec((1, tq, 2 * V_HEAD_DIM), lambda b, g, j: (b, j, g)),
        out_shape=jax.ShapeDtypeStruct((batch, seq, WIDTH_B), jnp.bfloat16),
        compiler_params=pltpu.CompilerParams(
            dimension_semantics=("arbitrary", "arbitrary", "arbitrary"),
            vmem_limit_bytes=VMEM_LIMIT_BYTES),
        name="mla",
    )(qm, km, vt)


def _mlp_kernel(x_ref, oa_ref, ob_ref, wout_ref, g_ref, wup_ref, wdown_ref, gf_ref, o_ref):
    o_cat = jnp.concatenate([oa_ref[...], ob_ref[...]], axis=-1)
    h = x_ref[...] + jnp.dot(o_cat, wout_ref[...], preferred_element_type=jnp.float32)
    u = _rms(h, g_ref[...]).astype(jnp.bfloat16)
    acc = h
    for c in range(D_FF // FF_CHUNK):
        a = jnp.dot(u, wup_ref[:, c * FF_CHUNK:(c + 1) * FF_CHUNK],
                    preferred_element_type=jnp.float32)
        a = jnp.square(jnp.maximum(a, 0.0)).astype(jnp.bfloat16)
        acc = acc + jnp.dot(a, wdown_ref[c * FF_CHUNK:(c + 1) * FF_CHUNK, :],
                            preferred_element_type=jnp.float32)
    o_ref[...] = _rms(acc, gf_ref[...])


def _mlp_call(x2, oa, ob, wout, g, wup, wdown, gf):
    tokens = x2.shape[0]
    tm = MLP_TM
    row = lambda i: (i, 0)
    const = lambda i: (0, 0)

    def wspec(shape):
        return pl.BlockSpec(shape, const, pipeline_mode=pl.Buffered(1))

    return pl.pallas_call(
        _mlp_kernel,
        grid=(tokens // tm,),
        in_specs=[
            pl.BlockSpec((tm, D_MODEL), row),
            pl.BlockSpec((tm, WIDTH_A), row),
            pl.BlockSpec((tm, WIDTH_B), row),
            wspec((WIDTH_A + WIDTH_B, D_MODEL)),
            wspec((1, D_MODEL)),
            wspec((D_MODEL, D_FF)),
            wspec((D_FF, D_MODEL)),
            wspec((1, D_MODEL)),
        ],
        out_specs=pl.BlockSpec((tm, D_MODEL), row),
        out_shape=jax.ShapeDtypeStruct((tokens, D_MODEL), jnp.float32),
        compiler_params=pltpu.CompilerParams(
            dimension_semantics=("arbitrary",), vmem_limit_bytes=VMEM_LIMIT_BYTES),
        name="mlp",
    )(x2, oa, ob, wout, g, wup, wdown, gf)


def _prep_in_weights(w_in):
    pad = jnp.zeros((D_MODEL, IN_COLS_PAD - w_in.shape[1]), w_in.dtype)
    return jnp.concatenate([w_in, pad], axis=1).astype(jnp.bfloat16)


def _prep_q_weights(w_q_b):
    dqk = QK_NOPE_DIM + QK_ROPE_DIM
    half = QK_ROPE_DIM // 2
    w = w_q_b.reshape(Q_LORA_RANK, N_HEADS_B, dqk)
    zeros = lambda n: jnp.zeros((Q_LORA_RANK, N_HEADS_B, n), w.dtype)
    nope, rope = w[..., :QK_NOPE_DIM], w[..., QK_NOPE_DIM:]
    plain = jnp.concatenate([nope, rope, zeros(HEAD_SLOT - dqk)], axis=-1)
    rot = jnp.concatenate([zeros(QK_NOPE_DIM), -rope[..., half:], rope[..., :half],
                           zeros(HEAD_SLOT - dqk)], axis=-1)
    out = jnp.concatenate([plain.reshape(Q_LORA_RANK, QK_WIDTH_B),
                           rot.reshape(Q_LORA_RANK, QK_WIDTH_B)], axis=1)
    return out.astype(jnp.bfloat16)


def _prep_kv_weights(w_kv_b):
    half = QK_ROPE_DIM // 2
    w = w_kv_b.reshape(KV_LORA_RANK, N_HEADS_B, QK_NOPE_DIM + V_HEAD_DIM)
    k_nope, v = w[..., :QK_NOPE_DIM], w[..., QK_NOPE_DIM:]
    eye = np.eye(QK_ROPE_DIM, dtype=np.float32)
    rot = np.zeros((QK_ROPE_DIM, QK_ROPE_DIM), np.float32)
    for jcol in range(half):
        rot[half + jcol, jcol] = -1.0
        rot[jcol, half + jcol] = 1.0

    def place(top, rope_block):
        blk = np.zeros((LAT_WIDTH - KV_LORA_RANK, N_HEADS_B, HEAD_SLOT), np.float32)
        blk[:QK_ROPE_DIM, :, QK_NOPE_DIM:QK_NOPE_DIM + QK_ROPE_DIM] = rope_block[:, None, :]
        top = jnp.concatenate(
            [top, jnp.zeros((KV_LORA_RANK, N_HEADS_B, HEAD_SLOT - QK_NOPE_DIM), top.dtype)], axis=-1)
        full = jnp.concatenate([top, jnp.asarray(blk)], axis=0)
        return full.reshape(LAT_WIDTH, QK_WIDTH_B)

    wk = jnp.concatenate([place(k_nope, eye), place(jnp.zeros_like(k_nope), rot)], axis=1)

    vt_top = jnp.transpose(v, (1, 2, 0))
    vt_top = jnp.concatenate(
        [vt_top, jnp.zeros((N_HEADS_B, VT_ROWS - V_HEAD_DIM, KV_LORA_RANK), v.dtype)], axis=1)
    ones_sel = np.zeros((N_HEADS_B, VT_ROWS, LAT_WIDTH - KV_LORA_RANK), np.float32)
    ones_sel[:, VT_ONES_ROW, -1] = 1.0
    wvt = jnp.concatenate([vt_top, jnp.asarray(ones_sel)], axis=2)
    return wk.astype(jnp.bfloat16), wvt.reshape(N_HEADS_B * VT_ROWS, LAT_WIDTH).astype(jnp.bfloat16)


def _rope_lane_tables(seq):
    inv_freq = ROPE_THETA ** (-jnp.arange(0, QK_ROPE_DIM, 2, dtype=jnp.float32) / QK_ROPE_DIM)
    pos = jnp.arange(seq, dtype=jnp.float32)
    freqs = pos[:, None] * inv_freq[None, :]
    cos, sin = jnp.cos(freqs), jnp.sin(freqs)
    ones = jnp.ones((seq, QK_NOPE_DIM), jnp.float32)
    zeros = lambda n: jnp.zeros((seq, n), jnp.float32)
    pad = HEAD_SLOT - QK_NOPE_DIM - QK_ROPE_DIM
    cos_t = jnp.concatenate([ones, cos, cos, zeros(pad)], axis=1)
    sin_t = jnp.concatenate([zeros(QK_NOPE_DIM), sin, sin, zeros(pad)], axis=1)
    q_scale = (QK_NOPE_DIM + QK_ROPE_DIM) ** -0.5 * math.log2(math.e)
    return cos_t * q_scale, sin_t * q_scale, cos_t, sin_t


def kernel(x, mix_norm_g, w_in, q_norm_g, w_q_b, kv_norm_g, w_kv_b, w_out,
           mlp_norm_g, w_up, w_down, rel_bias, final_norm_g):
    batch, seq, _ = x.shape
    depth = w_in.shape[0]
    assert depth == 1, "the final norm is fused into the single layer's MLP kernel"
    cq, sq, ck, sk = _rope_lane_tables(seq)
    biases = [_dilated_bias(rel_bias, d) for (_, d) in DILATED_PATTERNS]
    layer = 0
    x2 = x.reshape(batch * seq, D_MODEL)
    wk, wvt = _prep_kv_weights(w_kv_b[layer])
    qa, ka, va, qm, km, vt = _proj_call(
        x2, mix_norm_g[layer][None], _prep_in_weights(w_in[layer]),
        q_norm_g[layer][None], _prep_q_weights(w_q_b[layer]),
        kv_norm_g[layer][None], wk, wvt, cq, sq, ck, sk, batch, seq)

    shape_a = (batch, seq, WIDTH_A)
    qa, ka, va = qa.reshape(shape_a), ka.reshape(shape_a), va.reshape(shape_a)
    n1, s1 = _dilated_call(qa, ka, va, biases[0], DILATED_PATTERNS[0][1], seq)
    n2, s2 = _dilated_call(qa, ka, va, biases[1], DILATED_PATTERNS[1][1], seq)
    oa = _dilated_call(qa, ka, va, biases[2], DILATED_PATTERNS[2][1], seq, prev=(n1, s1, n2, s2))

    ob = _mla_call(qm.reshape(batch, seq, QK_WIDTH_B), km.reshape(batch, seq, QK_WIDTH_B), vt, seq)

    out = _mlp_call(
        x2, oa.reshape(batch * seq, WIDTH_A), ob.reshape(batch * seq, WIDTH_B),
        w_out[layer].astype(jnp.bfloat16), mlp_norm_g[layer][None],
        w_up[layer].astype(jnp.bfloat16), w_down[layer].astype(jnp.bfloat16),
        final_norm_g[None])
    return out.reshape(batch, seq, D_MODEL)
```

```python
import functools
import math

import jax
import jax.numpy as jnp
import numpy as np
from jax import lax
from jax.experimental import pallas as pl
from jax.experimental.pallas import tpu as pltpu

D_MODEL = 1024
HEAD_DIM = 64
N_HEADS_A = 8
DILATED_PATTERNS = ((128, 1), (512, 4), (2048, 16))
N_HEADS_B = 8
Q_LORA_RANK = 256
KV_LORA_RANK = 128
QK_NOPE_DIM = 64
QK_ROPE_DIM = 32
V_HEAD_DIM = 64
ROPE_THETA = 10000.0
N_BUCKETS = 32
MAX_DISTANCE = 1024
D_FF = 4 * D_MODEL
NORM_EPS = 1e-6
NEG_INF = -1e30
WIDTH_A = N_HEADS_A * HEAD_DIM
WIDTH_B = N_HEADS_B * V_HEAD_DIM

LANES = 128
VMEM_LIMIT_BYTES = 56 * 1024 * 1024

HEAD_SLOT = LANES
QK_WIDTH_B = N_HEADS_B * HEAD_SLOT
VT_ROWS = 80
VT_ONES_ROW = V_HEAD_DIM
IN_COLS_PAD = 2048
CQ_OFF = 3 * WIDTH_A
CKV_OFF = CQ_OFF + Q_LORA_RANK
LAT_WIDTH = 2 * LANES
RADIUS = 64
Q_BLK_A = 2 * RADIUS
K_WIN_A = 4 * RADIUS

PROJ_TM = 512
MLP_TM = 512
MLA_TQ = 256
MLA_TK = 512
FF_CHUNK = 1024


def _rms(xf, g):
    return xf * lax.rsqrt(jnp.mean(xf * xf, axis=-1, keepdims=True) + NORM_EPS) * g


def _proj_kernel(x_ref, g_ref, win_ref, qg_ref, wq_ref, kvg_ref, wk_ref, wvt_ref,
                 cq_ref, sq_ref, ck_ref, sk_ref,
                 qa_ref, ka_ref, va_ref, qm_ref, km_ref, vt_ref):
    x = x_ref[...]
    u = _rms(x, g_ref[...]).astype(jnp.bfloat16)
    proj = jnp.dot(u, win_ref[...], preferred_element_type=jnp.float32)

    qa_ref[...] = (proj[:, 0:WIDTH_A] * (HEAD_DIM ** -0.5)).astype(jnp.bfloat16)
    ka_ref[...] = proj[:, WIDTH_A:2 * WIDTH_A].astype(jnp.bfloat16)
    va_ref[...] = proj[:, 2 * WIDTH_A:3 * WIDTH_A].astype(jnp.bfloat16)

    cq = _rms(proj[:, CQ_OFF:CQ_OFF + Q_LORA_RANK], qg_ref[...]).astype(jnp.bfloat16)
    q2 = jnp.dot(cq, wq_ref[...], preferred_element_type=jnp.float32)
    cq_t = jnp.tile(cq_ref[...], (1, N_HEADS_B))
    sq_t = jnp.tile(sq_ref[...], (1, N_HEADS_B))
    qm_ref[...] = (q2[:, :QK_WIDTH_B] * cq_t + q2[:, QK_WIDTH_B:] * sq_t).astype(jnp.bfloat16)

    ckv = _rms(proj[:, CKV_OFF:CKV_OFF + KV_LORA_RANK], kvg_ref[...])
    hi = proj[:, CKV_OFF + KV_LORA_RANK:IN_COLS_PAD]
    lane = lax.broadcasted_iota(jnp.int32, hi.shape, 1)
    hi = jnp.where(lane == LANES - 1, 1.0, hi)
    lat = jnp.concatenate([ckv, hi], axis=-1).astype(jnp.bfloat16)
    k2 = jnp.dot(lat, wk_ref[...], preferred_element_type=jnp.float32)
    ck_t = jnp.tile(ck_ref[...], (1, N_HEADS_B))
    sk_t = jnp.tile(sk_ref[...], (1, N_HEADS_B))
    km_ref[...] = (k2[:, :QK_WIDTH_B] * ck_t + k2[:, QK_WIDTH_B:] * sk_t).astype(jnp.bfloat16)
    vt = lax.dot_general(wvt_ref[...], lat, (((1,), (1,)), ((), ())),
                         preferred_element_type=jnp.float32)
    vt_ref[0, 0] = vt.astype(jnp.bfloat16)


def _proj_call(x2, g, win, qg, wq, kvg, wk, wvt, cq, sq, ck, sk, batch, seq):
    tokens = x2.shape[0]
    tm = PROJ_TM
    assert tm == MLA_TK, "V^T is written in key chunks of the projection's token block"
    sblk = seq // tm
    row = lambda i: (i, 0)
    const = lambda i: (0, 0)
    pos = lambda i: (i % sblk, 0)

    def wspec(shape):
        return pl.BlockSpec(shape, const, pipeline_mode=pl.Buffered(1))

    bf = jnp.bfloat16
    return pl.pallas_call(
        _proj_kernel,
        grid=(tokens // tm,),
        in_specs=[
            pl.BlockSpec((tm, D_MODEL), row),
            wspec((1, D_MODEL)),
            wspec((D_MODEL, IN_COLS_PAD)),
            wspec((1, Q_LORA_RANK)),
            wspec((Q_LORA_RANK, 2 * QK_WIDTH_B)),
            wspec((1, KV_LORA_RANK)),
            wspec((LAT_WIDTH, 2 * QK_WIDTH_B)),
            wspec((N_HEADS_B * VT_ROWS, LAT_WIDTH)),
            pl.BlockSpec((tm, HEAD_SLOT), pos),
            pl.BlockSpec((tm, HEAD_SLOT), pos),
            pl.BlockSpec((tm, HEAD_SLOT), pos),
            pl.BlockSpec((tm, HEAD_SLOT), pos),
        ],
        out_specs=[
            pl.BlockSpec((tm, WIDTH_A), row),
            pl.BlockSpec((tm, WIDTH_A), row),
            pl.BlockSpec((tm, WIDTH_A), row),
            pl.BlockSpec((tm, QK_WIDTH_B), row),
            pl.BlockSpec((tm, QK_WIDTH_B), row),
            pl.BlockSpec((1, 1, N_HEADS_B * VT_ROWS, tm), lambda i: (i // sblk, i % sblk, 0, 0)),
        ],
        out_shape=[
            jax.ShapeDtypeStruct((tokens, WIDTH_A), bf),
            jax.ShapeDtypeStruct((tokens, WIDTH_A), bf),
            jax.ShapeDtypeStruct((tokens, WIDTH_A), bf),
            jax.ShapeDtypeStruct((tokens, QK_WIDTH_B), bf),
            jax.ShapeDtypeStruct((tokens, QK_WIDTH_B), bf),
            jax.ShapeDtypeStruct((batch, sblk, N_HEADS_B * VT_ROWS, tm), bf),
        ],
        compiler_params=pltpu.CompilerParams(
            dimension_semantics=("arbitrary",), vmem_limit_bytes=VMEM_LIMIT_BYTES),
        name="proj",
    )(x2, g, win, qg, wq, kvg, wk, wvt, cq, sq, ck, sk)


def _dilated_kernel(*refs, sub_len, final):
    if final:
        (q_ref, k_ref, v_ref, bias_ref, n1_ref, s1_ref, n2_ref, s2_ref, o_ref) = refs
    else:
        (q_ref, k_ref, v_ref, bias_ref, num_ref, st_ref) = refs
    j = pl.program_id(2)
    nblk = sub_len // Q_BLK_A
    w0 = jnp.clip(j * Q_BLK_A - RADIUS, 0, sub_len - K_WIN_A)
    w0 = pl.multiple_of(w0, RADIUS)
    variant = jnp.where(j == 0, 0, jnp.where(j == nblk - 1, 2, 1))

    lane = lax.broadcasted_iota(jnp.int32, (Q_BLK_A, LANES), 1)
    low_half = lane < HEAD_DIM
    st = jnp.zeros((Q_BLK_A, LANES), jnp.float32)
    if final:
        st1, st2 = s1_ref[0], s2_ref[0]

    for g in range(N_HEADS_A // 2):
        cols = slice(g * LANES, (g + 1) * LANES)
        qg = q_ref[0, :, cols]
        kg = k_ref[0, pl.ds(w0, K_WIN_A), cols]
        vg = v_ref[0, pl.ds(w0, K_WIN_A), cols]
        halves = []
        for half in range(2):
            h = 2 * g + half
            keep = low_half if half == 0 else jnp.logical_not(low_half)
            qh = jnp.where(keep, qg, jnp.zeros_like(qg))
            s = lax.dot_general(qh, kg, (((1,), (1,)), ((), ())),
                                preferred_element_type=jnp.float32)
            s = s + bias_ref[variant, h]
            m = jnp.max(s, axis=-1, keepdims=True)
            p = jnp.exp(s - m)
            den = jnp.sum(p, axis=-1, keepdims=True)
            pv = jnp.dot(p.astype(jnp.bfloat16), vg,
                         preferred_element_type=jnp.float32)
            if final:
                m1 = st1[:, h:h + 1]
                d1 = st1[:, N_HEADS_A + h:N_HEADS_A + h + 1]
                m2 = st2[:, h:h + 1]
                d2 = st2[:, N_HEADS_A + h:N_HEADS_A + h + 1]
                m_all = jnp.maximum(jnp.maximum(m1, m2), m)
                w1 = jnp.exp(m1 - m_all)
                w2 = jnp.exp(m2 - m_all)
                w3 = jnp.exp(m - m_all)
                num = (w1 * n1_ref[0, :, cols].astype(jnp.float32)
                       + w2 * n2_ref[0, :, cols].astype(jnp.float32) + w3 * pv)
                den_all = w1 * d1 + w2 * d2 + w3 * den
                halves.append(num / den_all)
            else:
                halves.append(pv)
                st = jnp.where(lane == h, m, st)
                st = jnp.where(lane == N_HEADS_A + h, den, st)
        pair = jnp.where(low_half, halves[0], halves[1])
        if final:
            o_ref[0, :, cols] = pair.astype(o_ref.dtype)
        else:
            num_ref[0, :, cols] = pair.astype(num_ref.dtype)
    if not final:
        st_ref[0] = st


def _dilated_call(qa, ka, va, bias, dilation, seq, prev=None):
    batch = qa.shape[0]
    sub_len = seq // dilation
    view = lambda t, c: t.reshape(batch, sub_len, dilation * c)
    qv, kv, vv = view(qa, WIDTH_A), view(ka, WIDTH_A), view(va, WIDTH_A)
    final = prev is not None
    blk = lambda b, r, j: (b, j, r)
    whole = lambda b, r, j: (b, 0, r)
    in_specs = [
        pl.BlockSpec((1, Q_BLK_A, WIDTH_A), blk),
        pl.BlockSpec((1, sub_len, WIDTH_A), whole),
        pl.BlockSpec((1, sub_len, WIDTH_A), whole),
        pl.BlockSpec((3, N_HEADS_A, Q_BLK_A, K_WIN_A), lambda b, r, j: (0, 0, 0, 0),
                     pipeline_mode=pl.Buffered(1)),
    ]
    args = [qv, kv, vv, bias]
    num_spec = pl.BlockSpec((1, Q_BLK_A, WIDTH_A), blk)
    st_spec = pl.BlockSpec((1, Q_BLK_A, LANES), blk)
    if final:
        n1, s1, n2, s2 = prev
        in_specs += [num_spec, st_spec, num_spec, st_spec]
        args += [view(n1, WIDTH_A), view(s1, LANES), view(n2, WIDTH_A), view(s2, LANES)]
        out_specs = num_spec
        out_shape = jax.ShapeDtypeStruct((batch, sub_len, dilation * WIDTH_A), jnp.bfloat16)
    else:
        out_specs = [num_spec, st_spec]
        out_shape = [
            jax.ShapeDtypeStruct((batch, sub_len, dilation * WIDTH_A), jnp.bfloat16),
            jax.ShapeDtypeStruct((batch, sub_len, dilation * LANES), jnp.float32),
        ]
    out = pl.pallas_call(
        functools.partial(_dilated_kernel, sub_len=sub_len, final=final),
        grid=(batch, dilation, sub_len // Q_BLK_A),
        in_specs=in_specs,
        out_specs=out_specs,
        out_shape=out_shape,
        compiler_params=pltpu.CompilerParams(
            dimension_semantics=("arbitrary", "arbitrary", "arbitrary"),
            vmem_limit_bytes=VMEM_LIMIT_BYTES),
        name=f"dilated_d{dilation}",
    )(*args)
    if final:
        return out.reshape(batch, seq, WIDTH_A)
    num, st = out
    return num.reshape(batch, seq, WIDTH_A), st.reshape(batch, seq, LANES)


def _t5_buckets(rel):
    nb = N_BUCKETS // 2
    max_exact = nb // 2
    ret = (rel > 0).astype(np.int32) * nb
    n = np.abs(rel)
    large = max_exact + (np.log(np.maximum(n, 1) / max_exact)
                         / np.log(MAX_DISTANCE / max_exact) * (nb - max_exact)).astype(np.int32)
    large = np.minimum(large, nb - 1)
    return (ret + np.where(n < max_exact, n, large)).astype(np.int32)


def _dilated_bias(rel_bias, dilation):
    r = np.arange(Q_BLK_A)[:, None]
    c = np.arange(K_WIN_A)[None, :]
    rel = np.stack([c - shift - r for shift in (0, RADIUS, 2 * RADIUS)])
    valid = np.abs(rel) <= RADIUS
    buckets = jnp.asarray(_t5_buckets(rel * dilation))
    onehot = (buckets[None] == jnp.arange(N_BUCKETS)[:, None, None, None]).astype(jnp.float32)
    b = jnp.einsum('nh,nvrc->vhrc', rel_bias.astype(jnp.float32), onehot,
                   precision=lax.Precision.HIGHEST)
    return jnp.where(jnp.asarray(valid)[:, None], b, NEG_INF)


def _mla_kernel(q_ref, k_ref, vt_ref, o_ref, s_even, s_odd, mc_even, mc_odd):
    nk = vt_ref.shape[1]
    nq = q_ref.shape[1] // MLA_TQ
    heads = range(2)
    bufs = ((s_even, mc_even), (s_odd, mc_odd))
    assert nk % 2 == 0, "chunk parity must restart at every query block"

    def scores(jq, c, parity):
        s_buf, mc_buf = bufs[parity]
        q_rows = pl.ds(pl.multiple_of(jq * MLA_TQ, MLA_TQ), MLA_TQ)
        for hh in heads:
            k = k_ref[0, c * MLA_TK:(c + 1) * MLA_TK, hh * HEAD_SLOT:(hh + 1) * HEAD_SLOT]
            q = q_ref[0, q_rows, hh * HEAD_SLOT:(hh + 1) * HEAD_SLOT]
            s = lax.dot_general(k, q, (((1,), (1,)), ((), ())),
                                preferred_element_type=jnp.float32)
            s_buf[hh] = s
            mc_buf[hh] = jnp.max(s, axis=0, keepdims=True)

    scores(0, 0, 0)

    def q_block(jq, carry):
        jq_next = jnp.minimum(jq + 1, nq - 1)
        m = [None, None]
        acc = [None, None]
        for c in range(nk):
            if c + 1 < nk:
                scores(jq, c + 1, (c + 1) % 2)
            else:
                scores(jq_next, 0, 0)
            s_buf, mc_buf = bufs[c % 2]
            for hh in heads:
                mc = mc_buf[hh]
                m_new = mc if c == 0 else jnp.maximum(m[hh], mc)
                pt = jnp.exp2(s_buf[hh] - m_new).astype(jnp.bfloat16)
                pv = jnp.dot(vt_ref[0, c, hh * VT_ROWS:(hh + 1) * VT_ROWS, :], pt,
                             preferred_element_type=jnp.float32)
                acc[hh] = pv if c == 0 else jnp.exp2(m[hh] - m_new) * acc[hh] + pv
                m[hh] = m_new
        outs = [a[:V_HEAD_DIM] / a[VT_ONES_ROW:VT_ONES_ROW + 1] for a in acc]
        o_rows = pl.ds(pl.multiple_of(jq * MLA_TQ, MLA_TQ), MLA_TQ)
        o_ref[0, o_rows, :] = jnp.concatenate(outs, axis=0).T.astype(o_ref.dtype)
        return carry

    lax.fori_loop(0, nq, q_block, 0)


def _mla_call(qm, km, vt, seq):
    batch, nk = vt.shape[0], vt.shape[1]
    pair = lambda b, g: (b, 0, g)
    return pl.pallas_call(
        _mla_kernel,
        grid=(batch, N_HEADS_B // 2),
        in_specs=[
            pl.BlockSpec((1, seq, 2 * HEAD_SLOT), pair),
            pl.BlockSpec((1, seq, 2 * HEAD_SLOT), pair),
            pl.BlockSpec((1, nk, 2 * VT_ROWS, MLA_TK), lambda b, g: (b, 0, g, 0)),
        ],
        out_specs=pl.BlockSpec((1, seq, 2 * V_HEAD_DIM), pair),
        out_shape=jax.ShapeDtypeStruct((batch, seq, WIDTH_B), jnp.bfloat16),
        scratch_shapes=[
            pltpu.VMEM((2, MLA_TK, MLA_TQ), jnp.float32),
            pltpu.VMEM((2, MLA_TK, MLA_TQ), jnp.float32),
            pltpu.VMEM((2, 1, MLA_TQ), jnp.float32),
            pltpu.VMEM((2, 1, MLA_TQ), jnp.float32),
        ],
        compiler_params=pltpu.CompilerParams(
            dimension_semantics=("arbitrary", "arbitrary"),
            vmem_limit_bytes=VMEM_LIMIT_BYTES),
        name="mla",
    )(qm, km, vt)


def _mlp_kernel(x_ref, oa_ref, ob_ref, wout_ref, g_ref, wup_ref, wdown_ref, gf_ref, o_ref):
    o_cat = jnp.concatenate([oa_ref[...], ob_ref[...]], axis=-1)
    h = x_ref[...] + jnp.dot(o_cat, wout_ref[...], preferred_element_type=jnp.float32)
    u = _rms(h, g_ref[...]).astype(jnp.bfloat16)
    acc = h
    for c in range(D_FF // FF_CHUNK):
        a = jnp.dot(u, wup_ref[:, c * FF_CHUNK:(c + 1) * FF_CHUNK],
                    preferred_element_type=jnp.float32)
        a = jnp.square(jnp.maximum(a, 0.0)).astype(jnp.bfloat16)
        acc = acc + jnp.dot(a, wdown_ref[c * FF_CHUNK:(c + 1) * FF_CHUNK, :],
                            preferred_element_type=jnp.float32)
    o_ref[...] = _rms(acc, gf_ref[...])


def _mlp_call(x2, oa, ob, wout, g, wup, wdown, gf):
    tokens = x2.shape[0]
    tm = MLP_TM
    row = lambda i: (i, 0)
    const = lambda i: (0, 0)

    def wspec(shape):
        return pl.BlockSpec(shape, const, pipeline_mode=pl.Buffered(1))

    return pl.pallas_call(
        _mlp_kernel,
        grid=(tokens // tm,),
        in_specs=[
            pl.BlockSpec((tm, D_MODEL), row),
            pl.BlockSpec((tm, WIDTH_A), row),
            pl.BlockSpec((tm, WIDTH_B), row),
            wspec((WIDTH_A + WIDTH_B, D_MODEL)),
            wspec((1, D_MODEL)),
            wspec((D_MODEL, D_FF)),
            wspec((D_FF, D_MODEL)),
            wspec((1, D_MODEL)),
        ],
        out_specs=pl.BlockSpec((tm, D_MODEL), row),
        out_shape=jax.ShapeDtypeStruct((tokens, D_MODEL), jnp.float32),
        compiler_params=pltpu.CompilerParams(
            dimension_semantics=("arbitrary",), vmem_limit_bytes=VMEM_LIMIT_BYTES),
        name="mlp",
    )(x2, oa, ob, wout, g, wup, wdown, gf)


def _prep_in_weights(w_in):
    pad = jnp.zeros((D_MODEL, IN_COLS_PAD - w_in.shape[1]), w_in.dtype)
    return jnp.concatenate([w_in, pad], axis=1).astype(jnp.bfloat16)


def _prep_q_weights(w_q_b):
    dqk = QK_NOPE_DIM + QK_ROPE_DIM
    half = QK_ROPE_DIM // 2
    w = w_q_b.reshape(Q_LORA_RANK, N_HEADS_B, dqk)
    zeros = lambda n: jnp.zeros((Q_LORA_RANK, N_HEADS_B, n), w.dtype)
    nope, rope = w[..., :QK_NOPE_DIM], w[..., QK_NOPE_DIM:]
    plain = jnp.concatenate([nope, rope, zeros(HEAD_SLOT - dqk)], axis=-1)
    rot = jnp.concatenate([zeros(QK_NOPE_DIM), -rope[..., half:], rope[..., :half],
                           zeros(HEAD_SLOT - dqk)], axis=-1)
    out = jnp.concatenate([plain.reshape(Q_LORA_RANK, QK_WIDTH_B),
                           rot.reshape(Q_LORA_RANK, QK_WIDTH_B)], axis=1)
    return out.astype(jnp.bfloat16)


def _prep_kv_weights(w_kv_b):
    half = QK_ROPE_DIM // 2
    w = w_kv_b.reshape(KV_LORA_RANK, N_HEADS_B, QK_NOPE_DIM + V_HEAD_DIM)
    k_nope, v = w[..., :QK_NOPE_DIM], w[..., QK_NOPE_DIM:]
    eye = np.eye(QK_ROPE_DIM, dtype=np.float32)
    rot = np.zeros((QK_ROPE_DIM, QK_ROPE_DIM), np.float32)
    for jcol in range(half):
        rot[half + jcol, jcol] = -1.0
        rot[jcol, half + jcol] = 1.0

    def place(top, rope_block):
        blk = np.zeros((LAT_WIDTH - KV_LORA_RANK, N_HEADS_B, HEAD_SLOT), np.float32)
        blk[:QK_ROPE_DIM, :, QK_NOPE_DIM:QK_NOPE_DIM + QK_ROPE_DIM] = rope_block[:, None, :]
        top = jnp.concatenate(
            [top, jnp.zeros((KV_LORA_RANK, N_HEADS_B, HEAD_SLOT - QK_NOPE_DIM), top.dtype)], axis=-1)
        full = jnp.concatenate([top, jnp.asarray(blk)], axis=0)
        return full.reshape(LAT_WIDTH, QK_WIDTH_B)

    wk = jnp.concatenate([place(k_nope, eye), place(jnp.zeros_like(k_nope), rot)], axis=1)

    vt_top = jnp.transpose(v, (1, 2, 0))
    vt_top = jnp.concatenate(
        [vt_top, jnp.zeros((N_HEADS_B, VT_ROWS - V_HEAD_DIM, KV_LORA_RANK), v.dtype)], axis=1)
    ones_sel = np.zeros((N_HEADS_B, VT_ROWS, LAT_WIDTH - KV_LORA_RANK), np.float32)
    ones_sel[:, VT_ONES_ROW, -1] = 1.0
    wvt = jnp.concatenate([vt_top, jnp.asarray(ones_sel)], axis=2)
    return wk.astype(jnp.bfloat16), wvt.reshape(N_HEADS_B * VT_ROWS, LAT_WIDTH).astype(jnp.bfloat16)


def _rope_lane_tables(seq):
    inv_freq = ROPE_THETA ** (-jnp.arange(0, QK_ROPE_DIM, 2, dtype=jnp.float32) / QK_ROPE_DIM)
    pos = jnp.arange(seq, dtype=jnp.float32)
    freqs = pos[:, None] * inv_freq[None, :]
    cos, sin = jnp.cos(freqs), jnp.sin(freqs)
    ones = jnp.ones((seq, QK_NOPE_DIM), jnp.float32)
    zeros = lambda n: jnp.zeros((seq, n), jnp.float32)
    pad = HEAD_SLOT - QK_NOPE_DIM - QK_ROPE_DIM
    cos_t = jnp.concatenate([ones, cos, cos, zeros(pad)], axis=1)
    sin_t = jnp.concatenate([zeros(QK_NOPE_DIM), sin, sin, zeros(pad)], axis=1)
    q_scale = (QK_NOPE_DIM + QK_ROPE_DIM) ** -0.5 * math.log2(math.e)
    return cos_t * q_scale, sin_t * q_scale, cos_t, sin_t


def kernel(x, mix_norm_g, w_in, q_norm_g, w_q_b, kv_norm_g, w_kv_b, w_out,
           mlp_norm_g, w_up, w_down, rel_bias, final_norm_g):
    batch, seq, _ = x.shape
    depth = w_in.shape[0]
    assert depth == 1, "the final norm is fused into the single layer's MLP kernel"
    cq, sq, ck, sk = _rope_lane_tables(seq)
    biases = [_dilated_bias(rel_bias, d) for (_, d) in DILATED_PATTERNS]
    layer = 0
    x2 = x.reshape(batch * seq, D_MODEL)
    wk, wvt = _prep_kv_weights(w_kv_b[layer])
    qa, ka, va, qm, km, vt = _proj_call(
        x2, mix_norm_g[layer][None], _prep_in_weights(w_in[layer]),
        q_norm_g[layer][None], _prep_q_weights(w_q_b[layer]),
        kv_norm_g[layer][None], wk, wvt, cq, sq, ck, sk, batch, seq)

    shape_a = (batch, seq, WIDTH_A)
    qa, ka, va = qa.reshape(shape_a), ka.reshape(shape_a), va.reshape(shape_a)
    n1, s1 = _dilated_call(qa, ka, va, biases[0], DILATED_PATTERNS[0][1], seq)
    n2, s2 = _dilated_call(qa, ka, va, biases[1], DILATED_PATTERNS[1][1], seq)
    oa = _dilated_call(qa, ka, va, biases[2], DILATED_PATTERNS[2][1], seq, prev=(n1, s1, n2, s2))

    ob = _mla_call(qm.reshape(batch, seq, QK_WIDTH_B), km.reshape(batch, seq, QK_WIDTH_B), vt, seq)

    out = _mlp_call(
        x2, oa.reshape(batch * seq, WIDTH_A), ob.reshape(batch * seq, WIDTH_B),
        w_out[layer].astype(jnp.bfloat16), mlp_norm_g[layer][None],
        w_up[layer].astype(jnp.bfloat16), w_down[layer].astype(jnp.bfloat16),
        final_norm_g[None])
    return out.reshape(batch, seq, D_MODEL)
```

```python
import functools
import math

import jax
import jax.numpy as jnp
import numpy as np
from jax import lax
from jax.experimental import pallas as pl
from jax.experimental.pallas import tpu as pltpu

D_MODEL = 1024
HEAD_DIM = 64
N_HEADS_A = 8
DILATED_PATTERNS = ((128, 1), (512, 4), (2048, 16))
N_HEADS_B = 8
Q_LORA_RANK = 256
KV_LORA_RANK = 128
QK_NOPE_DIM = 64
QK_ROPE_DIM = 32
V_HEAD_DIM = 64
ROPE_THETA = 10000.0
N_BUCKETS = 32
MAX_DISTANCE = 1024
D_FF = 4 * D_MODEL
NORM_EPS = 1e-6
NEG_INF = -1e30
WIDTH_A = N_HEADS_A * HEAD_DIM
WIDTH_B = N_HEADS_B * V_HEAD_DIM

LANES = 128
VMEM_LIMIT_BYTES = 56 * 1024 * 1024

HEAD_SLOT = LANES
QK_WIDTH_B = N_HEADS_B * HEAD_SLOT
VT_ROWS = 80
VT_ONES_ROW = V_HEAD_DIM
IN_COLS_PAD = 2048
CQ_OFF = 3 * WIDTH_A
CKV_OFF = CQ_OFF + Q_LORA_RANK
LAT_WIDTH = 2 * LANES
RADIUS = 64
Q_BLK_A = 2 * RADIUS
K_WIN_A = 4 * RADIUS

PROJ_TM = 512
MLP_TM = 512
MLA_TQ = 256
MLA_TK = 512
FF_CHUNK = 1024


def _rms(xf, g):
    return xf * lax.rsqrt(jnp.mean(xf * xf, axis=-1, keepdims=True) + NORM_EPS) * g


def _proj_kernel(x_ref, g_ref, win_ref, qg_ref, wq_ref, kvg_ref, wk_ref, wvt_ref,
                 cq_ref, sq_ref, ck_ref, sk_ref,
                 qa_ref, ka_ref, va_ref, qa4_ref, ka4_ref, va4_ref, qa16_ref, ka16_ref, va16_ref,
                 qm_ref, km_ref, vt_ref, slab_scr):
    x = x_ref[...]
    u = _rms(x, g_ref[...]).astype(jnp.bfloat16)
    proj = jnp.dot(u, win_ref[...], preferred_element_type=jnp.float32)

    tm = x.shape[0]
    scales = (HEAD_DIM ** -0.5 * math.log2(math.e), 1.0, 1.0)
    groups = ((qa_ref, qa4_ref, qa16_ref), (ka_ref, ka4_ref, ka16_ref), (va_ref, va4_ref, va16_ref))
    for a, (nat_ref, *strided_refs) in enumerate(groups):
        t = proj[:, a * WIDTH_A:(a + 1) * WIDTH_A] * scales[a]
        nat_ref[...] = t.astype(jnp.bfloat16)
        for g in range(WIDTH_A // LANES):
            slab_scr[g] = t[:, g * LANES:(g + 1) * LANES]
        for ref in strided_refs:
            d = ref.shape[1]
            for g in range(WIDTH_A // LANES):
                for r in range(d):
                    ref[0, r, :, g * LANES:(g + 1) * LANES] = (
                        slab_scr[g, pl.ds(r, tm // d, stride=d), :].astype(jnp.bfloat16))

    cq = _rms(proj[:, CQ_OFF:CQ_OFF + Q_LORA_RANK], qg_ref[...]).astype(jnp.bfloat16)
    q2 = jnp.dot(cq, wq_ref[...], preferred_element_type=jnp.float32)
    cq_t = jnp.tile(cq_ref[...], (1, N_HEADS_B))
    sq_t = jnp.tile(sq_ref[...], (1, N_HEADS_B))
    qm_ref[...] = (q2[:, :QK_WIDTH_B] * cq_t + q2[:, QK_WIDTH_B:] * sq_t).astype(jnp.bfloat16)

    ckv = _rms(proj[:, CKV_OFF:CKV_OFF + KV_LORA_RANK], kvg_ref[...])
    hi = proj[:, CKV_OFF + KV_LORA_RANK:IN_COLS_PAD]
    lane = lax.broadcasted_iota(jnp.int32, hi.shape, 1)
    hi = jnp.where(lane == LANES - 1, 1.0, hi)
    lat = jnp.concatenate([ckv, hi], axis=-1).astype(jnp.bfloat16)
    k2 = jnp.dot(lat, wk_ref[...], preferred_element_type=jnp.float32)
    ck_t = jnp.tile(ck_ref[...], (1, N_HEADS_B))
    sk_t = jnp.tile(sk_ref[...], (1, N_HEADS_B))
    km_ref[...] = (k2[:, :QK_WIDTH_B] * ck_t + k2[:, QK_WIDTH_B:] * sk_t).astype(jnp.bfloat16)
    vt = lax.dot_general(wvt_ref[...], lat, (((1,), (1,)), ((), ())),
                         preferred_element_type=jnp.float32)
    vt_ref[0, 0] = vt.astype(jnp.bfloat16)


def _proj_call(x2, g, win, qg, wq, kvg, wk, wvt, cq, sq, ck, sk, batch, seq):
    tokens = x2.shape[0]
    tm = PROJ_TM
    assert tm == MLA_TK, "V^T is written in key chunks of the projection's token block"
    sblk = seq // tm
    row = lambda i: (i, 0)
    const = lambda i: (0, 0)
    pos = lambda i: (i % sblk, 0)

    def wspec(shape):
        return pl.BlockSpec(shape, const, pipeline_mode=pl.Buffered(1))

    strides = [d for (_, d) in DILATED_PATTERNS if d > 1]

    def strided_spec(d):
        return pl.BlockSpec((1, d, tm // d, WIDTH_A), lambda i: (i // sblk, 0, i % sblk, 0))

    bf = jnp.bfloat16
    return pl.pallas_call(
        _proj_kernel,
        grid=(tokens // tm,),
        in_specs=[
            pl.BlockSpec((tm, D_MODEL), row),
            wspec((1, D_MODEL)),
            wspec((D_MODEL, IN_COLS_PAD)),
            wspec((1, Q_LORA_RANK)),
            wspec((Q_LORA_RANK, 2 * QK_WIDTH_B)),
            wspec((1, KV_LORA_RANK)),
            wspec((LAT_WIDTH, 2 * QK_WIDTH_B)),
            wspec((N_HEADS_B * VT_ROWS, LAT_WIDTH)),
            pl.BlockSpec((tm, HEAD_SLOT), pos),
            pl.BlockSpec((tm, HEAD_SLOT), pos),
            pl.BlockSpec((tm, HEAD_SLOT), pos),
            pl.BlockSpec((tm, HEAD_SLOT), pos),
        ],
        out_specs=[
            pl.BlockSpec((tm, WIDTH_A), row),
            pl.BlockSpec((tm, WIDTH_A), row),
            pl.BlockSpec((tm, WIDTH_A), row),
            *[strided_spec(d) for d in strides for _ in range(3)],
            pl.BlockSpec((tm, QK_WIDTH_B), row),
            pl.BlockSpec((tm, QK_WIDTH_B), row),
            pl.BlockSpec((1, 1, N_HEADS_B * VT_ROWS, tm), lambda i: (i // sblk, i % sblk, 0, 0)),
        ],
        out_shape=[
            jax.ShapeDtypeStruct((tokens, WIDTH_A), bf),
            jax.ShapeDtypeStruct((tokens, WIDTH_A), bf),
            jax.ShapeDtypeStruct((tokens, WIDTH_A), bf),
            *[jax.ShapeDtypeStruct((batch, d, seq // d, WIDTH_A), bf) for d in strides for _ in range(3)],
            jax.ShapeDtypeStruct((tokens, QK_WIDTH_B), bf),
            jax.ShapeDtypeStruct((tokens, QK_WIDTH_B), bf),
            jax.ShapeDtypeStruct((batch, sblk, N_HEADS_B * VT_ROWS, tm), bf),
        ],
        scratch_shapes=[pltpu.VMEM((WIDTH_A // LANES, tm, LANES), jnp.float32)],
        compiler_params=pltpu.CompilerParams(
            dimension_semantics=("arbitrary",), vmem_limit_bytes=VMEM_LIMIT_BYTES),
        name="proj",
    )(x2, g, win, qg, wq, kvg, wk, wvt, cq, sq, ck, sk)


def _dilated_kernel(*refs, dilation, final):
    if final:
        (q_ref, k_ref, v_ref, bias_ref, o2_ref, l2_ref, o3_ref, l3_ref, out_ref, s_even, s_odd) = refs
    else:
        (q_ref, k_ref, v_ref, bias_ref, o_ref, l_ref, s_even, s_odd) = refs
    j = pl.program_id(1)
    sub_len = k_ref.shape[2]
    nblk = sub_len // Q_BLK_A
    w0 = pl.multiple_of(jnp.clip(j * Q_BLK_A - RADIUS, 0, sub_len - K_WIN_A), RADIUS)
    win = pl.ds(w0, K_WIN_A)
    variant = jnp.where(j == 0, 0, jnp.where(j == nblk - 1, 2, 1))
    lane = lax.broadcasted_iota(jnp.int32, (Q_BLK_A, LANES), 1)
    low_half = lane < HEAD_DIM
    s_bufs = (s_even, s_odd)

    def residue(r):
        def scores(h):
            g, half = divmod(h, 2)
            cols = slice(g * LANES, (g + 1) * LANES)
            qg = q_ref[0, r, :, cols]
            keep = low_half if half == 0 else jnp.logical_not(low_half)
            qh = jnp.where(keep, qg, jnp.zeros_like(qg))
            s = lax.dot_general(qh, k_ref[0, r, win, cols], (((1,), (1,)), ((), ())),
                                preferred_element_type=jnp.float32)
            s = s + bias_ref[variant, h]
            s_bufs[h % 2][...] = s
            return jnp.max(s, axis=-1, keepdims=True)

        def attend(h, m):
            cols = slice((h // 2) * LANES, (h // 2 + 1) * LANES)
            p = jnp.exp2(s_bufs[h % 2][...] - m)
            den = jnp.sum(p, axis=-1, keepdims=True)
            pv = jnp.dot(p.astype(jnp.bfloat16), v_ref[0, r, win, cols],
                         preferred_element_type=jnp.float32)
            return pv / den, m + jnp.log2(den)

        rows = pl.ds(r, Q_BLK_A, stride=dilation) if dilation > 1 else slice(None)
        m_next = scores(0)
        halves = []
        for h in range(N_HEADS_A):
            m_cur = m_next
            if h + 1 < N_HEADS_A:
                m_next = scores(h + 1)
            halves.append(attend(h, m_cur))
            if h % 2 == 0:
                continue
            g = h // 2
            (o_lo, l_lo), (o_hi, l_hi) = halves
            halves = []
            o_pair = jnp.where(low_half, o_lo, o_hi)
            l_pair = jnp.where(low_half, l_lo, l_hi)
            if final:
                l2, l3 = l2_ref[0, g], l3_ref[0, g]
                top = jnp.maximum(jnp.maximum(l_pair, l2), l3)
                w1, w2, w3 = jnp.exp2(l_pair - top), jnp.exp2(l2 - top), jnp.exp2(l3 - top)
                merged = (w1 * o_pair + w2 * o2_ref[0, g] + w3 * o3_ref[0, g]) / (w1 + w2 + w3)
                out_ref[0, :, g * LANES:(g + 1) * LANES] = merged.astype(out_ref.dtype)
            else:
                o_ref[0, g, rows, :] = o_pair
                l_ref[0, g, rows, :] = l_pair

    if dilation == 1:
        residue(0)
    else:
        def body(r, carry):
            residue(r)
            return carry
        lax.fori_loop(0, dilation, body, 0)


def _dilated_call(q, k, v, bias, prev=None):
    batch, dilation, sub_len, _ = q.shape
    seq = sub_len * dilation
    final = prev is not None
    tokens = Q_BLK_A * dilation
    n_pairs = N_HEADS_A // 2
    whole = pl.BlockSpec((1, dilation, sub_len, WIDTH_A), lambda b, j: (b, 0, 0, 0))
    in_specs = [
        pl.BlockSpec((1, dilation, Q_BLK_A, WIDTH_A), lambda b, j: (b, 0, j, 0)),
        whole,
        whole,
        pl.BlockSpec((3, N_HEADS_A, Q_BLK_A, K_WIN_A), lambda b, j: (0, 0, 0, 0),
                     pipeline_mode=pl.Buffered(1)),
    ]
    args = [q, k, v, bias]
    slab_spec = pl.BlockSpec((1, n_pairs, tokens, LANES), lambda b, j: (b, 0, j, 0))
    slab_shape = jax.ShapeDtypeStruct((batch, n_pairs, seq, LANES), jnp.float32)
    if final:
        in_specs += [slab_spec] * 4
        args += list(prev)
        out_specs = pl.BlockSpec((1, tokens, WIDTH_A), lambda b, j: (b, j, 0))
        out_shape = jax.ShapeDtypeStruct((batch, seq, WIDTH_A), jnp.bfloat16)
    else:
        out_specs = [slab_spec, slab_spec]
        out_shape = [slab_shape, slab_shape]
    return pl.pallas_call(
        functools.partial(_dilated_kernel, dilation=dilation, final=final),
        grid=(batch, sub_len // Q_BLK_A),
        in_specs=in_specs,
        out_specs=out_specs,
        out_shape=out_shape,
        scratch_shapes=[pltpu.VMEM((Q_BLK_A, K_WIN_A), jnp.float32),
                        pltpu.VMEM((Q_BLK_A, K_WIN_A), jnp.float32)],
        compiler_params=pltpu.CompilerParams(
            dimension_semantics=("arbitrary", "arbitrary"),
            vmem_limit_bytes=VMEM_LIMIT_BYTES),
        name=f"dilated_d{dilation}",
    )(*args)


def _t5_buckets(rel):
    nb = N_BUCKETS // 2
    max_exact = nb // 2
    ret = (rel > 0).astype(np.int32) * nb
    n = np.abs(rel)
    large = max_exact + (np.log(np.maximum(n, 1) / max_exact)
                         / np.log(MAX_DISTANCE / max_exact) * (nb - max_exact)).astype(np.int32)
    large = np.minimum(large, nb - 1)
    return (ret + np.where(n < max_exact, n, large)).astype(np.int32)


def _dilated_bias(rel_bias, dilation):
    r = np.arange(Q_BLK_A)[:, None]
    c = np.arange(K_WIN_A)[None, :]
    rel = np.stack([c - shift - r for shift in (0, RADIUS, 2 * RADIUS)])
    valid = np.abs(rel) <= RADIUS
    buckets = jnp.asarray(_t5_buckets(rel * dilation))
    onehot = (buckets[None] == jnp.arange(N_BUCKETS)[:, None, None, None]).astype(jnp.float32)
    b = jnp.einsum('nh,nvrc->vhrc', rel_bias.astype(jnp.float32), onehot,
                   precision=lax.Precision.HIGHEST)
    return jnp.where(jnp.asarray(valid)[:, None], b * math.log2(math.e), NEG_INF)


def _mla_kernel(q_ref, k_ref, vt_ref, o_ref, s_even, s_odd, mc_even, mc_odd):
    nk = vt_ref.shape[1]
    nq = q_ref.shape[1] // MLA_TQ
    heads = range(2)
    bufs = ((s_even, mc_even), (s_odd, mc_odd))
    assert nk % 2 == 0, "chunk parity must restart at every query block"

    def scores(jq, c, parity):
        s_buf, mc_buf = bufs[parity]
        q_rows = pl.ds(pl.multiple_of(jq * MLA_TQ, MLA_TQ), MLA_TQ)
        for hh in heads:
            k = k_ref[0, c * MLA_TK:(c + 1) * MLA_TK, hh * HEAD_SLOT:(hh + 1) * HEAD_SLOT]
            q = q_ref[0, q_rows, hh * HEAD_SLOT:(hh + 1) * HEAD_SLOT]
            s = lax.dot_general(k, q, (((1,), (1,)), ((), ())),
                                preferred_element_type=jnp.float32)
            s_buf[hh] = s
            mc_buf[hh] = jnp.max(s, axis=0, keepdims=True)

    scores(0, 0, 0)

    def q_block(jq, carry):
        jq_next = jnp.minimum(jq + 1, nq - 1)
        m = [None, None]
        acc = [None, None]
        for c in range(nk):
            if c + 1 < nk:
                scores(jq, c + 1, (c + 1) % 2)
            else:
                scores(jq_next, 0, 0)
            s_buf, mc_buf = bufs[c % 2]
            for hh in heads:
                mc = mc_buf[hh]
                m_new = mc if c == 0 else jnp.maximum(m[hh], mc)
                pt = jnp.exp2(s_buf[hh] - m_new).astype(jnp.bfloat16)
                pv = jnp.dot(vt_ref[0, c, hh * VT_ROWS:(hh + 1) * VT_ROWS, :], pt,
                             preferred_element_type=jnp.float32)
                acc[hh] = pv if c == 0 else jnp.exp2(m[hh] - m_new) * acc[hh] + pv
                m[hh] = m_new
        outs = [a[:V_HEAD_DIM] / a[VT_ONES_ROW:VT_ONES_ROW + 1] for a in acc]
        o_rows = pl.ds(pl.multiple_of(jq * MLA_TQ, MLA_TQ), MLA_TQ)
        o_ref[0, o_rows, :] = jnp.concatenate(outs, axis=0).T.astype(o_ref.dtype)
        return carry

    lax.fori_loop(0, nq, q_block, 0)


def _mla_call(qm, km, vt, seq):
    batch, nk = vt.shape[0], vt.shape[1]
    pair = lambda b, g: (b, 0, g)
    return pl.pallas_call(
        _mla_kernel,
        grid=(batch, N_HEADS_B // 2),
        in_specs=[
            pl.BlockSpec((1, seq, 2 * HEAD_SLOT), pair),
            pl.BlockSpec((1, seq, 2 * HEAD_SLOT), pair),
            pl.BlockSpec((1, nk, 2 * VT_ROWS, MLA_TK), lambda b, g: (b, 0, g, 0)),
        ],
        out_specs=pl.BlockSpec((1, seq, 2 * V_HEAD_DIM), pair),
        out_shape=jax.ShapeDtypeStruct((batch, seq, WIDTH_B), jnp.bfloat16),
        scratch_shapes=[
            pltpu.VMEM((2, MLA_TK, MLA_TQ), jnp.float32),
            pltpu.VMEM((2, MLA_TK, MLA_TQ), jnp.float32),
            pltpu.VMEM((2, 1, MLA_TQ), jnp.float32),
            pltpu.VMEM((2, 1, MLA_TQ), jnp.float32),
        ],
        compiler_params=pltpu.CompilerParams(
            dimension_semantics=("arbitrary", "arbitrary"),
            vmem_limit_bytes=VMEM_LIMIT_BYTES),
        name="mla",
    )(qm, km, vt)


def _mlp_kernel(x_ref, oa_ref, ob_ref, wout_ref, g_ref, wup_ref, wdown_ref, gf_ref, o_ref):
    o_cat = jnp.concatenate([oa_ref[...], ob_ref[...]], axis=-1)
    h = x_ref[...] + jnp.dot(o_cat, wout_ref[...], preferred_element_type=jnp.float32)
    u = _rms(h, g_ref[...]).astype(jnp.bfloat16)
    acc = h
    for c in range(D_FF // FF_CHUNK):
        a = jnp.dot(u, wup_ref[:, c * FF_CHUNK:(c + 1) * FF_CHUNK],
                    preferred_element_type=jnp.float32)
        a = jnp.square(jnp.maximum(a, 0.0)).astype(jnp.bfloat16)
        acc = acc + jnp.dot(a, wdown_ref[c * FF_CHUNK:(c + 1) * FF_CHUNK, :],
                            preferred_element_type=jnp.float32)
    o_ref[...] = _rms(acc, gf_ref[...])


def _mlp_call(x2, oa, ob, wout, g, wup, wdown, gf):
    tokens = x2.shape[0]
    tm = MLP_TM
    row = lambda i: (i, 0)
    const = lambda i: (0, 0)

    def wspec(shape):
        return pl.BlockSpec(shape, const, pipeline_mode=pl.Buffered(1))

    return pl.pallas_call(
        _mlp_kernel,
        grid=(tokens // tm,),
        in_specs=[
            pl.BlockSpec((tm, D_MODEL), row),
            pl.BlockSpec((tm, WIDTH_A), row),
            pl.BlockSpec((tm, WIDTH_B), row),
            wspec((WIDTH_A + WIDTH_B, D_MODEL)),
            wspec((1, D_MODEL)),
            wspec((D_MODEL, D_FF)),
            wspec((D_FF, D_MODEL)),
            wspec((1, D_MODEL)),
        ],
        out_specs=pl.BlockSpec((tm, D_MODEL), row),
        out_shape=jax.ShapeDtypeStruct((tokens, D_MODEL), jnp.float32),
        compiler_params=pltpu.CompilerParams(
            dimension_semantics=("arbitrary",), vmem_limit_bytes=VMEM_LIMIT_BYTES),
        name="mlp",
    )(x2, oa, ob, wout, g, wup, wdown, gf)


def _prep_in_weights(w_in):
    pad = jnp.zeros((D_MODEL, IN_COLS_PAD - w_in.shape[1]), w_in.dtype)
    return jnp.concatenate([w_in, pad], axis=1).astype(jnp.bfloat16)


def _prep_q_weights(w_q_b):
    dqk = QK_NOPE_DIM + QK_ROPE_DIM
    half = QK_ROPE_DIM // 2
    w = w_q_b.reshape(Q_LORA_RANK, N_HEADS_B, dqk)
    zeros = lambda n: jnp.zeros((Q_LORA_RANK, N_HEADS_B, n), w.dtype)
    nope, rope = w[..., :QK_NOPE_DIM], w[..., QK_NOPE_DIM:]
    plain = jnp.concatenate([nope, rope, zeros(HEAD_SLOT - dqk)], axis=-1)
    rot = jnp.concatenate([zeros(QK_NOPE_DIM), -rope[..., half:], rope[..., :half],
                           zeros(HEAD_SLOT - dqk)], axis=-1)
    out = jnp.concatenate([plain.reshape(Q_LORA_RANK, QK_WIDTH_B),
                           rot.reshape(Q_LORA_RANK, QK_WIDTH_B)], axis=1)
    return out.astype(jnp.bfloat16)


def _prep_kv_weights(w_kv_b):
    half = QK_ROPE_DIM // 2
    w = w_kv_b.reshape(KV_LORA_RANK, N_HEADS_B, QK_NOPE_DIM + V_HEAD_DIM)
    k_nope, v = w[..., :QK_NOPE_DIM], w[..., QK_NOPE_DIM:]
    eye = np.eye(QK_ROPE_DIM, dtype=np.float32)
    rot = np.zeros((QK_ROPE_DIM, QK_ROPE_DIM), np.float32)
    for jcol in range(half):
        rot[half + jcol, jcol] = -1.0
        rot[jcol, half + jcol] = 1.0

    def place(top, rope_block):
        blk = np.zeros((LAT_WIDTH - KV_LORA_RANK, N_HEADS_B, HEAD_SLOT), np.float32)
        blk[:QK_ROPE_DIM, :, QK_NOPE_DIM:QK_NOPE_DIM + QK_ROPE_DIM] = rope_block[:, None, :]
        top = jnp.concatenate(
            [top, jnp.zeros((KV_LORA_RANK, N_HEADS_B, HEAD_SLOT - QK_NOPE_DIM), top.dtype)], axis=-1)
        full = jnp.concatenate([top, jnp.asarray(blk)], axis=0)
        return full.reshape(LAT_WIDTH, QK_WIDTH_B)

    wk = jnp.concatenate([place(k_nope, eye), place(jnp.zeros_like(k_nope), rot)], axis=1)

    vt_top = jnp.transpose(v, (1, 2, 0))
    vt_top = jnp.concatenate(
        [vt_top, jnp.zeros((N_HEADS_B, VT_ROWS - V_HEAD_DIM, KV_LORA_RANK), v.dtype)], axis=1)
    ones_sel = np.zeros((N_HEADS_B, VT_ROWS, LAT_WIDTH - KV_LORA_RANK), np.float32)
    ones_sel[:, VT_ONES_ROW, -1] = 1.0
    wvt = jnp.concatenate([vt_top, jnp.asarray(ones_sel)], axis=2)
    return wk.astype(jnp.bfloat16), wvt.reshape(N_HEADS_B * VT_ROWS, LAT_WIDTH).astype(jnp.bfloat16)


def _rope_lane_tables(seq):
    inv_freq = ROPE_THETA ** (-jnp.arange(0, QK_ROPE_DIM, 2, dtype=jnp.float32) / QK_ROPE_DIM)
    pos = jnp.arange(seq, dtype=jnp.float32)
    freqs = pos[:, None] * inv_freq[None, :]
    cos, sin = jnp.cos(freqs), jnp.sin(freqs)
    ones = jnp.ones((seq, QK_NOPE_DIM), jnp.float32)
    zeros = lambda n: jnp.zeros((seq, n), jnp.float32)
    pad = HEAD_SLOT - QK_NOPE_DIM - QK_ROPE_DIM
    cos_t = jnp.concatenate([ones, cos, cos, zeros(pad)], axis=1)
    sin_t = jnp.concatenate([zeros(QK_NOPE_DIM), sin, sin, zeros(pad)], axis=1)
    q_scale = (QK_NOPE_DIM + QK_ROPE_DIM) ** -0.5 * math.log2(math.e)
    return cos_t * q_scale, sin_t * q_scale, cos_t, sin_t


def kernel(x, mix_norm_g, w_in, q_norm_g, w_q_b, kv_norm_g, w_kv_b, w_out,
           mlp_norm_g, w_up, w_down, rel_bias, final_norm_g):
    batch, seq, _ = x.shape
    depth = w_in.shape[0]
    assert depth == 1, "the final norm is fused into the single layer's MLP kernel"
    cq, sq, ck, sk = _rope_lane_tables(seq)
    biases = [_dilated_bias(rel_bias, d) for (_, d) in DILATED_PATTERNS]
    layer = 0
    x2 = x.reshape(batch * seq, D_MODEL)
    wk, wvt = _prep_kv_weights(w_kv_b[layer])
    qa, ka, va, qa4, ka4, va4, qa16, ka16, va16, qm, km, vt = _proj_call(
        x2, mix_norm_g[layer][None], _prep_in_weights(w_in[layer]),
        q_norm_g[layer][None], _prep_q_weights(w_q_b[layer]),
        kv_norm_g[layer][None], wk, wvt, cq, sq, ck, sk, batch, seq)

    shape_a = (batch, 1, seq, WIDTH_A)
    o16, l16 = _dilated_call(qa16, ka16, va16, biases[2])
    o4, l4 = _dilated_call(qa4, ka4, va4, biases[1])
    oa = _dilated_call(qa.reshape(shape_a), ka.reshape(shape_a), va.reshape(shape_a), biases[0],
                       prev=(o4, l4, o16, l16))

    ob = _mla_call(qm.reshape(batch, seq, QK_WIDTH_B), km.reshape(batch, seq, QK_WIDTH_B), vt, seq)

    out = _mlp_call(
        x2, oa.reshape(batch * seq, WIDTH_A), ob.reshape(batch * seq, WIDTH_B),
        w_out[layer].astype(jnp.bfloat16), mlp_norm_g[layer][None],
        w_up[layer].astype(jnp.bfloat16), w_down[layer].astype(jnp.bfloat16),
        final_norm_g[None])
    return out.reshape(batch, seq, D_MODEL)
```

```python
import functools
import math

import jax
import jax.numpy as jnp
import numpy as np
from jax import lax
from jax.experimental import pallas as pl
from jax.experimental.pallas import tpu as pltpu

D_MODEL = 1024
HEAD_DIM = 64
N_HEADS_A = 8
DILATED_PATTERNS = ((128, 1), (512, 4), (2048, 16))
N_HEADS_B = 8
Q_LORA_RANK = 256
KV_LORA_RANK = 128
QK_NOPE_DIM = 64
QK_ROPE_DIM = 32
V_HEAD_DIM = 64
ROPE_THETA = 10000.0
N_BUCKETS = 32
MAX_DISTANCE = 1024
D_FF = 4 * D_MODEL
NORM_EPS = 1e-6
NEG_INF = -1e30
WIDTH_A = N_HEADS_A * HEAD_DIM
WIDTH_B = N_HEADS_B * V_HEAD_DIM

LANES = 128
VMEM_LIMIT_BYTES = 56 * 1024 * 1024

HEAD_SLOT = LANES
QK_WIDTH_B = N_HEADS_B * HEAD_SLOT
VT_ROWS = 80
VT_ONES_ROW = V_HEAD_DIM
IN_COLS_PAD = 2048
CQ_OFF = 3 * WIDTH_A
LAT_WIDTH = 2 * LANES
RADIUS = 64
Q_BLK_A = 2 * RADIUS
K_WIN_A = 4 * RADIUS

PROJ_TM = 512
MLP_TM = 512
MLA_TQ = 256
MLA_TK = 512
FF_CHUNK = 1024


def _rms(xf, g):
    return xf * lax.rsqrt(jnp.mean(xf * xf, axis=-1, keepdims=True) + NORM_EPS) * g


def _proj_kernel(x_ref, g_ref, win_ref, qg_ref, wq_ref, kvg_ref, wk_ref, wvt_ref,
                 cq_ref, sq_ref, ck_ref, sk_ref,
                 qa_ref, ka_ref, va_ref, qa4_ref, ka4_ref, va4_ref, qa16_ref, ka16_ref, va16_ref,
                 qm_ref, km_ref, vt_ref, slab_scr):
    x = x_ref[...]
    u = _rms(x, g_ref[...]).astype(jnp.bfloat16)

    def in_proj(lo, hi):
        return jnp.dot(u, win_ref[:, lo:hi], preferred_element_type=jnp.float32)

    tm = x.shape[0]
    n_slabs = WIDTH_A // LANES
    scales = (HEAD_DIM ** -0.5 * math.log2(math.e), 1.0, 1.0)
    groups = ((qa_ref, qa4_ref, qa16_ref), (ka_ref, ka4_ref, ka16_ref), (va_ref, va4_ref, va16_ref))
    for a, (nat_ref, *strided_refs) in enumerate(groups):
        t = in_proj(a * WIDTH_A, (a + 1) * WIDTH_A) * scales[a]
        nat_ref[...] = t.astype(jnp.bfloat16)
        for g in range(n_slabs):
            slab_scr[a * n_slabs + g] = t[:, g * LANES:(g + 1) * LANES]
        for ref in strided_refs:
            d = ref.shape[1]
            for g in range(n_slabs):
                for r in range(d):
                    ref[0, r, :, g * LANES:(g + 1) * LANES] = (
                        slab_scr[a * n_slabs + g, pl.ds(r, tm // d, stride=d), :].astype(jnp.bfloat16))
    proj = in_proj(CQ_OFF, IN_COLS_PAD)

    cq = _rms(proj[:, :Q_LORA_RANK], qg_ref[...]).astype(jnp.bfloat16)
    q2 = jnp.dot(cq, wq_ref[...], preferred_element_type=jnp.float32)
    cq_t = jnp.tile(cq_ref[...], (1, N_HEADS_B))
    sq_t = jnp.tile(sq_ref[...], (1, N_HEADS_B))
    qm_ref[...] = (q2[:, :QK_WIDTH_B] * cq_t + q2[:, QK_WIDTH_B:] * sq_t).astype(jnp.bfloat16)

    ckv = _rms(proj[:, Q_LORA_RANK:Q_LORA_RANK + KV_LORA_RANK], kvg_ref[...])
    hi = proj[:, Q_LORA_RANK + KV_LORA_RANK:]
    lane = lax.broadcasted_iota(jnp.int32, hi.shape, 1)
    hi = jnp.where(lane == LANES - 1, 1.0, hi)
    lat = jnp.concatenate([ckv, hi], axis=-1).astype(jnp.bfloat16)
    k2 = jnp.dot(lat, wk_ref[...], preferred_element_type=jnp.float32)
    ck_t = jnp.tile(ck_ref[...], (1, N_HEADS_B))
    sk_t = jnp.tile(sk_ref[...], (1, N_HEADS_B))
    km_ref[...] = (k2[:, :QK_WIDTH_B] * ck_t + k2[:, QK_WIDTH_B:] * sk_t).astype(jnp.bfloat16)
    vt = lax.dot_general(wvt_ref[...], lat, (((1,), (1,)), ((), ())),
                         preferred_element_type=jnp.float32)
    vt_ref[0, 0] = vt.astype(jnp.bfloat16)


def _proj_call(x2, g, win, qg, wq, kvg, wk, wvt, cq, sq, ck, sk, batch, seq):
    tokens = x2.shape[0]
    tm = PROJ_TM
    sblk = seq // tm
    row = lambda i: (i, 0)
    const = lambda i: (0, 0)
    pos = lambda i: (i % sblk, 0)

    def wspec(shape):
        return pl.BlockSpec(shape, const, pipeline_mode=pl.Buffered(1))

    strides = [d for (_, d) in DILATED_PATTERNS if d > 1]

    def strided_spec(d):
        return pl.BlockSpec((1, d, tm // d, WIDTH_A), lambda i: (i // sblk, 0, i % sblk, 0))

    bf = jnp.bfloat16
    return pl.pallas_call(
        _proj_kernel,
        grid=(tokens // tm,),
        in_specs=[
            pl.BlockSpec((tm, D_MODEL), row),
            wspec((1, D_MODEL)),
            wspec((D_MODEL, IN_COLS_PAD)),
            wspec((1, Q_LORA_RANK)),
            wspec((Q_LORA_RANK, 2 * QK_WIDTH_B)),
            wspec((1, KV_LORA_RANK)),
            wspec((LAT_WIDTH, 2 * QK_WIDTH_B)),
            wspec((N_HEADS_B * VT_ROWS, LAT_WIDTH)),
            pl.BlockSpec((tm, HEAD_SLOT), pos),
            pl.BlockSpec((tm, HEAD_SLOT), pos),
            pl.BlockSpec((tm, HEAD_SLOT), pos),
            pl.BlockSpec((tm, HEAD_SLOT), pos),
        ],
        out_specs=[
            pl.BlockSpec((tm, WIDTH_A), row),
            pl.BlockSpec((tm, WIDTH_A), row),
            pl.BlockSpec((tm, WIDTH_A), row),
            *[strided_spec(d) for d in strides for _ in range(3)],
            pl.BlockSpec((tm, QK_WIDTH_B), row),
            pl.BlockSpec((tm, QK_WIDTH_B), row),
            pl.BlockSpec((1, 1, N_HEADS_B * VT_ROWS, tm), lambda i: (i // sblk, i % sblk, 0, 0)),
        ],
        out_shape=[
            jax.ShapeDtypeStruct((tokens, WIDTH_A), bf),
            jax.ShapeDtypeStruct((tokens, WIDTH_A), bf),
            jax.ShapeDtypeStruct((tokens, WIDTH_A), bf),
            *[jax.ShapeDtypeStruct((batch, d, seq // d, WIDTH_A), bf) for d in strides for _ in range(3)],
            jax.ShapeDtypeStruct((tokens, QK_WIDTH_B), bf),
            jax.ShapeDtypeStruct((tokens, QK_WIDTH_B), bf),
            jax.ShapeDtypeStruct((batch, sblk, N_HEADS_B * VT_ROWS, tm), bf),
        ],
        scratch_shapes=[pltpu.VMEM((3 * WIDTH_A // LANES, tm, LANES), jnp.float32)],
        compiler_params=pltpu.CompilerParams(
            dimension_semantics=("arbitrary",), vmem_limit_bytes=VMEM_LIMIT_BYTES),
        name="proj",
    )(x2, g, win, qg, wq, kvg, wk, wvt, cq, sq, ck, sk)


def _dilated_kernel(*refs, dilation, final):
    if final:
        (q_ref, k_ref, v_ref, bias_ref, o2_ref, l2_ref, o3_ref, l3_ref, out_ref, s_even, s_odd) = refs
    else:
        (q_ref, k_ref, v_ref, bias_ref, o_ref, l_ref, s_even, s_odd) = refs
    j = pl.program_id(1)
    sub_len = k_ref.shape[2]
    nblk = sub_len // Q_BLK_A
    w0 = pl.multiple_of(jnp.clip(j * Q_BLK_A - RADIUS, 0, sub_len - K_WIN_A), RADIUS)
    win = pl.ds(w0, K_WIN_A)
    variant = jnp.where(j == 0, 0, jnp.where(j == nblk - 1, 2, 1))
    lane = lax.broadcasted_iota(jnp.int32, (Q_BLK_A, LANES), 1)
    low_half = lane < HEAD_DIM
    s_bufs = (s_even, s_odd)

    def residue(r):
        def scores(h):
            g, half = divmod(h, 2)
            cols = slice(g * LANES, (g + 1) * LANES)
            qg = q_ref[0, r, :, cols]
            keep = low_half if half == 0 else jnp.logical_not(low_half)
            qh = jnp.where(keep, qg, jnp.zeros_like(qg))
            s = lax.dot_general(qh, k_ref[0, r, win, cols], (((1,), (1,)), ((), ())),
                                preferred_element_type=jnp.float32)
            s = s + bias_ref[variant, h]
            s_bufs[h % 2][...] = s
            return jnp.max(s, axis=-1, keepdims=True)

        def attend(h, m):
            cols = slice((h // 2) * LANES, (h // 2 + 1) * LANES)
            p = jnp.exp2(s_bufs[h % 2][...] - m)
            den = jnp.sum(p, axis=-1, keepdims=True)
            pv = jnp.dot(p.astype(jnp.bfloat16), v_ref[0, r, win, cols],
                         preferred_element_type=jnp.float32)
            return pv / den, m + jnp.log2(den)

        rows = pl.ds(r, Q_BLK_A, stride=dilation) if dilation > 1 else slice(None)
        m_next = scores(0)
        halves = []
        for h in range(N_HEADS_A):
            m_cur = m_next
            if h + 1 < N_HEADS_A:
                m_next = scores(h + 1)
            halves.append(attend(h, m_cur))
            if h % 2 == 0:
                continue
            g = h // 2
            (o_lo, l_lo), (o_hi, l_hi) = halves
            halves = []
            o_pair = jnp.where(low_half, o_lo, o_hi)
            l_pair = jnp.where(low_half, l_lo, l_hi)
            if final:
                l2, l3 = l2_ref[0, g], l3_ref[0, g]
                top = jnp.maximum(jnp.maximum(l_pair, l2), l3)
                w1, w2, w3 = jnp.exp2(l_pair - top), jnp.exp2(l2 - top), jnp.exp2(l3 - top)
                merged = (w1 * o_pair + w2 * o2_ref[0, g] + w3 * o3_ref[0, g]) / (w1 + w2 + w3)
                out_ref[0, :, g * LANES:(g + 1) * LANES] = merged.astype(out_ref.dtype)
            else:
                o_ref[0, g, rows, :] = o_pair
                l_ref[0, g, rows, :] = l_pair

    if dilation == 1:
        residue(0)
    else:
        def body(r, carry):
            residue(r)
            return carry
        lax.fori_loop(0, dilation, body, 0)


def _dilated_call(q, k, v, bias, prev=None):
    batch, dilation, sub_len, _ = q.shape
    seq = sub_len * dilation
    final = prev is not None
    tokens = Q_BLK_A * dilation
    n_pairs = N_HEADS_A // 2
    whole = pl.BlockSpec((1, dilation, sub_len, WIDTH_A), lambda b, j: (b, 0, 0, 0))
    in_specs = [
        pl.BlockSpec((1, dilation, Q_BLK_A, WIDTH_A), lambda b, j: (b, 0, j, 0)),
        whole,
        whole,
        pl.BlockSpec((3, N_HEADS_A, Q_BLK_A, K_WIN_A), lambda b, j: (0, 0, 0, 0),
                     pipeline_mode=pl.Buffered(1)),
    ]
    args = [q, k, v, bias]
    slab_spec = pl.BlockSpec((1, n_pairs, tokens, LANES), lambda b, j: (b, 0, j, 0))
    slab_shape = jax.ShapeDtypeStruct((batch, n_pairs, seq, LANES), jnp.float32)
    if final:
        in_specs += [slab_spec] * 4
        args += list(prev)
        out_specs = pl.BlockSpec((1, tokens, WIDTH_A), lambda b, j: (b, j, 0))
        out_shape = jax.ShapeDtypeStruct((batch, seq, WIDTH_A), jnp.bfloat16)
    else:
        out_specs = [slab_spec, slab_spec]
        out_shape = [slab_shape, slab_shape]
    return pl.pallas_call(
        functools.partial(_dilated_kernel, dilation=dilation, final=final),
        grid=(batch, sub_len // Q_BLK_A),
        in_specs=in_specs,
        out_specs=out_specs,
        out_shape=out_shape,
        scratch_shapes=[pltpu.VMEM((Q_BLK_A, K_WIN_A), jnp.float32),
                        pltpu.VMEM((Q_BLK_A, K_WIN_A), jnp.float32)],
        compiler_params=pltpu.CompilerParams(
            dimension_semantics=("arbitrary", "arbitrary"),
            vmem_limit_bytes=VMEM_LIMIT_BYTES),
        name=f"dilated_d{dilation}",
    )(*args)


def _t5_buckets(rel):
    nb = N_BUCKETS // 2
    max_exact = nb // 2
    ret = (rel > 0).astype(np.int32) * nb
    n = np.abs(rel)
    large = max_exact + (np.log(np.maximum(n, 1) / max_exact)
                         / np.log(MAX_DISTANCE / max_exact) * (nb - max_exact)).astype(np.int32)
    large = np.minimum(large, nb - 1)
    return (ret + np.where(n < max_exact, n, large)).astype(np.int32)


def _dilated_bias(rel_bias, dilation):
    r = np.arange(Q_BLK_A)[:, None]
    c = np.arange(K_WIN_A)[None, :]
    rel = np.stack([c - shift - r for shift in (0, RADIUS, 2 * RADIUS)])
    valid = np.abs(rel) <= RADIUS
    buckets = jnp.asarray(_t5_buckets(rel * dilation))
    onehot = (buckets[None] == jnp.arange(N_BUCKETS)[:, None, None, None]).astype(jnp.float32)
    b = jnp.einsum('nh,nvrc->vhrc', rel_bias.astype(jnp.float32), onehot,
                   precision=lax.Precision.HIGHEST)
    return jnp.where(jnp.asarray(valid)[:, None], b * math.log2(math.e), NEG_INF)


def _mla_kernel(q_ref, k_ref, vt_ref, o_ref, s_even, s_odd, mc_even, mc_odd):
    nk = k_ref.shape[1] // MLA_TK
    nq = q_ref.shape[1] // MLA_TQ
    heads = range(2)
    bufs = ((s_even, mc_even), (s_odd, mc_odd))
    assert nk % 2 == 0, "chunk parity must restart at every query block"

    def scores(jq, c):
        s_buf, mc_buf = bufs[c % 2]
        q_rows = pl.ds(pl.multiple_of(jq * MLA_TQ, MLA_TQ), MLA_TQ)
        for hh in heads:
            k = k_ref[0, c * MLA_TK:(c + 1) * MLA_TK, hh * HEAD_SLOT:(hh + 1) * HEAD_SLOT]
            q = q_ref[0, q_rows, hh * HEAD_SLOT:(hh + 1) * HEAD_SLOT]
            s = lax.dot_general(k, q, (((1,), (1,)), ((), ())),
                                preferred_element_type=jnp.float32)
            s_buf[hh] = s
            mc_buf[hh] = jnp.max(s, axis=0, keepdims=True)

    scores(0, 0)

    def q_block(jq, carry):
        jq_next = jnp.minimum(jq + 1, nq - 1)
        m = [None, None]
        acc = [None, None]
        for c in range(nk):
            if c + 1 < nk:
                scores(jq, c + 1)
            else:
                scores(jq_next, 0)
            s_buf, mc_buf = bufs[c % 2]
            for hh in heads:
                mc = mc_buf[hh]
                m_new = mc if c == 0 else jnp.maximum(m[hh], mc)
                pt = jnp.exp2((s_buf[hh] - m_new).astype(jnp.bfloat16))
                blk, off = divmod(c * MLA_TK, vt_ref.shape[3])
                pv = jnp.dot(vt_ref[0, blk, hh * VT_ROWS:(hh + 1) * VT_ROWS, off:off + MLA_TK], pt,
                             preferred_element_type=jnp.float32)
                acc[hh] = pv if c == 0 else jnp.exp2(m[hh] - m_new) * acc[hh] + pv
                m[hh] = m_new
        outs = [a[:V_HEAD_DIM] / a[VT_ONES_ROW:VT_ONES_ROW + 1] for a in acc]
        o_rows = pl.ds(pl.multiple_of(jq * MLA_TQ, MLA_TQ), MLA_TQ)
        o_ref[0, o_rows, :] = jnp.concatenate(outs, axis=0).T.astype(o_ref.dtype)
        return carry

    lax.fori_loop(0, nq, q_block, 0)


def _mla_call(qm, km, vt, seq):
    batch, n_blk, _, blk_keys = vt.shape
    assert blk_keys % MLA_TK == 0, "a key chunk must not straddle two V^T blocks"
    pair = lambda b, g: (b, 0, g)
    return pl.pallas_call(
        _mla_kernel,
        grid=(batch, N_HEADS_B // 2),
        in_specs=[
            pl.BlockSpec((1, seq, 2 * HEAD_SLOT), pair),
            pl.BlockSpec((1, seq, 2 * HEAD_SLOT), pair),
            pl.BlockSpec((1, n_blk, 2 * VT_ROWS, blk_keys), lambda b, g: (b, 0, g, 0)),
        ],
        out_specs=pl.BlockSpec((1, seq, 2 * V_HEAD_DIM), pair),
        out_shape=jax.ShapeDtypeStruct((batch, seq, WIDTH_B), jnp.bfloat16),
        scratch_shapes=[
            pltpu.VMEM((2, MLA_TK, MLA_TQ), jnp.float32),
            pltpu.VMEM((2, MLA_TK, MLA_TQ), jnp.float32),
            pltpu.VMEM((2, 1, MLA_TQ), jnp.float32),
            pltpu.VMEM((2, 1, MLA_TQ), jnp.float32),
        ],
        compiler_params=pltpu.CompilerParams(
            dimension_semantics=("arbitrary", "arbitrary"),
            vmem_limit_bytes=VMEM_LIMIT_BYTES),
        name="mla",
    )(qm, km, vt)


def _mlp_kernel(x_ref, oa_ref, ob_ref, wout_ref, g_ref, wup_ref, wdown_ref, gf_ref, o_ref):
    o_cat = jnp.concatenate([oa_ref[...], ob_ref[...]], axis=-1)
    h = x_ref[...] + jnp.dot(o_cat, wout_ref[...], preferred_element_type=jnp.float32)
    u = _rms(h, g_ref[...]).astype(jnp.bfloat16)
    acc = h
    for c in range(D_FF // FF_CHUNK):
        a = jnp.dot(u, wup_ref[:, c * FF_CHUNK:(c + 1) * FF_CHUNK],
                    preferred_element_type=jnp.float32)
        a = jnp.square(jnp.maximum(a, 0.0)).astype(jnp.bfloat16)
        acc = acc + jnp.dot(a, wdown_ref[c * FF_CHUNK:(c + 1) * FF_CHUNK, :],
                            preferred_element_type=jnp.float32)
    o_ref[...] = _rms(acc, gf_ref[...])


def _mlp_call(x2, oa, ob, wout, g, wup, wdown, gf):
    tokens = x2.shape[0]
    tm = MLP_TM
    row = lambda i: (i, 0)
    const = lambda i: (0, 0)

    def wspec(shape):
        return pl.BlockSpec(shape, const, pipeline_mode=pl.Buffered(1))

    return pl.pallas_call(
        _mlp_kernel,
        grid=(tokens // tm,),
        in_specs=[
            pl.BlockSpec((tm, D_MODEL), row),
            pl.BlockSpec((tm, WIDTH_A), row),
            pl.BlockSpec((tm, WIDTH_B), row),
            wspec((WIDTH_A + WIDTH_B, D_MODEL)),
            wspec((1, D_MODEL)),
            wspec((D_MODEL, D_FF)),
            wspec((D_FF, D_MODEL)),
            wspec((1, D_MODEL)),
        ],
        out_specs=pl.BlockSpec((tm, D_MODEL), row),
        out_shape=jax.ShapeDtypeStruct((tokens, D_MODEL), jnp.float32),
        compiler_params=pltpu.CompilerParams(
            dimension_semantics=("arbitrary",), vmem_limit_bytes=VMEM_LIMIT_BYTES),
        name="mlp",
    )(x2, oa, ob, wout, g, wup, wdown, gf)


def _prep_in_weights(w_in):
    pad = jnp.zeros((D_MODEL, IN_COLS_PAD - w_in.shape[1]), w_in.dtype)
    return jnp.concatenate([w_in, pad], axis=1).astype(jnp.bfloat16)


def _prep_q_weights(w_q_b):
    dqk = QK_NOPE_DIM + QK_ROPE_DIM
    half = QK_ROPE_DIM // 2
    w = w_q_b.reshape(Q_LORA_RANK, N_HEADS_B, dqk)
    zeros = lambda n: jnp.zeros((Q_LORA_RANK, N_HEADS_B, n), w.dtype)
    nope, rope = w[..., :QK_NOPE_DIM], w[..., QK_NOPE_DIM:]
    plain = jnp.concatenate([nope, rope, zeros(HEAD_SLOT - dqk)], axis=-1)
    rot = jnp.concatenate([zeros(QK_NOPE_DIM), -rope[..., half:], rope[..., :half],
                           zeros(HEAD_SLOT - dqk)], axis=-1)
    out = jnp.concatenate([plain.reshape(Q_LORA_RANK, QK_WIDTH_B),
                           rot.reshape(Q_LORA_RANK, QK_WIDTH_B)], axis=1)
    return out.astype(jnp.bfloat16)


def _prep_kv_weights(w_kv_b):
    half = QK_ROPE_DIM // 2
    w = w_kv_b.reshape(KV_LORA_RANK, N_HEADS_B, QK_NOPE_DIM + V_HEAD_DIM)
    k_nope, v = w[..., :QK_NOPE_DIM], w[..., QK_NOPE_DIM:]
    eye = np.eye(QK_ROPE_DIM, dtype=np.float32)
    rot = np.zeros((QK_ROPE_DIM, QK_ROPE_DIM), np.float32)
    for jcol in range(half):
        rot[half + jcol, jcol] = -1.0
        rot[jcol, half + jcol] = 1.0

    def place(top, rope_block):
        blk = np.zeros((LAT_WIDTH - KV_LORA_RANK, N_HEADS_B, HEAD_SLOT), np.float32)
        blk[:QK_ROPE_DIM, :, QK_NOPE_DIM:QK_NOPE_DIM + QK_ROPE_DIM] = rope_block[:, None, :]
        top = jnp.concatenate(
            [top, jnp.zeros((KV_LORA_RANK, N_HEADS_B, HEAD_SLOT - QK_NOPE_DIM), top.dtype)], axis=-1)
        full = jnp.concatenate([top, jnp.asarray(blk)], axis=0)
        return full.reshape(LAT_WIDTH, QK_WIDTH_B)

    wk = jnp.concatenate([place(k_nope, eye), place(jnp.zeros_like(k_nope), rot)], axis=1)

    vt_top = jnp.transpose(v, (1, 2, 0))
    vt_top = jnp.concatenate(
        [vt_top, jnp.zeros((N_HEADS_B, VT_ROWS - V_HEAD_DIM, KV_LORA_RANK), v.dtype)], axis=1)
    ones_sel = np.zeros((N_HEADS_B, VT_ROWS, LAT_WIDTH - KV_LORA_RANK), np.float32)
    ones_sel[:, VT_ONES_ROW, -1] = 1.0
    wvt = jnp.concatenate([vt_top, jnp.asarray(ones_sel)], axis=2)
    return wk.astype(jnp.bfloat16), wvt.reshape(N_HEADS_B * VT_ROWS, LAT_WIDTH).astype(jnp.bfloat16)


def _rope_lane_tables(seq):
    inv_freq = ROPE_THETA ** (-jnp.arange(0, QK_ROPE_DIM, 2, dtype=jnp.float32) / QK_ROPE_DIM)
    pos = jnp.arange(seq, dtype=jnp.float32)
    freqs = pos[:, None] * inv_freq[None, :]
    cos, sin = jnp.cos(freqs), jnp.sin(freqs)
    ones = jnp.ones((seq, QK_NOPE_DIM), jnp.float32)
    zeros = lambda n: jnp.zeros((seq, n), jnp.float32)
    pad = HEAD_SLOT - QK_NOPE_DIM - QK_ROPE_DIM
    cos_t = jnp.concatenate([ones, cos, cos, zeros(pad)], axis=1)
    sin_t = jnp.concatenate([zeros(QK_NOPE_DIM), sin, sin, zeros(pad)], axis=1)
    q_scale = (QK_NOPE_DIM + QK_ROPE_DIM) ** -0.5 * math.log2(math.e)
    return cos_t * q_scale, sin_t * q_scale, cos_t, sin_t


def kernel(x, mix_norm_g, w_in, q_norm_g, w_q_b, kv_norm_g, w_kv_b, w_out,
           mlp_norm_g, w_up, w_down, rel_bias, final_norm_g):
    batch, seq, _ = x.shape
    depth = w_in.shape[0]
    assert depth == 1, "the final norm is fused into the single layer's MLP kernel"
    cq, sq, ck, sk = _rope_lane_tables(seq)
    biases = [_dilated_bias(rel_bias, d) for (_, d) in DILATED_PATTERNS]
    layer = 0
    x2 = x.reshape(batch * seq, D_MODEL)
    wk, wvt = _prep_kv_weights(w_kv_b[layer])
    qa, ka, va, qa4, ka4, va4, qa16, ka16, va16, qm, km, vt = _proj_call(
        x2, mix_norm_g[layer][None], _prep_in_weights(w_in[layer]),
        q_norm_g[layer][None], _prep_q_weights(w_q_b[layer]),
        kv_norm_g[layer][None], wk, wvt, cq, sq, ck, sk, batch, seq)

    shape_a = (batch, 1, seq, WIDTH_A)
    o16, l16 = _dilated_call(qa16, ka16, va16, biases[2])
    o4, l4 = _dilated_call(qa4, ka4, va4, biases[1])
    oa = _dilated_call(qa.reshape(shape_a), ka.reshape(shape_a), va.reshape(shape_a), biases[0],
                       prev=(o4, l4, o16, l16))

    ob = _mla_call(qm.reshape(batch, seq, QK_WIDTH_B), km.reshape(batch, seq, QK_WIDTH_B), vt, seq)

    out = _mlp_call(
        x2, oa.reshape(batch * seq, WIDTH_A), ob.reshape(batch * seq, WIDTH_B),
        w_out[layer].astype(jnp.bfloat16), mlp_norm_g[layer][None],
        w_up[layer].astype(jnp.bfloat16), w_down[layer].astype(jnp.bfloat16),
        final_norm_g[None])
    return out.reshape(batch, seq, D_MODEL)
```

```python
import functools
import math

import jax
import jax.numpy as jnp
import numpy as np
from jax import lax
from jax.experimental import pallas as pl
from jax.experimental.pallas import tpu as pltpu

D_MODEL = 1024
HEAD_DIM = 64
N_HEADS_A = 8
DILATED_PATTERNS = ((128, 1), (512, 4), (2048, 16))
N_HEADS_B = 8
Q_LORA_RANK = 256
KV_LORA_RANK = 128
QK_NOPE_DIM = 64
QK_ROPE_DIM = 32
V_HEAD_DIM = 64
ROPE_THETA = 10000.0
N_BUCKETS = 32
MAX_DISTANCE = 1024
D_FF = 4 * D_MODEL
NORM_EPS = 1e-6
NEG_INF = -1e30
WIDTH_A = N_HEADS_A * HEAD_DIM
WIDTH_B = N_HEADS_B * V_HEAD_DIM

LANES = 128
VMEM_LIMIT_BYTES = 56 * 1024 * 1024

HEAD_SLOT = LANES
QK_WIDTH_B = N_HEADS_B * HEAD_SLOT
VT_ROWS = 80
VT_ONES_ROW = V_HEAD_DIM
IN_COLS_PAD = 2048
CQ_OFF = 3 * WIDTH_A
LAT_WIDTH = 2 * LANES
RADIUS = 64
Q_BLK_A = 2 * RADIUS
K_WIN_A = 4 * RADIUS

PROJ_TM = 512
MLP_TM = 512
MLA_TQ = 256
DIL_LOOKAHEAD = 2
DIL_BUFFERS = 4
DIL_Q_BLOCKS_D1 = 4
MLA_TK = 512
FF_CHUNK = 1024


def _rms(xf, g):
    return xf * lax.rsqrt(jnp.mean(xf * xf, axis=-1, keepdims=True) + NORM_EPS) * g


def _proj_kernel(x_ref, g_ref, win_ref, qg_ref, wq_ref, kvg_ref, wk_ref, wvt_ref,
                 cq_ref, sq_ref, ck_ref, sk_ref,
                 qa_ref, ka_ref, va_ref, qa4_ref, ka4_ref, va4_ref, qa16_ref, ka16_ref, va16_ref,
                 qm_ref, km_ref, vt_ref, slab_scr):
    x = x_ref[...]
    u = _rms(x, g_ref[...]).astype(jnp.bfloat16)

    def in_proj(lo, hi):
        return jnp.dot(u, win_ref[:, lo:hi], preferred_element_type=jnp.float32)

    tm = x.shape[0]
    n_slabs = WIDTH_A // LANES
    scales = (HEAD_DIM ** -0.5 * math.log2(math.e), 1.0, 1.0)
    groups = ((qa_ref, qa4_ref, qa16_ref), (ka_ref, ka4_ref, ka16_ref), (va_ref, va4_ref, va16_ref))
    for a, (nat_ref, *strided_refs) in enumerate(groups):
        t = in_proj(a * WIDTH_A, (a + 1) * WIDTH_A) * scales[a]
        nat_ref[...] = t.astype(jnp.bfloat16)
        for g in range(n_slabs):
            slab_scr[a * n_slabs + g] = t[:, g * LANES:(g + 1) * LANES]
        for ref in strided_refs:
            d = ref.shape[1]
            for g in range(n_slabs):
                for r in range(d):
                    ref[0, r, :, g * LANES:(g + 1) * LANES] = (
                        slab_scr[a * n_slabs + g, pl.ds(r, tm // d, stride=d), :].astype(jnp.bfloat16))
    proj = in_proj(CQ_OFF, IN_COLS_PAD)

    cq = _rms(proj[:, :Q_LORA_RANK], qg_ref[...]).astype(jnp.bfloat16)
    q2 = jnp.dot(cq, wq_ref[...], preferred_element_type=jnp.float32)
    cq_t = jnp.tile(cq_ref[...], (1, N_HEADS_B))
    sq_t = jnp.tile(sq_ref[...], (1, N_HEADS_B))
    qm_ref[...] = (q2[:, :QK_WIDTH_B] * cq_t + q2[:, QK_WIDTH_B:] * sq_t).astype(jnp.bfloat16)

    ckv = _rms(proj[:, Q_LORA_RANK:Q_LORA_RANK + KV_LORA_RANK], kvg_ref[...])
    hi = proj[:, Q_LORA_RANK + KV_LORA_RANK:]
    lane = lax.broadcasted_iota(jnp.int32, hi.shape, 1)
    hi = jnp.where(lane == LANES - 1, 1.0, hi)
    lat = jnp.concatenate([ckv, hi], axis=-1).astype(jnp.bfloat16)
    k2 = jnp.dot(lat, wk_ref[...], preferred_element_type=jnp.float32)
    ck_t = jnp.tile(ck_ref[...], (1, N_HEADS_B))
    sk_t = jnp.tile(sk_ref[...], (1, N_HEADS_B))
    km_ref[...] = (k2[:, :QK_WIDTH_B] * ck_t + k2[:, QK_WIDTH_B:] * sk_t).astype(jnp.bfloat16)
    vt = lax.dot_general(wvt_ref[...], lat, (((1,), (1,)), ((), ())),
                         preferred_element_type=jnp.float32)
    vt_ref[0, 0] = vt.astype(jnp.bfloat16)


def _proj_call(x2, g, win, qg, wq, kvg, wk, wvt, cq, sq, ck, sk, batch, seq):
    tokens = x2.shape[0]
    tm = PROJ_TM
    sblk = seq // tm
    row = lambda i: (i, 0)
    const = lambda i: (0, 0)
    pos = lambda i: (i % sblk, 0)

    def wspec(shape):
        return pl.BlockSpec(shape, const, pipeline_mode=pl.Buffered(1))

    strides = [d for (_, d) in DILATED_PATTERNS if d > 1]

    def strided_spec(d):
        return pl.BlockSpec((1, d, tm // d, WIDTH_A), lambda i: (i // sblk, 0, i % sblk, 0))

    bf = jnp.bfloat16
    return pl.pallas_call(
        _proj_kernel,
        grid=(tokens // tm,),
        in_specs=[
            pl.BlockSpec((tm, D_MODEL), row),
            wspec((1, D_MODEL)),
            wspec((D_MODEL, IN_COLS_PAD)),
            wspec((1, Q_LORA_RANK)),
            wspec((Q_LORA_RANK, 2 * QK_WIDTH_B)),
            wspec((1, KV_LORA_RANK)),
            wspec((LAT_WIDTH, 2 * QK_WIDTH_B)),
            wspec((N_HEADS_B * VT_ROWS, LAT_WIDTH)),
            pl.BlockSpec((tm, HEAD_SLOT), pos),
            pl.BlockSpec((tm, HEAD_SLOT), pos),
            pl.BlockSpec((tm, HEAD_SLOT), pos),
            pl.BlockSpec((tm, HEAD_SLOT), pos),
        ],
        out_specs=[
            pl.BlockSpec((tm, WIDTH_A), row),
            pl.BlockSpec((tm, WIDTH_A), row),
            pl.BlockSpec((tm, WIDTH_A), row),
            *[strided_spec(d) for d in strides for _ in range(3)],
            pl.BlockSpec((tm, QK_WIDTH_B), row),
            pl.BlockSpec((tm, QK_WIDTH_B), row),
            pl.BlockSpec((1, 1, N_HEADS_B * VT_ROWS, tm), lambda i: (i // sblk, i % sblk, 0, 0)),
        ],
        out_shape=[
            jax.ShapeDtypeStruct((tokens, WIDTH_A), bf),
            jax.ShapeDtypeStruct((tokens, WIDTH_A), bf),
            jax.ShapeDtypeStruct((tokens, WIDTH_A), bf),
            *[jax.ShapeDtypeStruct((batch, d, seq // d, WIDTH_A), bf) for d in strides for _ in range(3)],
            jax.ShapeDtypeStruct((tokens, QK_WIDTH_B), bf),
            jax.ShapeDtypeStruct((tokens, QK_WIDTH_B), bf),
            jax.ShapeDtypeStruct((batch, sblk, N_HEADS_B * VT_ROWS, tm), bf),
        ],
        scratch_shapes=[pltpu.VMEM((3 * WIDTH_A // LANES, tm, LANES), jnp.float32)],
        compiler_params=pltpu.CompilerParams(
            dimension_semantics=("arbitrary",), vmem_limit_bytes=VMEM_LIMIT_BYTES),
        name="proj",
    )(x2, g, win, qg, wq, kvg, wk, wvt, cq, sq, ck, sk)


def _dilated_kernel(*refs, dilation, final, q_blocks, n_buf):
    if final:
        q_ref, k_ref, v_ref, bias_ref, o2_ref, l2_ref, o3_ref, l3_ref, out_ref = refs[:9]
    else:
        q_ref, k_ref, v_ref, bias_ref, o_ref, l_ref = refs[:6]
    scratch = refs[-2 * n_buf:]
    z_bufs, m_bufs = scratch[:n_buf], scratch[n_buf:]
    assert N_HEADS_A % n_buf == 0 and DIL_LOOKAHEAD < n_buf, "buffer rotation must restart per item"
    assert dilation == 1 or q_blocks == 1
    j = pl.program_id(1)
    sub_len = k_ref.shape[2]
    nblk = sub_len // Q_BLK_A
    n_items = dilation * q_blocks
    lane = lax.broadcasted_iota(jnp.int32, (Q_BLK_A, LANES), 1)
    low_half = lane < HEAD_DIM

    def coords(item):
        if dilation > 1:
            return 0, item
        q0 = item * Q_BLK_A
        return (q0 if isinstance(item, int) else pl.multiple_of(q0, Q_BLK_A)), 0

    def window(q0):
        jb = j * q_blocks + q0 // Q_BLK_A
        w0 = pl.multiple_of(jnp.clip(jb * Q_BLK_A - RADIUS, 0, sub_len - K_WIN_A), RADIUS)
        variant = jnp.where(jb == 0, 0, jnp.where(jb == nblk - 1, 2, 1))
        return pl.ds(w0, K_WIN_A), variant

    def scores(item, h):
        q0, r = coords(item)
        win, variant = window(q0)
        g, half = divmod(h, 2)
        cols = slice(g * LANES, (g + 1) * LANES)
        qg = q_ref[0, r, pl.ds(q0, Q_BLK_A), cols]
        keep = low_half if half == 0 else jnp.logical_not(low_half)
        qh = jnp.where(keep, qg, jnp.zeros_like(qg))
        s = lax.dot_general(qh, k_ref[0, r, win, cols], (((1,), (1,)), ((), ())),
                            preferred_element_type=jnp.float32)
        s = s + bias_ref[variant, h]
        m = jnp.max(s, axis=-1, keepdims=True)
        z_bufs[h % n_buf][...] = s - m
        m_bufs[h % n_buf][...] = jnp.broadcast_to(m, (Q_BLK_A, LANES))

    def attend(item, h):
        q0, r = coords(item)
        win, _ = window(q0)
        cols = slice((h // 2) * LANES, (h // 2 + 1) * LANES)
        p = jnp.exp2(z_bufs[h % n_buf][...])
        den = jnp.sum(p, axis=-1, keepdims=True)
        pv = jnp.dot(p.astype(jnp.bfloat16), v_ref[0, r, win, cols],
                     preferred_element_type=jnp.float32)
        return pv / den, m_bufs[h % n_buf][...] + jnp.log2(den)

    def work_item(item, carry):
        q0, r = coords(item)
        item_next = jnp.minimum(item + 1, n_items - 1)
        if dilation == 1:
            rows = pl.ds(q0, Q_BLK_A)
        else:
            rows = pl.ds(r, Q_BLK_A, stride=dilation)
        halves = []
        for h in range(N_HEADS_A):
            ahead = h + DIL_LOOKAHEAD
            if ahead < N_HEADS_A:
                scores(item, ahead)
            else:
                scores(item_next, ahead - N_HEADS_A)
            halves.append(attend(item, h))
            if h % 2 == 0:
                continue
            g = h // 2
            (o_lo, l_lo), (o_hi, l_hi) = halves
            halves = []
            o_pair = jnp.where(low_half, o_lo, o_hi)
            l_pair = jnp.where(low_half, l_lo, l_hi)
            if final:
                l2, l3 = l2_ref[0, g, rows, :], l3_ref[0, g, rows, :]
                top = jnp.maximum(jnp.maximum(l_pair, l2), l3)
                w1, w2, w3 = jnp.exp2(l_pair - top), jnp.exp2(l2 - top), jnp.exp2(l3 - top)
                merged = ((w1 * o_pair + w2 * o2_ref[0, g, rows, :] + w3 * o3_ref[0, g, rows, :])
                          / (w1 + w2 + w3))
                out_ref[0, rows, g * LANES:(g + 1) * LANES] = merged.astype(out_ref.dtype)
            else:
                o_ref[0, g, rows, :] = o_pair
                l_ref[0, g, rows, :] = l_pair
        return carry

    for h in range(DIL_LOOKAHEAD):
        scores(0, h)
    lax.fori_loop(0, n_items, work_item, 0)


def _dilated_call(q, k, v, bias, q_blocks=1, prev=None):
    batch, dilation, sub_len, _ = q.shape
    seq = sub_len * dilation
    final = prev is not None
    tokens = Q_BLK_A * dilation * q_blocks
    n_pairs = N_HEADS_A // 2
    whole = pl.BlockSpec((1, dilation, sub_len, WIDTH_A), lambda b, j: (b, 0, 0, 0))
    in_specs = [
        pl.BlockSpec((1, dilation, q_blocks * Q_BLK_A, WIDTH_A), lambda b, j: (b, 0, j, 0)),
        whole,
        whole,
        pl.BlockSpec((3, N_HEADS_A, Q_BLK_A, K_WIN_A), lambda b, j: (0, 0, 0, 0),
                     pipeline_mode=pl.Buffered(1)),
    ]
    args = [q, k, v, bias]
    slab_spec = pl.BlockSpec((1, n_pairs, tokens, LANES), lambda b, j: (b, 0, j, 0))
    slab_shape = jax.ShapeDtypeStruct((batch, n_pairs, seq, LANES), jnp.float32)
    if final:
        in_specs += [slab_spec] * 4
        args += list(prev)
        out_specs = pl.BlockSpec((1, tokens, WIDTH_A), lambda b, j: (b, j, 0))
        out_shape = jax.ShapeDtypeStruct((batch, seq, WIDTH_A), jnp.bfloat16)
    else:
        out_specs = [slab_spec, slab_spec]
        out_shape = [slab_shape, slab_shape]
    return pl.pallas_call(
        functools.partial(_dilated_kernel, dilation=dilation, final=final, q_blocks=q_blocks,
                          n_buf=DIL_BUFFERS),
        grid=(batch, sub_len // (q_blocks * Q_BLK_A)),
        in_specs=in_specs,
        out_specs=out_specs,
        out_shape=out_shape,
        scratch_shapes=([pltpu.VMEM((Q_BLK_A, K_WIN_A), jnp.float32)] * DIL_BUFFERS
                        + [pltpu.VMEM((Q_BLK_A, LANES), jnp.float32)] * DIL_BUFFERS),
        compiler_params=pltpu.CompilerParams(
            dimension_semantics=("arbitrary", "arbitrary"),
            vmem_limit_bytes=VMEM_LIMIT_BYTES),
        name=f"dilated_d{dilation}",
    )(*args)


def _t5_buckets(rel):
    nb = N_BUCKETS // 2
    max_exact = nb // 2
    ret = (rel > 0).astype(np.int32) * nb
    n = np.abs(rel)
    large = max_exact + (np.log(np.maximum(n, 1) / max_exact)
                         / np.log(MAX_DISTANCE / max_exact) * (nb - max_exact)).astype(np.int32)
    large = np.minimum(large, nb - 1)
    return (ret + np.where(n < max_exact, n, large)).astype(np.int32)


def _dilated_bias(rel_bias, dilation):
    r = np.arange(Q_BLK_A)[:, None]
    c = np.arange(K_WIN_A)[None, :]
    rel = np.stack([c - shift - r for shift in (0, RADIUS, 2 * RADIUS)])
    valid = np.abs(rel) <= RADIUS
    buckets = jnp.asarray(_t5_buckets(rel * dilation))
    onehot = (buckets[None] == jnp.arange(N_BUCKETS)[:, None, None, None]).astype(jnp.float32)
    b = jnp.einsum('nh,nvrc->vhrc', rel_bias.astype(jnp.float32), onehot,
                   precision=lax.Precision.HIGHEST)
    return jnp.where(jnp.asarray(valid)[:, None], b * math.log2(math.e), NEG_INF)


def _mla_kernel(q_ref, k_ref, vt_ref, o_ref, s_even, s_odd, p_even, p_odd, m_even, m_odd, acc_scr):
    seq = k_ref.shape[1]
    nk = seq // MLA_TK
    nq = seq // MLA_TQ
    heads = range(2)
    s_bufs, p_bufs, m_bufs = (s_even, s_odd), (p_even, p_odd), (m_even, m_odd)
    assert nq % 2 == 0 and nq >= 4

    def q_rows(t):
        start = t * MLA_TQ
        return pl.ds(start if isinstance(t, int) else pl.multiple_of(start, MLA_TQ), MLA_TQ)

    def keys(c):
        return slice(c * MLA_TK, (c + 1) * MLA_TK)

    def stage(score_t=None, exp_t=None, pv_t=None):
        run_max = [None, None]
        for c in range(nk):
            if score_t is not None:
                t, par = score_t
                for hh in heads:
                    lanes = slice(hh * HEAD_SLOT, (hh + 1) * HEAD_SLOT)
                    s = lax.dot_general(k_ref[0, keys(c), lanes], q_ref[0, q_rows(t), lanes],
                                        (((1,), (1,)), ((), ())),
                                        preferred_element_type=jnp.float32)
                    s_bufs[par][hh, keys(c), :] = s
                    mc = jnp.max(s, axis=0, keepdims=True)
                    run_max[hh] = mc if c == 0 else jnp.maximum(run_max[hh], mc)
            if exp_t is not None:
                _, par = exp_t
                for hh in heads:
                    z = s_bufs[par][hh, keys(c), :] - m_bufs[par][hh]
                    p_bufs[par][hh, keys(c), :] = jnp.exp2(z).astype(jnp.bfloat16)
            if pv_t is not None:
                _, par = pv_t
                blk, off = divmod(c * MLA_TK, vt_ref.shape[3])
                for hh in heads:
                    pv = jnp.dot(vt_ref[0, blk, hh * VT_ROWS:(hh + 1) * VT_ROWS, off:off + MLA_TK],
                                 p_bufs[par][hh, keys(c), :],
                                 preferred_element_type=jnp.float32)
                    acc_scr[hh] = pv if c == 0 else acc_scr[hh] + pv
        if score_t is not None:
            for hh in heads:
                m_bufs[score_t[1]][hh] = run_max[hh]
        if pv_t is not None:
            outs = [acc_scr[hh, :V_HEAD_DIM] / acc_scr[hh, VT_ONES_ROW:VT_ONES_ROW + 1]
                    for hh in heads]
            o_ref[0, q_rows(pv_t[0]), :] = jnp.concatenate(outs, axis=0).T.astype(o_ref.dtype)

    stage(score_t=(0, 0))
    stage(score_t=(1, 1), exp_t=(0, 0))

    def stage_pair(i, carry):
        t = 2 * i + 1
        stage(score_t=(t + 1, 0), exp_t=(t, 1), pv_t=(t - 1, 0))
        stage(score_t=(t + 2, 1), exp_t=(t + 1, 0), pv_t=(t, 1))
        return carry

    lax.fori_loop(0, nq // 2 - 1, stage_pair, 0)
    stage(exp_t=(nq - 1, 1), pv_t=(nq - 2, 0))
    stage(pv_t=(nq - 1, 1))


def _mla_call(qm, km, vt, seq):
    batch, n_blk, _, blk_keys = vt.shape
    assert blk_keys % MLA_TK == 0, "a key chunk must not straddle two V^T blocks"
    pair = lambda b, g: (b, 0, g)
    return pl.pallas_call(
        _mla_kernel,
        grid=(batch, N_HEADS_B // 2),
        in_specs=[
            pl.BlockSpec((1, seq, 2 * HEAD_SLOT), pair),
            pl.BlockSpec((1, seq, 2 * HEAD_SLOT), pair),
            pl.BlockSpec((1, n_blk, 2 * VT_ROWS, blk_keys), lambda b, g: (b, 0, g, 0)),
        ],
        out_specs=pl.BlockSpec((1, seq, 2 * V_HEAD_DIM), pair),
        out_shape=jax.ShapeDtypeStruct((batch, seq, WIDTH_B), jnp.bfloat16),
        scratch_shapes=(
            [pltpu.VMEM((2, seq, MLA_TQ), jnp.float32)] * 2
            + [pltpu.VMEM((2, seq, MLA_TQ), jnp.bfloat16)] * 2
            + [pltpu.VMEM((2, 1, MLA_TQ), jnp.float32)] * 2
            + [pltpu.VMEM((2, VT_ROWS, MLA_TQ), jnp.float32)]),
        compiler_params=pltpu.CompilerParams(
            dimension_semantics=("arbitrary", "arbitrary"),
            vmem_limit_bytes=VMEM_LIMIT_BYTES),
        name="mla",
    )(qm, km, vt)


def _mlp_kernel(x_ref, oa_ref, ob_ref, wout_ref, g_ref, wup_ref, wdown_ref, gf_ref, o_ref):
    o_cat = jnp.concatenate([oa_ref[...], ob_ref[...]], axis=-1)
    h = x_ref[...] + jnp.dot(o_cat, wout_ref[...], preferred_element_type=jnp.float32)
    u = _rms(h, g_ref[...]).astype(jnp.bfloat16)
    acc = h
    for c in range(D_FF // FF_CHUNK):
        a = jnp.dot(u, wup_ref[:, c * FF_CHUNK:(c + 1) * FF_CHUNK],
                    preferred_element_type=jnp.float32)
        a = jnp.square(jnp.maximum(a, 0.0)).astype(jnp.bfloat16)
        acc = acc + jnp.dot(a, wdown_ref[c * FF_CHUNK:(c + 1) * FF_CHUNK, :],
                            preferred_element_type=jnp.float32)
    o_ref[...] = _rms(acc, gf_ref[...])


def _mlp_call(x2, oa, ob, wout, g, wup, wdown, gf):
    tokens = x2.shape[0]
    tm = MLP_TM
    row = lambda i: (i, 0)
    const = lambda i: (0, 0)

    def wspec(shape):
        return pl.BlockSpec(shape, const, pipeline_mode=pl.Buffered(1))

    return pl.pallas_call(
        _mlp_kernel,
        grid=(tokens // tm,),
        in_specs=[
            pl.BlockSpec((tm, D_MODEL), row),
            pl.BlockSpec((tm, WIDTH_A), row),
            pl.BlockSpec((tm, WIDTH_B), row),
            wspec((WIDTH_A + WIDTH_B, D_MODEL)),
            wspec((1, D_MODEL)),
            wspec((D_MODEL, D_FF)),
            wspec((D_FF, D_MODEL)),
            wspec((1, D_MODEL)),
        ],
        out_specs=pl.BlockSpec((tm, D_MODEL), row),
        out_shape=jax.ShapeDtypeStruct((tokens, D_MODEL), jnp.float32),
        compiler_params=pltpu.CompilerParams(
            dimension_semantics=("arbitrary",), vmem_limit_bytes=VMEM_LIMIT_BYTES),
        name="mlp",
    )(x2, oa, ob, wout, g, wup, wdown, gf)


def _prep_in_weights(w_in):
    pad = jnp.zeros((D_MODEL, IN_COLS_PAD - w_in.shape[1]), w_in.dtype)
    return jnp.concatenate([w_in, pad], axis=1).astype(jnp.bfloat16)


def _prep_q_weights(w_q_b):
    dqk = QK_NOPE_DIM + QK_ROPE_DIM
    half = QK_ROPE_DIM // 2
    w = w_q_b.reshape(Q_LORA_RANK, N_HEADS_B, dqk)
    zeros = lambda n: jnp.zeros((Q_LORA_RANK, N_HEADS_B, n), w.dtype)
    nope, rope = w[..., :QK_NOPE_DIM], w[..., QK_NOPE_DIM:]
    plain = jnp.concatenate([nope, rope, zeros(HEAD_SLOT - dqk)], axis=-1)
    rot = jnp.concatenate([zeros(QK_NOPE_DIM), -rope[..., half:], rope[..., :half],
                           zeros(HEAD_SLOT - dqk)], axis=-1)
    out = jnp.concatenate([plain.reshape(Q_LORA_RANK, QK_WIDTH_B),
                           rot.reshape(Q_LORA_RANK, QK_WIDTH_B)], axis=1)
    return out.astype(jnp.bfloat16)


def _prep_kv_weights(w_kv_b):
    half = QK_ROPE_DIM // 2
    w = w_kv_b.reshape(KV_LORA_RANK, N_HEADS_B, QK_NOPE_DIM + V_HEAD_DIM)
    k_nope, v = w[..., :QK_NOPE_DIM], w[..., QK_NOPE_DIM:]
    eye = np.eye(QK_ROPE_DIM, dtype=np.float32)
    rot = np.zeros((QK_ROPE_DIM, QK_ROPE_DIM), np.float32)
    for jcol in range(half):
        rot[half + jcol, jcol] = -1.0
        rot[jcol, half + jcol] = 1.0

    def place(top, rope_block):
        blk = np.zeros((LAT_WIDTH - KV_LORA_RANK, N_HEADS_B, HEAD_SLOT), np.float32)
        blk[:QK_ROPE_DIM, :, QK_NOPE_DIM:QK_NOPE_DIM + QK_ROPE_DIM] = rope_block[:, None, :]
        top = jnp.concatenate(
            [top, jnp.zeros((KV_LORA_RANK, N_HEADS_B, HEAD_SLOT - QK_NOPE_DIM), top.dtype)], axis=-1)
        full = jnp.concatenate([top, jnp.asarray(blk)], axis=0)
        return full.reshape(LAT_WIDTH, QK_WIDTH_B)

    wk = jnp.concatenate([place(k_nope, eye), place(jnp.zeros_like(k_nope), rot)], axis=1)

    vt_top = jnp.transpose(v, (1, 2, 0))
    vt_top = jnp.concatenate(
        [vt_top, jnp.zeros((N_HEADS_B, VT_ROWS - V_HEAD_DIM, KV_LORA_RANK), v.dtype)], axis=1)
    ones_sel = np.zeros((N_HEADS_B, VT_ROWS, LAT_WIDTH - KV_LORA_RANK), np.float32)
    ones_sel[:, VT_ONES_ROW, -1] = 1.0
    wvt = jnp.concatenate([vt_top, jnp.asarray(ones_sel)], axis=2)
    return wk.astype(jnp.bfloat16), wvt.reshape(N_HEADS_B * VT_ROWS, LAT_WIDTH).astype(jnp.bfloat16)


def _rope_lane_tables(seq):
    inv_freq = ROPE_THETA ** (-jnp.arange(0, QK_ROPE_DIM, 2, dtype=jnp.float32) / QK_ROPE_DIM)
    pos = jnp.arange(seq, dtype=jnp.float32)
    freqs = pos[:, None] * inv_freq[None, :]
    cos, sin = jnp.cos(freqs), jnp.sin(freqs)
    ones = jnp.ones((seq, QK_NOPE_DIM), jnp.float32)
    zeros = lambda n: jnp.zeros((seq, n), jnp.float32)
    pad = HEAD_SLOT - QK_NOPE_DIM - QK_ROPE_DIM
    cos_t = jnp.concatenate([ones, cos, cos, zeros(pad)], axis=1)
    sin_t = jnp.concatenate([zeros(QK_NOPE_DIM), sin, sin, zeros(pad)], axis=1)
    q_scale = (QK_NOPE_DIM + QK_ROPE_DIM) ** -0.5 * math.log2(math.e)
    return cos_t * q_scale, sin_t * q_scale, cos_t, sin_t


def kernel(x, mix_norm_g, w_in, q_norm_g, w_q_b, kv_norm_g, w_kv_b, w_out,
           mlp_norm_g, w_up, w_down, rel_bias, final_norm_g):
    batch, seq, _ = x.shape
    depth = w_in.shape[0]
    assert depth == 1, "the final norm is fused into the single layer's MLP kernel"
    cq, sq, ck, sk = _rope_lane_tables(seq)
    biases = [_dilated_bias(rel_bias, d) for (_, d) in DILATED_PATTERNS]
    layer = 0
    x2 = x.reshape(batch * seq, D_MODEL)
    wk, wvt = _prep_kv_weights(w_kv_b[layer])
    qa, ka, va, qa4, ka4, va4, qa16, ka16, va16, qm, km, vt = _proj_call(
        x2, mix_norm_g[layer][None], _prep_in_weights(w_in[layer]),
        q_norm_g[layer][None], _prep_q_weights(w_q_b[layer]),
        kv_norm_g[layer][None], wk, wvt, cq, sq, ck, sk, batch, seq)

    shape_a = (batch, 1, seq, WIDTH_A)
    o16, l16 = _dilated_call(qa16, ka16, va16, biases[2])
    o4, l4 = _dilated_call(qa4, ka4, va4, biases[1])
    oa = _dilated_call(qa.reshape(shape_a), ka.reshape(shape_a), va.reshape(shape_a), biases[0],
                       q_blocks=DIL_Q_BLOCKS_D1, prev=(o4, l4, o16, l16))

    ob = _mla_call(qm.reshape(batch, seq, QK_WIDTH_B), km.reshape(batch, seq, QK_WIDTH_B), vt, seq)

    out = _mlp_call(
        x2, oa.reshape(batch * seq, WIDTH_A), ob.reshape(batch * seq, WIDTH_B),
        w_out[layer].astype(jnp.bfloat16), mlp_norm_g[layer][None],
        w_up[layer].astype(jnp.bfloat16), w_down[layer].astype(jnp.bfloat16),
        final_norm_g[None])
    return out.reshape(batch, seq, D_MODEL)
```

```python
import functools
import math

import jax
import jax.numpy as jnp
import numpy as np
from jax import lax
from jax.experimental import pallas as pl
from jax.experimental.pallas import tpu as pltpu

D_MODEL = 1024
HEAD_DIM = 64
N_HEADS_A = 8
DILATED_PATTERNS = ((128, 1), (512, 4), (2048, 16))
N_HEADS_B = 8
Q_LORA_RANK = 256
KV_LORA_RANK = 128
QK_NOPE_DIM = 64
QK_ROPE_DIM = 32
V_HEAD_DIM = 64
ROPE_THETA = 10000.0
N_BUCKETS = 32
MAX_DISTANCE = 1024
D_FF = 4 * D_MODEL
NORM_EPS = 1e-6
NEG_INF = -1e30
WIDTH_A = N_HEADS_A * HEAD_DIM
WIDTH_B = N_HEADS_B * V_HEAD_DIM

LANES = 128
VMEM_LIMIT_BYTES = 56 * 1024 * 1024

HEAD_SLOT = LANES
QK_WIDTH_B = N_HEADS_B * HEAD_SLOT
VT_ROWS = 80
VT_ONES_ROW = V_HEAD_DIM
IN_COLS_PAD = 2048
CQ_OFF = 3 * WIDTH_A
LAT_WIDTH = 2 * LANES
RADIUS = 64
Q_BLK_A = 2 * RADIUS
K_WIN_A = 4 * RADIUS

PROJ_TM = 512
MLP_TM = 512
MLA_TQ = 256
DIL_LOOKAHEAD = 3
DIL_BUFFERS = 4
DIL_Q_BLOCKS_D1 = 4
MLA_TK = 512
FF_CHUNK = 1024


def _rms(xf, g):
    return xf * lax.rsqrt(jnp.mean(xf * xf, axis=-1, keepdims=True) + NORM_EPS) * g


def _proj_kernel(x_ref, g_ref, win_ref, qg_ref, wq_ref, kvg_ref, wk_ref, wvt_ref,
                 cq_ref, sq_ref, ck_ref, sk_ref,
                 qa_ref, ka_ref, va_ref, qa4_ref, ka4_ref, va4_ref, qa16_ref, ka16_ref, va16_ref,
                 qm_ref, km_ref, vt_ref, slab_scr, part_scr):
    x = x_ref[...]
    u = _rms(x, g_ref[...]).astype(jnp.bfloat16)

    def in_proj(lo, hi):
        return jnp.dot(u, win_ref[:, lo:hi], preferred_element_type=jnp.float32)

    tm = x.shape[0]
    n_slabs = WIDTH_A // LANES
    scales = (HEAD_DIM ** -0.5 * math.log2(math.e), 1.0, 1.0)
    groups = ((qa_ref, qa4_ref, qa16_ref), (ka_ref, ka4_ref, ka16_ref), (va_ref, va4_ref, va16_ref))
    for a, (nat_ref, *strided_refs) in enumerate(groups):
        t = in_proj(a * WIDTH_A, (a + 1) * WIDTH_A) * scales[a]
        nat_ref[...] = t.astype(jnp.bfloat16)
        ref4, ref16 = strided_refs
        d4, d16 = ref4.shape[1], ref16.shape[1]
        assert d16 == d4 * d4, "the second copy is a stride-d4 pass over the first"
        for g in range(n_slabs):
            cols = slice(g * LANES, (g + 1) * LANES)
            slab_scr[a * n_slabs + g] = t[:, cols]
            for r in range(d4):
                part = slab_scr[a * n_slabs + g, pl.ds(r, tm // d4, stride=d4), :]
                ref4[0, r, :, cols] = part.astype(jnp.bfloat16)
                part_scr[a * n_slabs + g, r] = part
            for r in range(d16):
                part = part_scr[a * n_slabs + g, r % d4, pl.ds(r // d4, tm // d16, stride=d4), :]
                ref16[0, r, :, cols] = part.astype(jnp.bfloat16)
    proj = in_proj(CQ_OFF, IN_COLS_PAD)

    cq = _rms(proj[:, :Q_LORA_RANK], qg_ref[...]).astype(jnp.bfloat16)
    q2 = jnp.dot(cq, wq_ref[...], preferred_element_type=jnp.float32)
    cq_t = jnp.tile(cq_ref[...], (1, N_HEADS_B))
    sq_t = jnp.tile(sq_ref[...], (1, N_HEADS_B))
    qm_ref[...] = (q2[:, :QK_WIDTH_B] * cq_t + q2[:, QK_WIDTH_B:] * sq_t).astype(jnp.bfloat16)

    ckv = _rms(proj[:, Q_LORA_RANK:Q_LORA_RANK + KV_LORA_RANK], kvg_ref[...])
    hi = proj[:, Q_LORA_RANK + KV_LORA_RANK:]
    lane = lax.broadcasted_iota(jnp.int32, hi.shape, 1)
    hi = jnp.where(lane == LANES - 1, 1.0, hi)
    lat = jnp.concatenate([ckv, hi], axis=-1).astype(jnp.bfloat16)
    k2 = jnp.dot(lat, wk_ref[...], preferred_element_type=jnp.float32)
    ck_t = jnp.tile(ck_ref[...], (1, N_HEADS_B))
    sk_t = jnp.tile(sk_ref[...], (1, N_HEADS_B))
    km_ref[...] = (k2[:, :QK_WIDTH_B] * ck_t + k2[:, QK_WIDTH_B:] * sk_t).astype(jnp.bfloat16)
    vt = lax.dot_general(wvt_ref[...], lat, (((1,), (1,)), ((), ())),
                         preferred_element_type=jnp.float32)
    vt_ref[0, 0] = vt.astype(jnp.bfloat16)


def _proj_call(x2, g, win, qg, wq, kvg, wk, wvt, cq, sq, ck, sk, batch, seq):
    tokens = x2.shape[0]
    tm = PROJ_TM
    sblk = seq // tm
    row = lambda i: (i, 0)
    const = lambda i: (0, 0)
    pos = lambda i: (i % sblk, 0)

    def wspec(shape):
        return pl.BlockSpec(shape, const, pipeline_mode=pl.Buffered(1))

    strides = [d for (_, d) in DILATED_PATTERNS if d > 1]

    def strided_spec(d):
        return pl.BlockSpec((1, d, tm // d, WIDTH_A), lambda i: (i // sblk, 0, i % sblk, 0))

    bf = jnp.bfloat16
    return pl.pallas_call(
        _proj_kernel,
        grid=(tokens // tm,),
        in_specs=[
            pl.BlockSpec((tm, D_MODEL), row),
            wspec((1, D_MODEL)),
            wspec((D_MODEL, IN_COLS_PAD)),
            wspec((1, Q_LORA_RANK)),
            wspec((Q_LORA_RANK, 2 * QK_WIDTH_B)),
            wspec((1, KV_LORA_RANK)),
            wspec((LAT_WIDTH, 2 * QK_WIDTH_B)),
            wspec((N_HEADS_B * VT_ROWS, LAT_WIDTH)),
            pl.BlockSpec((tm, HEAD_SLOT), pos),
            pl.BlockSpec((tm, HEAD_SLOT), pos),
            pl.BlockSpec((tm, HEAD_SLOT), pos),
            pl.BlockSpec((tm, HEAD_SLOT), pos),
        ],
        out_specs=[
            pl.BlockSpec((tm, WIDTH_A), row),
            pl.BlockSpec((tm, WIDTH_A), row),
            pl.BlockSpec((tm, WIDTH_A), row),
            *[strided_spec(d) for d in strides for _ in range(3)],
            pl.BlockSpec((tm, QK_WIDTH_B), row),
            pl.BlockSpec((tm, QK_WIDTH_B), row),
            pl.BlockSpec((1, 1, N_HEADS_B * VT_ROWS, tm), lambda i: (i // sblk, i % sblk, 0, 0)),
        ],
        out_shape=[
            jax.ShapeDtypeStruct((tokens, WIDTH_A), bf),
            jax.ShapeDtypeStruct((tokens, WIDTH_A), bf),
            jax.ShapeDtypeStruct((tokens, WIDTH_A), bf),
            *[jax.ShapeDtypeStruct((batch, d, seq // d, WIDTH_A), bf) for d in strides for _ in range(3)],
            jax.ShapeDtypeStruct((tokens, QK_WIDTH_B), bf),
            jax.ShapeDtypeStruct((tokens, QK_WIDTH_B), bf),
            jax.ShapeDtypeStruct((batch, sblk, N_HEADS_B * VT_ROWS, tm), bf),
        ],
        scratch_shapes=[
            pltpu.VMEM((3 * WIDTH_A // LANES, tm, LANES), jnp.float32),
            pltpu.VMEM((3 * WIDTH_A // LANES, strides[0], tm // strides[0], LANES), jnp.float32),
        ],
        compiler_params=pltpu.CompilerParams(
            dimension_semantics=("arbitrary",), vmem_limit_bytes=VMEM_LIMIT_BYTES),
        name="proj",
    )(x2, g, win, qg, wq, kvg, wk, wvt, cq, sq, ck, sk)


def _dilated_kernel(*refs, dilation, final, q_blocks, n_buf):
    if final:
        q_ref, k_ref, v_ref, bias_ref, o2_ref, l2_ref, o3_ref, l3_ref, out_ref = refs[:9]
    else:
        q_ref, k_ref, v_ref, bias_ref, o_ref, l_ref = refs[:6]
    scratch = refs[-2 * n_buf:]
    z_bufs, m_bufs = scratch[:n_buf], scratch[n_buf:]
    assert N_HEADS_A % n_buf == 0 and DIL_LOOKAHEAD < n_buf, "buffer rotation must restart per item"
    assert dilation == 1 or q_blocks == 1
    j = pl.program_id(1)
    sub_len = k_ref.shape[2]
    nblk = sub_len // Q_BLK_A
    n_items = dilation * q_blocks
    lane = lax.broadcasted_iota(jnp.int32, (Q_BLK_A, LANES), 1)
    low_half = lane < HEAD_DIM

    def coords(item):
        if dilation > 1:
            return 0, item
        q0 = item * Q_BLK_A
        return (q0 if isinstance(item, int) else pl.multiple_of(q0, Q_BLK_A)), 0

    def window(q0):
        jb = j * q_blocks + q0 // Q_BLK_A
        w0 = pl.multiple_of(jnp.clip(jb * Q_BLK_A - RADIUS, 0, sub_len - K_WIN_A), RADIUS)
        variant = jnp.where(jb == 0, 0, jnp.where(jb == nblk - 1, 2, 1))
        return pl.ds(w0, K_WIN_A), variant

    def scores(item, h):
        q0, r = coords(item)
        win, variant = window(q0)
        g, half = divmod(h, 2)
        cols = slice(g * LANES, (g + 1) * LANES)
        qg = q_ref[0, r, pl.ds(q0, Q_BLK_A), cols]
        keep = low_half if half == 0 else jnp.logical_not(low_half)
        qh = jnp.where(keep, qg, jnp.zeros_like(qg))
        s = lax.dot_general(qh, k_ref[0, r, win, cols], (((1,), (1,)), ((), ())),
                            preferred_element_type=jnp.float32)
        s = s + bias_ref[variant, h]
        m = jnp.max(jnp.maximum(s[:, :LANES], s[:, LANES:]), axis=-1, keepdims=True)
        z_bufs[h % n_buf][...] = s - m
        m_bufs[h % n_buf][...] = jnp.broadcast_to(m, (Q_BLK_A, LANES))

    def attend(item, h):
        q0, r = coords(item)
        win, _ = window(q0)
        cols = slice((h // 2) * LANES, (h // 2 + 1) * LANES)
        p = jnp.exp2(z_bufs[h % n_buf][...])
        den = jnp.sum(p[:, :LANES] + p[:, LANES:], axis=-1, keepdims=True)
        pv = jnp.dot(p.astype(jnp.bfloat16), v_ref[0, r, win, cols],
                     preferred_element_type=jnp.float32)
        return pv, den, m_bufs[h % n_buf][...]

    def work_item(item, carry):
        q0, r = coords(item)
        item_next = jnp.minimum(item + 1, n_items - 1)
        if dilation == 1:
            rows = pl.ds(q0, Q_BLK_A)
        else:
            rows = pl.ds(r, Q_BLK_A, stride=dilation)
        halves = []
        for h in range(N_HEADS_A):
            ahead = h + DIL_LOOKAHEAD
            if ahead < N_HEADS_A:
                scores(item, ahead)
            else:
                scores(item_next, ahead - N_HEADS_A)
            halves.append(attend(item, h))
            if h % 2 == 0:
                continue
            g = h // 2
            (pv_lo, den_lo, m_lo), (pv_hi, den_hi, m_hi) = halves
            halves = []
            den_pair = jnp.where(low_half, den_lo, den_hi)
            o_pair = jnp.where(low_half, pv_lo, pv_hi) / den_pair
            l_pair = jnp.where(low_half, m_lo, m_hi) + jnp.log2(den_pair)
            if final:
                l2, l3 = l2_ref[0, g, rows, :], l3_ref[0, g, rows, :]
                top = jnp.maximum(jnp.maximum(l_pair, l2), l3)
                w1, w2, w3 = jnp.exp2(l_pair - top), jnp.exp2(l2 - top), jnp.exp2(l3 - top)
                merged = ((w1 * o_pair + w2 * o2_ref[0, g, rows, :] + w3 * o3_ref[0, g, rows, :])
                          / (w1 + w2 + w3))
                out_ref[0, rows, g * LANES:(g + 1) * LANES] = merged.astype(out_ref.dtype)
            else:
                o_ref[0, g, rows, :] = o_pair
                l_ref[0, g, rows, :] = l_pair
        return carry

    for h in range(DIL_LOOKAHEAD):
        scores(0, h)
    lax.fori_loop(0, n_items, work_item, 0)


def _dilated_call(q, k, v, bias, q_blocks=1, prev=None):
    batch, dilation, sub_len, _ = q.shape
    seq = sub_len * dilation
    final = prev is not None
    tokens = Q_BLK_A * dilation * q_blocks
    n_pairs = N_HEADS_A // 2
    whole = pl.BlockSpec((1, dilation, sub_len, WIDTH_A), lambda b, j: (b, 0, 0, 0))
    in_specs = [
        pl.BlockSpec((1, dilation, q_blocks * Q_BLK_A, WIDTH_A), lambda b, j: (b, 0, j, 0)),
        whole,
        whole,
        pl.BlockSpec((3, N_HEADS_A, Q_BLK_A, K_WIN_A), lambda b, j: (0, 0, 0, 0),
                     pipeline_mode=pl.Buffered(1)),
    ]
    args = [q, k, v, bias]
    slab_spec = pl.BlockSpec((1, n_pairs, tokens, LANES), lambda b, j: (b, 0, j, 0))
    slab_shape = jax.ShapeDtypeStruct((batch, n_pairs, seq, LANES), jnp.float32)
    if final:
        in_specs += [slab_spec] * 4
        args += list(prev)
        out_specs = pl.BlockSpec((1, tokens, WIDTH_A), lambda b, j: (b, j, 0))
        out_shape = jax.ShapeDtypeStruct((batch, seq, WIDTH_A), jnp.bfloat16)
    else:
        out_specs = [slab_spec, slab_spec]
        out_shape = [slab_shape, slab_shape]
    return pl.pallas_call(
        functools.partial(_dilated_kernel, dilation=dilation, final=final, q_blocks=q_blocks,
                          n_buf=DIL_BUFFERS),
        grid=(batch, sub_len // (q_blocks * Q_BLK_A)),
        in_specs=in_specs,
        out_specs=out_specs,
        out_shape=out_shape,
        scratch_shapes=([pltpu.VMEM((Q_BLK_A, K_WIN_A), jnp.float32)] * DIL_BUFFERS
                        + [pltpu.VMEM((Q_BLK_A, LANES), jnp.float32)] * DIL_BUFFERS),
        compiler_params=pltpu.CompilerParams(
            dimension_semantics=("arbitrary", "arbitrary"),
            vmem_limit_bytes=VMEM_LIMIT_BYTES),
        name=f"dilated_d{dilation}",
    )(*args)


def _t5_buckets(rel):
    nb = N_BUCKETS // 2
    max_exact = nb // 2
    ret = (rel > 0).astype(np.int32) * nb
    n = np.abs(rel)
    large = max_exact + (np.log(np.maximum(n, 1) / max_exact)
                         / np.log(MAX_DISTANCE / max_exact) * (nb - max_exact)).astype(np.int32)
    large = np.minimum(large, nb - 1)
    return (ret + np.where(n < max_exact, n, large)).astype(np.int32)


def _dilated_bias(rel_bias, dilation):
    r = np.arange(Q_BLK_A)[:, None]
    c = np.arange(K_WIN_A)[None, :]
    rel = np.stack([c - shift - r for shift in (0, RADIUS, 2 * RADIUS)])
    valid = np.abs(rel) <= RADIUS
    buckets = jnp.asarray(_t5_buckets(rel * dilation))
    onehot = (buckets[None] == jnp.arange(N_BUCKETS)[:, None, None, None]).astype(jnp.float32)
    b = jnp.einsum('nh,nvrc->vhrc', rel_bias.astype(jnp.float32), onehot,
                   precision=lax.Precision.HIGHEST)
    return jnp.where(jnp.asarray(valid)[:, None], b * math.log2(math.e), NEG_INF)


def _mla_kernel(q_ref, k_ref, vt_ref, o_ref, s_even, s_odd, p_even, p_odd, m_even, m_odd, acc_scr):
    seq = k_ref.shape[1]
    nk = seq // MLA_TK
    nq = seq // MLA_TQ
    heads = range(2)
    s_bufs, p_bufs, m_bufs = (s_even, s_odd), (p_even, p_odd), (m_even, m_odd)
    assert nq % 2 == 0 and nq >= 4

    def q_rows(t):
        start = t * MLA_TQ
        return pl.ds(start if isinstance(t, int) else pl.multiple_of(start, MLA_TQ), MLA_TQ)

    def keys(c):
        return slice(c * MLA_TK, (c + 1) * MLA_TK)

    def stage(score_t=None, exp_t=None, pv_t=None):
        run_max = [None, None]
        for c in range(nk):
            if score_t is not None:
                t, par = score_t
                for hh in heads:
                    lanes = slice(hh * HEAD_SLOT, (hh + 1) * HEAD_SLOT)
                    s = lax.dot_general(k_ref[0, keys(c), lanes], q_ref[0, q_rows(t), lanes],
                                        (((1,), (1,)), ((), ())),
                                        preferred_element_type=jnp.float32)
                    s_bufs[par][hh, keys(c), :] = s
                    mc = jnp.max(s, axis=0, keepdims=True)
                    run_max[hh] = mc if c == 0 else jnp.maximum(run_max[hh], mc)
            if exp_t is not None:
                _, par = exp_t
                for hh in heads:
                    z = s_bufs[par][hh, keys(c), :] - m_bufs[par][hh]
                    p_bufs[par][hh, keys(c), :] = jnp.exp2(z).astype(jnp.bfloat16)
            if pv_t is not None:
                _, par = pv_t
                blk, off = divmod(c * MLA_TK, vt_ref.shape[3])
                for hh in heads:
                    pv = jnp.dot(vt_ref[0, blk, hh * VT_ROWS:(hh + 1) * VT_ROWS, off:off + MLA_TK],
                                 p_bufs[par][hh, keys(c), :],
                                 preferred_element_type=jnp.float32)
                    acc_scr[hh] = pv if c == 0 else acc_scr[hh] + pv
        if score_t is not None:
            for hh in heads:
                m_bufs[score_t[1]][hh] = run_max[hh]
        if pv_t is not None:
            outs = [acc_scr[hh, :V_HEAD_DIM] / acc_scr[hh, VT_ONES_ROW:VT_ONES_ROW + 1]
                    for hh in heads]
            o_ref[0, q_rows(pv_t[0]), :] = jnp.concatenate(outs, axis=0).T.astype(o_ref.dtype)

    stage(score_t=(0, 0))
    stage(score_t=(1, 1), exp_t=(0, 0))

    def stage_pair(i, carry):
        t = 2 * i + 1
        stage(score_t=(t + 1, 0), exp_t=(t, 1), pv_t=(t - 1, 0))
        stage(score_t=(t + 2, 1), exp_t=(t + 1, 0), pv_t=(t, 1))
        return carry

    lax.fori_loop(0, nq // 2 - 1, stage_pair, 0)
    stage(exp_t=(nq - 1, 1), pv_t=(nq - 2, 0))
    stage(pv_t=(nq - 1, 1))


def _mla_call(qm, km, vt, seq):
    batch, n_blk, _, blk_keys = vt.shape
    assert blk_keys % MLA_TK == 0, "a key chunk must not straddle two V^T blocks"
    pair = lambda b, g: (b, 0, g)
    return pl.pallas_call(
        _mla_kernel,
        grid=(batch, N_HEADS_B // 2),
        in_specs=[
            pl.BlockSpec((1, seq, 2 * HEAD_SLOT), pair),
            pl.BlockSpec((1, seq, 2 * HEAD_SLOT), pair),
            pl.BlockSpec((1, n_blk, 2 * VT_ROWS, blk_keys), lambda b, g: (b, 0, g, 0)),
        ],
        out_specs=pl.BlockSpec((1, seq, 2 * V_HEAD_DIM), pair),
        out_shape=jax.ShapeDtypeStruct((batch, seq, WIDTH_B), jnp.bfloat16),
        scratch_shapes=(
            [pltpu.VMEM((2, seq, MLA_TQ), jnp.float32)] * 2
            + [pltpu.VMEM((2, seq, MLA_TQ), jnp.bfloat16)] * 2
            + [pltpu.VMEM((2, 1, MLA_TQ), jnp.float32)] * 2
            + [pltpu.VMEM((2, VT_ROWS, MLA_TQ), jnp.float32)]),
        compiler_params=pltpu.CompilerParams(
            dimension_semantics=("arbitrary", "arbitrary"),
            vmem_limit_bytes=VMEM_LIMIT_BYTES),
        name="mla",
    )(qm, km, vt)


def _mlp_kernel(x_ref, oa_ref, ob_ref, wout_ref, g_ref, wup_ref, wdown_ref, gf_ref, o_ref):
    o_cat = jnp.concatenate([oa_ref[...], ob_ref[...]], axis=-1)
    h = x_ref[...] + jnp.dot(o_cat, wout_ref[...], preferred_element_type=jnp.float32)
    u = _rms(h, g_ref[...]).astype(jnp.bfloat16)
    acc = h
    for c in range(D_FF // FF_CHUNK):
        a = jnp.dot(u, wup_ref[:, c * FF_CHUNK:(c + 1) * FF_CHUNK],
                    preferred_element_type=jnp.float32)
        a = jnp.square(jnp.maximum(a, 0.0)).astype(jnp.bfloat16)
        acc = acc + jnp.dot(a, wdown_ref[c * FF_CHUNK:(c + 1) * FF_CHUNK, :],
                            preferred_element_type=jnp.float32)
    o_ref[...] = _rms(acc, gf_ref[...])


def _mlp_call(x2, oa, ob, wout, g, wup, wdown, gf):
    tokens = x2.shape[0]
    tm = MLP_TM
    row = lambda i: (i, 0)
    const = lambda i: (0, 0)

    def wspec(shape):
        return pl.BlockSpec(shape, const, pipeline_mode=pl.Buffered(1))

    return pl.pallas_call(
        _mlp_kernel,
        grid=(tokens // tm,),
        in_specs=[
            pl.BlockSpec((tm, D_MODEL), row),
            pl.BlockSpec((tm, WIDTH_A), row),
            pl.BlockSpec((tm, WIDTH_B), row),
            wspec((WIDTH_A + WIDTH_B, D_MODEL)),
            wspec((1, D_MODEL)),
            wspec((D_MODEL, D_FF)),
            wspec((D_FF, D_MODEL)),
            wspec((1, D_MODEL)),
        ],
        out_specs=pl.BlockSpec((tm, D_MODEL), row),
        out_shape=jax.ShapeDtypeStruct((tokens, D_MODEL), jnp.float32),
        compiler_params=pltpu.CompilerParams(
            dimension_semantics=("arbitrary",), vmem_limit_bytes=VMEM_LIMIT_BYTES),
        name="mlp",
    )(x2, oa, ob, wout, g, wup, wdown, gf)


def _prep_in_weights(w_in):
    pad = jnp.zeros((D_MODEL, IN_COLS_PAD - w_in.shape[1]), w_in.dtype)
    return jnp.concatenate([w_in, pad], axis=1).astype(jnp.bfloat16)


def _prep_q_weights(w_q_b):
    dqk = QK_NOPE_DIM + QK_ROPE_DIM
    half = QK_ROPE_DIM // 2
    w = w_q_b.reshape(Q_LORA_RANK, N_HEADS_B, dqk)
    zeros = lambda n: jnp.zeros((Q_LORA_RANK, N_HEADS_B, n), w.dtype)
    nope, rope = w[..., :QK_NOPE_DIM], w[..., QK_NOPE_DIM:]
    plain = jnp.concatenate([nope, rope, zeros(HEAD_SLOT - dqk)], axis=-1)
    rot = jnp.concatenate([zeros(QK_NOPE_DIM), -rope[..., half:], rope[..., :half],
                           zeros(HEAD_SLOT - dqk)], axis=-1)
    out = jnp.concatenate([plain.reshape(Q_LORA_RANK, QK_WIDTH_B),
                           rot.reshape(Q_LORA_RANK, QK_WIDTH_B)], axis=1)
    return out.astype(jnp.bfloat16)


def _prep_kv_weights(w_kv_b):
    half = QK_ROPE_DIM // 2
    w = w_kv_b.reshape(KV_LORA_RANK, N_HEADS_B, QK_NOPE_DIM + V_HEAD_DIM)
    k_nope, v = w[..., :QK_NOPE_DIM], w[..., QK_NOPE_DIM:]
    eye = np.eye(QK_ROPE_DIM, dtype=np.float32)
    rot = np.zeros((QK_ROPE_DIM, QK_ROPE_DIM), np.float32)
    for jcol in range(half):
        rot[half + jcol, jcol] = -1.0
        rot[jcol, half + jcol] = 1.0

    def place(top, rope_block):
        blk = np.zeros((LAT_WIDTH - KV_LORA_RANK, N_HEADS_B, HEAD_SLOT), np.float32)
        blk[:QK_ROPE_DIM, :, QK_NOPE_DIM:QK_NOPE_DIM + QK_ROPE_DIM] = rope_block[:, None, :]
        top = jnp.concatenate(
            [top, jnp.zeros((KV_LORA_RANK, N_HEADS_B, HEAD_SLOT - QK_NOPE_DIM), top.dtype)], axis=-1)
        full = jnp.concatenate([top, jnp.asarray(blk)], axis=0)
        return full.reshape(LAT_WIDTH, QK_WIDTH_B)

    wk = jnp.concatenate([place(k_nope, eye), place(jnp.zeros_like(k_nope), rot)], axis=1)

    vt_top = jnp.transpose(v, (1, 2, 0))
    vt_top = jnp.concatenate(
        [vt_top, jnp.zeros((N_HEADS_B, VT_ROWS - V_HEAD_DIM, KV_LORA_RANK), v.dtype)], axis=1)
    ones_sel = np.zeros((N_HEADS_B, VT_ROWS, LAT_WIDTH - KV_LORA_RANK), np.float32)
    ones_sel[:, VT_ONES_ROW, -1] = 1.0
    wvt = jnp.concatenate([vt_top, jnp.asarray(ones_sel)], axis=2)
    return wk.astype(jnp.bfloat16), wvt.reshape(N_HEADS_B * VT_ROWS, LAT_WIDTH).astype(jnp.bfloat16)


def _rope_lane_tables(seq):
    inv_freq = ROPE_THETA ** (-jnp.arange(0, QK_ROPE_DIM, 2, dtype=jnp.float32) / QK_ROPE_DIM)
    pos = jnp.arange(seq, dtype=jnp.float32)
    freqs = pos[:, None] * inv_freq[None, :]
    cos, sin = jnp.cos(freqs), jnp.sin(freqs)
    ones = jnp.ones((seq, QK_NOPE_DIM), jnp.float32)
    zeros = lambda n: jnp.zeros((seq, n), jnp.float32)
    pad = HEAD_SLOT - QK_NOPE_DIM - QK_ROPE_DIM
    cos_t = jnp.concatenate([ones, cos, cos, zeros(pad)], axis=1)
    sin_t = jnp.concatenate([zeros(QK_NOPE_DIM), sin, sin, zeros(pad)], axis=1)
    q_scale = (QK_NOPE_DIM + QK_ROPE_DIM) ** -0.5 * math.log2(math.e)
    return cos_t * q_scale, sin_t * q_scale, cos_t, sin_t


def kernel(x, mix_norm_g, w_in, q_norm_g, w_q_b, kv_norm_g, w_kv_b, w_out,
           mlp_norm_g, w_up, w_down, rel_bias, final_norm_g):
    batch, seq, _ = x.shape
    depth = w_in.shape[0]
    assert depth == 1, "the final norm is fused into the single layer's MLP kernel"
    cq, sq, ck, sk = _rope_lane_tables(seq)
    biases = [_dilated_bias(rel_bias, d) for (_, d) in DILATED_PATTERNS]
    layer = 0
    x2 = x.reshape(batch * seq, D_MODEL)
    wk, wvt = _prep_kv_weights(w_kv_b[layer])
    qa, ka, va, qa4, ka4, va4, qa16, ka16, va16, qm, km, vt = _proj_call(
        x2, mix_norm_g[layer][None], _prep_in_weights(w_in[layer]),
        q_norm_g[layer][None], _prep_q_weights(w_q_b[layer]),
        kv_norm_g[layer][None], wk, wvt, cq, sq, ck, sk, batch, seq)

    shape_a = (batch, 1, seq, WIDTH_A)
    o16, l16 = _dilated_call(qa16, ka16, va16, biases[2])
    o4, l4 = _dilated_call(qa4, ka4, va4, biases[1])
    oa = _dilated_call(qa.reshape(shape_a), ka.reshape(shape_a), va.reshape(shape_a), biases[0],
                       q_blocks=DIL_Q_BLOCKS_D1, prev=(o4, l4, o16, l16))

    ob = _mla_call(qm.reshape(batch, seq, QK_WIDTH_B), km.reshape(batch, seq, QK_WIDTH_B), vt, seq)

    out = _mlp_call(
        x2, oa.reshape(batch * seq, WIDTH_A), ob.reshape(batch * seq, WIDTH_B),
        w_out[layer].astype(jnp.bfloat16), mlp_norm_g[layer][None],
        w_up[layer].astype(jnp.bfloat16), w_down[layer].astype(jnp.bfloat16),
        final_norm_g[None])
    return out.reshape(batch, seq, D_MODEL)
```

```python
import functools
import math

import jax
import jax.numpy as jnp
import numpy as np
from jax import lax
from jax.experimental import pallas as pl
from jax.experimental.pallas import tpu as pltpu

D_MODEL = 1024
HEAD_DIM = 64
N_HEADS_A = 8
DILATED_PATTERNS = ((128, 1), (512, 4), (2048, 16))
N_HEADS_B = 8
Q_LORA_RANK = 256
KV_LORA_RANK = 128
QK_NOPE_DIM = 64
QK_ROPE_DIM = 32
V_HEAD_DIM = 64
ROPE_THETA = 10000.0
N_BUCKETS = 32
MAX_DISTANCE = 1024
D_FF = 4 * D_MODEL
NORM_EPS = 1e-6
NEG_INF = -1e30
WIDTH_A = N_HEADS_A * HEAD_DIM
WIDTH_B = N_HEADS_B * V_HEAD_DIM

LANES = 128
VMEM_LIMIT_BYTES = 56 * 1024 * 1024

HEAD_SLOT = LANES
QK_WIDTH_B = N_HEADS_B * HEAD_SLOT
VT_ROWS = 80
VT_ONES_ROW = V_HEAD_DIM
IN_COLS_PAD = 2048
CQ_OFF = 3 * WIDTH_A
LAT_WIDTH = 2 * LANES
RADIUS = 64
Q_BLK_A = 2 * RADIUS
K_WIN_A = 4 * RADIUS

PROJ_TM = 512
MLP_TM = 512
MLA_TQ = 256
DIL_LOOKAHEAD = 3
DIL_BUFFERS = 4
DIL_Q_BLOCKS = (8, 4, 1)
MLA_TK = 512
FF_CHUNK = 1024


def _rms(xf, g):
    return xf * lax.rsqrt(jnp.mean(xf * xf, axis=-1, keepdims=True) + NORM_EPS) * g


def _proj_kernel(x_ref, g_ref, win_ref, qg_ref, wq_ref, kvg_ref, wk_ref, wvt_ref,
                 tq_ref, ck_ref, sk_ref,
                 qa_ref, ka_ref, va_ref, qa4_ref, ka4_ref, va4_ref, qa16_ref, ka16_ref, va16_ref,
                 qm_ref, km_ref, vt_ref, slab_scr, part_scr):
    x = x_ref[...]
    u = _rms(x, g_ref[...]).astype(jnp.bfloat16)

    def in_proj(lo, hi):
        return jnp.dot(u, win_ref[:, lo:hi], preferred_element_type=jnp.float32)

    tm = x.shape[0]
    n_slabs = WIDTH_A // LANES
    scales = (HEAD_DIM ** -0.5 * math.log2(math.e), 1.0, 1.0)
    groups = ((qa_ref, qa4_ref, qa16_ref), (ka_ref, ka4_ref, ka16_ref), (va_ref, va4_ref, va16_ref))
    for a, (nat_ref, *strided_refs) in enumerate(groups):
        t = in_proj(a * WIDTH_A, (a + 1) * WIDTH_A) * scales[a]
        nat_ref[...] = t.astype(jnp.bfloat16)
        ref4, ref16 = strided_refs
        d4, d16 = ref4.shape[1], ref16.shape[1]
        assert d16 == d4 * d4, "the second copy is a stride-d4 pass over the first"
        for g in range(n_slabs):
            cols = slice(g * LANES, (g + 1) * LANES)
            slab_scr[a * n_slabs + g] = t[:, cols]
            for r in range(d4):
                part = slab_scr[a * n_slabs + g, pl.ds(r, tm // d4, stride=d4), :]
                ref4[0, r, :, cols] = part.astype(jnp.bfloat16)
                part_scr[a * n_slabs + g, r] = part
            for r in range(d16):
                part = part_scr[a * n_slabs + g, r % d4, pl.ds(r // d4, tm // d16, stride=d4), :]
                ref16[0, r, :, cols] = part.astype(jnp.bfloat16)
    proj = in_proj(CQ_OFF, IN_COLS_PAD)

    cq = _rms(proj[:, :Q_LORA_RANK], qg_ref[...]).astype(jnp.bfloat16)
    q2 = jnp.dot(cq, wq_ref[...], preferred_element_type=jnp.float32)
    qm_ref[...] = (q2 * jnp.tile(tq_ref[...], (1, N_HEADS_B))).astype(jnp.bfloat16)

    ckv = _rms(proj[:, Q_LORA_RANK:Q_LORA_RANK + KV_LORA_RANK], kvg_ref[...])
    hi = proj[:, Q_LORA_RANK + KV_LORA_RANK:]
    lane = lax.broadcasted_iota(jnp.int32, hi.shape, 1)
    hi = jnp.where(lane == LANES - 1, 1.0, hi)
    lat = jnp.concatenate([ckv, hi], axis=-1).astype(jnp.bfloat16)
    k2 = jnp.dot(lat, wk_ref[...], preferred_element_type=jnp.float32)
    ck_t = jnp.tile(ck_ref[...], (1, N_HEADS_B))
    sk_t = jnp.tile(sk_ref[...], (1, N_HEADS_B))
    km_ref[...] = (k2[:, :QK_WIDTH_B] * ck_t + k2[:, QK_WIDTH_B:] * sk_t).astype(jnp.bfloat16)
    vt = lax.dot_general(wvt_ref[...], lat, (((1,), (1,)), ((), ())),
                         preferred_element_type=jnp.float32)
    vt_ref[0, 0] = vt.astype(jnp.bfloat16)


def _proj_call(x2, g, win, qg, wq, kvg, wk, wvt, tq, ck, sk, batch, seq):
    tokens = x2.shape[0]
    tm = PROJ_TM
    sblk = seq // tm
    row = lambda i: (i, 0)
    const = lambda i: (0, 0)
    pos = lambda i: (i % sblk, 0)

    def wspec(shape):
        return pl.BlockSpec(shape, const, pipeline_mode=pl.Buffered(1))

    strides = [d for (_, d) in DILATED_PATTERNS if d > 1]

    def strided_spec(d):
        return pl.BlockSpec((1, d, tm // d, WIDTH_A), lambda i: (i // sblk, 0, i % sblk, 0))

    bf = jnp.bfloat16
    return pl.pallas_call(
        _proj_kernel,
        grid=(tokens // tm,),
        in_specs=[
            pl.BlockSpec((tm, D_MODEL), row),
            wspec((1, D_MODEL)),
            wspec((D_MODEL, IN_COLS_PAD)),
            wspec((1, Q_LORA_RANK)),
            wspec((Q_LORA_RANK, QK_WIDTH_B)),
            wspec((1, KV_LORA_RANK)),
            wspec((LAT_WIDTH, 2 * QK_WIDTH_B)),
            wspec((N_HEADS_B * VT_ROWS, LAT_WIDTH)),
            pl.BlockSpec((tm, HEAD_SLOT), pos),
            pl.BlockSpec((tm, HEAD_SLOT), pos),
            pl.BlockSpec((tm, HEAD_SLOT), pos),
        ],
        out_specs=[
            pl.BlockSpec((tm, WIDTH_A), row),
            pl.BlockSpec((tm, WIDTH_A), row),
            pl.BlockSpec((tm, WIDTH_A), row),
            *[strided_spec(d) for d in strides for _ in range(3)],
            pl.BlockSpec((tm, QK_WIDTH_B), row),
            pl.BlockSpec((tm, QK_WIDTH_B), row),
            pl.BlockSpec((1, 1, N_HEADS_B * VT_ROWS, tm), lambda i: (i // sblk, i % sblk, 0, 0)),
        ],
        out_shape=[
            jax.ShapeDtypeStruct((tokens, WIDTH_A), bf),
            jax.ShapeDtypeStruct((tokens, WIDTH_A), bf),
            jax.ShapeDtypeStruct((tokens, WIDTH_A), bf),
            *[jax.ShapeDtypeStruct((batch, d, seq // d, WIDTH_A), bf) for d in strides for _ in range(3)],
            jax.ShapeDtypeStruct((tokens, QK_WIDTH_B), bf),
            jax.ShapeDtypeStruct((tokens, QK_WIDTH_B), bf),
            jax.ShapeDtypeStruct((batch, sblk, N_HEADS_B * VT_ROWS, tm), bf),
        ],
        scratch_shapes=[
            pltpu.VMEM((3 * WIDTH_A // LANES, tm, LANES), jnp.float32),
            pltpu.VMEM((3 * WIDTH_A // LANES, strides[0], tm // strides[0], LANES), jnp.float32),
        ],
        compiler_params=pltpu.CompilerParams(
            dimension_semantics=("arbitrary",), vmem_limit_bytes=VMEM_LIMIT_BYTES),
        name="proj",
    )(x2, g, win, qg, wq, kvg, wk, wvt, tq, ck, sk)


def _dilated_kernel(*refs, dilation, final, q_blocks, n_buf):
    if final:
        q_ref, k_ref, v_ref, bias_ref, o2_ref, l2_ref, o3_ref, l3_ref, out_ref = refs[:9]
    else:
        q_ref, k_ref, v_ref, bias_ref, o_ref, l_ref = refs[:6]
    scratch = refs[-2 * n_buf:]
    z_bufs, m_bufs = scratch[:n_buf], scratch[n_buf:]
    assert N_HEADS_A % n_buf == 0 and DIL_LOOKAHEAD < n_buf, "buffer rotation must restart per item"
    assert dilation & (dilation - 1) == 0, "work items are decoded with shifts"
    j = pl.program_id(1)
    sub_len = k_ref.shape[2]
    nblk = sub_len // Q_BLK_A
    n_items = dilation * q_blocks
    lane = lax.broadcasted_iota(jnp.int32, (Q_BLK_A, LANES), 1)
    low_half = lane < HEAD_DIM

    def coords(item):
        if isinstance(item, int):
            return (item // dilation) * Q_BLK_A, item % dilation
        qb = lax.shift_right_logical(item, int(math.log2(dilation)))
        return pl.multiple_of(qb * Q_BLK_A, Q_BLK_A), lax.bitwise_and(item, dilation - 1)

    def window(q0):
        jb = j * q_blocks + q0 // Q_BLK_A
        w0 = pl.multiple_of(jnp.clip(jb * Q_BLK_A - RADIUS, 0, sub_len - K_WIN_A), RADIUS)
        variant = jnp.where(jb == 0, 0, jnp.where(jb == nblk - 1, 2, 1))
        return pl.ds(w0, K_WIN_A), variant

    def scores(item, h):
        q0, r = coords(item)
        win, variant = window(q0)
        g, half = divmod(h, 2)
        cols = slice(g * LANES, (g + 1) * LANES)
        qg = q_ref[0, r, pl.ds(q0, Q_BLK_A), cols]
        keep = low_half if half == 0 else jnp.logical_not(low_half)
        qh = jnp.where(keep, qg, jnp.zeros_like(qg))
        s = lax.dot_general(qh, k_ref[0, r, win, cols], (((1,), (1,)), ((), ())),
                            preferred_element_type=jnp.float32)
        s = s + bias_ref[variant, h]
        m = jnp.max(jnp.maximum(s[:, :LANES], s[:, LANES:]), axis=-1, keepdims=True)
        z_bufs[h % n_buf][...] = s - m
        m_bufs[h % n_buf][...] = jnp.broadcast_to(m, (Q_BLK_A, LANES))

    def attend(item, h):
        q0, r = coords(item)
        win, _ = window(q0)
        cols = slice((h // 2) * LANES, (h // 2 + 1) * LANES)
        p = jnp.exp2(z_bufs[h % n_buf][...])
        den = jnp.sum(p[:, :LANES] + p[:, LANES:], axis=-1, keepdims=True)
        pv = jnp.dot(p.astype(jnp.bfloat16), v_ref[0, r, win, cols],
                     preferred_element_type=jnp.float32)
        return pv, den, m_bufs[h % n_buf][...]

    def work_item(item, carry):
        q0, r = coords(item)
        item_next = jnp.minimum(item + 1, n_items - 1)
        if dilation == 1:
            rows = pl.ds(q0, Q_BLK_A)
        else:
            rows = pl.ds(q0 * dilation + r, Q_BLK_A, stride=dilation)
        halves = []
        for h in range(N_HEADS_A):
            ahead = h + DIL_LOOKAHEAD
            if ahead < N_HEADS_A:
                scores(item, ahead)
            else:
                scores(item_next, ahead - N_HEADS_A)
            halves.append(attend(item, h))
            if h % 2 == 0:
                continue
            g = h // 2
            (pv_lo, den_lo, m_lo), (pv_hi, den_hi, m_hi) = halves
            halves = []
            den_pair = jnp.where(low_half, den_lo, den_hi)
            o_pair = jnp.where(low_half, pv_lo, pv_hi) / den_pair
            l_pair = jnp.where(low_half, m_lo, m_hi) + jnp.log2(den_pair)
            if final:
                l2, l3 = l2_ref[0, g, rows, :], l3_ref[0, g, rows, :]
                top = jnp.maximum(jnp.maximum(l_pair, l2), l3)
                w1, w2, w3 = jnp.exp2(l_pair - top), jnp.exp2(l2 - top), jnp.exp2(l3 - top)
                merged = ((w1 * o_pair + w2 * o2_ref[0, g, rows, :] + w3 * o3_ref[0, g, rows, :])
                          / (w1 + w2 + w3))
                out_ref[0, rows, g * LANES:(g + 1) * LANES] = merged.astype(out_ref.dtype)
            else:
                o_ref[0, g, rows, :] = o_pair
                l_ref[0, g, rows, :] = l_pair
        return carry

    for h in range(DIL_LOOKAHEAD):
        scores(0, h)
    lax.fori_loop(0, n_items, work_item, 0)


def _dilated_call(q, k, v, bias, q_blocks=1, prev=None):
    batch, dilation, sub_len, _ = q.shape
    seq = sub_len * dilation
    final = prev is not None
    tokens = Q_BLK_A * dilation * q_blocks
    n_pairs = N_HEADS_A // 2
    whole = pl.BlockSpec((1, dilation, sub_len, WIDTH_A), lambda b, j: (b, 0, 0, 0))
    in_specs = [
        pl.BlockSpec((1, dilation, q_blocks * Q_BLK_A, WIDTH_A), lambda b, j: (b, 0, j, 0)),
        whole,
        whole,
        pl.BlockSpec((3, N_HEADS_A, Q_BLK_A, K_WIN_A), lambda b, j: (0, 0, 0, 0),
                     pipeline_mode=pl.Buffered(1)),
    ]
    args = [q, k, v, bias]
    slab_spec = pl.BlockSpec((1, n_pairs, tokens, LANES), lambda b, j: (b, 0, j, 0))
    slab_shape = jax.ShapeDtypeStruct((batch, n_pairs, seq, LANES), jnp.float32)
    if final:
        in_specs += [slab_spec] * 4
        args += list(prev)
        out_specs = pl.BlockSpec((1, tokens, WIDTH_A), lambda b, j: (b, j, 0))
        out_shape = jax.ShapeDtypeStruct((batch, seq, WIDTH_A), jnp.bfloat16)
    else:
        out_specs = [slab_spec, slab_spec]
        out_shape = [slab_shape, slab_shape]
    return pl.pallas_call(
        functools.partial(_dilated_kernel, dilation=dilation, final=final, q_blocks=q_blocks,
                          n_buf=DIL_BUFFERS),
        grid=(batch, sub_len // (q_blocks * Q_BLK_A)),
        in_specs=in_specs,
        out_specs=out_specs,
        out_shape=out_shape,
        scratch_shapes=([pltpu.VMEM((Q_BLK_A, K_WIN_A), jnp.float32)] * DIL_BUFFERS
                        + [pltpu.VMEM((Q_BLK_A, LANES), jnp.float32)] * DIL_BUFFERS),
        compiler_params=pltpu.CompilerParams(
            dimension_semantics=("arbitrary", "arbitrary"),
            vmem_limit_bytes=VMEM_LIMIT_BYTES),
        name=f"dilated_d{dilation}",
    )(*args)


def _t5_buckets(rel):
    nb = N_BUCKETS // 2
    max_exact = nb // 2
    ret = (rel > 0).astype(np.int32) * nb
    n = np.abs(rel)
    large = max_exact + (np.log(np.maximum(n, 1) / max_exact)
                         / np.log(MAX_DISTANCE / max_exact) * (nb - max_exact)).astype(np.int32)
    large = np.minimum(large, nb - 1)
    return (ret + np.where(n < max_exact, n, large)).astype(np.int32)


def _dilated_biases(rel_bias):
    r = np.arange(Q_BLK_A)[:, None]
    c = np.arange(K_WIN_A)[None, :]
    rel = np.stack([c - shift - r for shift in (0, RADIUS, 2 * RADIUS)])
    valid = np.abs(rel) <= RADIUS
    buckets = jnp.asarray(np.stack([_t5_buckets(rel * d) for (_, d) in DILATED_PATTERNS]))
    onehot = (buckets[None] == jnp.arange(N_BUCKETS)[:, None, None, None, None]).astype(jnp.float32)
    b = jnp.einsum('nh,npvrc->pvhrc', rel_bias.astype(jnp.float32), onehot,
                   precision=lax.Precision.HIGHEST)
    tables = jnp.where(jnp.asarray(valid)[None, :, None], b * math.log2(math.e), NEG_INF)
    return [tables[p] for p in range(len(DILATED_PATTERNS))]


def _mla_kernel(q_ref, k_ref, vt_ref, o_ref, s_even, s_odd, p_even, p_odd, m_even, m_odd, acc_scr):
    seq = k_ref.shape[1]
    nk = seq // MLA_TK
    nq = seq // MLA_TQ
    heads = range(2)
    s_bufs, p_bufs, m_bufs = (s_even, s_odd), (p_even, p_odd), (m_even, m_odd)
    assert nq % 2 == 0 and nq >= 4

    def q_rows(t):
        start = t * MLA_TQ
        return pl.ds(start if isinstance(t, int) else pl.multiple_of(start, MLA_TQ), MLA_TQ)

    def keys(c):
        return slice(c * MLA_TK, (c + 1) * MLA_TK)

    def stage(score_t=None, exp_t=None, pv_t=None):
        run_max = [None, None]
        for c in range(nk):
            if score_t is not None:
                t, par = score_t
                for hh in heads:
                    lanes = slice(hh * HEAD_SLOT, (hh + 1) * HEAD_SLOT)
                    s = lax.dot_general(k_ref[0, keys(c), lanes], q_ref[0, q_rows(t), lanes],
                                        (((1,), (1,)), ((), ())),
                                        preferred_element_type=jnp.float32)
                    s_bufs[par][hh, keys(c), :] = s
                    mc = jnp.max(s, axis=0, keepdims=True)
                    run_max[hh] = mc if c == 0 else jnp.maximum(run_max[hh], mc)
            if exp_t is not None:
                _, par = exp_t
                for hh in heads:
                    z = s_bufs[par][hh, keys(c), :] - m_bufs[par][hh]
                    p_bufs[par][hh, keys(c), :] = jnp.exp2(z).astype(jnp.bfloat16)
            if pv_t is not None:
                _, par = pv_t
                blk, off = divmod(c * MLA_TK, vt_ref.shape[3])
                for hh in heads:
                    pv = jnp.dot(vt_ref[0, blk, hh * VT_ROWS:(hh + 1) * VT_ROWS, off:off + MLA_TK],
                                 p_bufs[par][hh, keys(c), :],
                                 preferred_element_type=jnp.float32)
                    acc_scr[hh] = pv if c == 0 else acc_scr[hh] + pv
        if score_t is not None:
            for hh in heads:
                m_bufs[score_t[1]][hh] = run_max[hh]
        if pv_t is not None:
            outs = [acc_scr[hh, :V_HEAD_DIM] / acc_scr[hh, VT_ONES_ROW:VT_ONES_ROW + 1]
                    for hh in heads]
            o_ref[0, q_rows(pv_t[0]), :] = jnp.concatenate(outs, axis=0).T.astype(o_ref.dtype)

    stage(score_t=(0, 0))
    stage(score_t=(1, 1), exp_t=(0, 0))

    def stage_pair(i, carry):
        t = 2 * i + 1
        stage(score_t=(t + 1, 0), exp_t=(t, 1), pv_t=(t - 1, 0))
        stage(score_t=(t + 2, 1), exp_t=(t + 1, 0), pv_t=(t, 1))
        return carry

    lax.fori_loop(0, nq // 2 - 1, stage_pair, 0)
    stage(exp_t=(nq - 1, 1), pv_t=(nq - 2, 0))
    stage(pv_t=(nq - 1, 1))


def _mla_call(qm, km, vt, seq):
    batch, n_blk, _, blk_keys = vt.shape
    assert blk_keys % MLA_TK == 0, "a key chunk must not straddle two V^T blocks"
    pair = lambda b, g: (b, 0, g)
    return pl.pallas_call(
        _mla_kernel,
        grid=(batch, N_HEADS_B // 2),
        in_specs=[
            pl.BlockSpec((1, seq, 2 * HEAD_SLOT), pair),
            pl.BlockSpec((1, seq, 2 * HEAD_SLOT), pair),
            pl.BlockSpec((1, n_blk, 2 * VT_ROWS, blk_keys), lambda b, g: (b, 0, g, 0)),
        ],
        out_specs=pl.BlockSpec((1, seq, 2 * V_HEAD_DIM), pair),
        out_shape=jax.ShapeDtypeStruct((batch, seq, WIDTH_B), jnp.bfloat16),
        scratch_shapes=(
            [pltpu.VMEM((2, seq, MLA_TQ), jnp.float32)] * 2
            + [pltpu.VMEM((2, seq, MLA_TQ), jnp.bfloat16)] * 2
            + [pltpu.VMEM((2, 1, MLA_TQ), jnp.float32)] * 2
            + [pltpu.VMEM((2, VT_ROWS, MLA_TQ), jnp.float32)]),
        compiler_params=pltpu.CompilerParams(
            dimension_semantics=("arbitrary", "arbitrary"),
            vmem_limit_bytes=VMEM_LIMIT_BYTES),
        name="mla",
    )(qm, km, vt)


def _mlp_kernel(x_ref, oa_ref, ob_ref, wout_ref, g_ref, wup_ref, wdown_ref, gf_ref, o_ref):
    o_cat = jnp.concatenate([oa_ref[...], ob_ref[...]], axis=-1)
    h = x_ref[...] + jnp.dot(o_cat, wout_ref[...], preferred_element_type=jnp.float32)
    u = _rms(h, g_ref[...]).astype(jnp.bfloat16)
    acc = h
    for c in range(D_FF // FF_CHUNK):
        a = jnp.dot(u, wup_ref[:, c * FF_CHUNK:(c + 1) * FF_CHUNK],
                    preferred_element_type=jnp.float32)
        a = jnp.square(jnp.maximum(a, 0.0)).astype(jnp.bfloat16)
        acc = acc + jnp.dot(a, wdown_ref[c * FF_CHUNK:(c + 1) * FF_CHUNK, :],
                            preferred_element_type=jnp.float32)
    o_ref[...] = _rms(acc, gf_ref[...])


def _mlp_call(x2, oa, ob, wout, g, wup, wdown, gf):
    tokens = x2.shape[0]
    tm = MLP_TM
    row = lambda i: (i, 0)
    const = lambda i: (0, 0)

    def wspec(shape):
        return pl.BlockSpec(shape, const, pipeline_mode=pl.Buffered(1))

    return pl.pallas_call(
        _mlp_kernel,
        grid=(tokens // tm,),
        in_specs=[
            pl.BlockSpec((tm, D_MODEL), row),
            pl.BlockSpec((tm, WIDTH_A), row),
            pl.BlockSpec((tm, WIDTH_B), row),
            wspec((WIDTH_A + WIDTH_B, D_MODEL)),
            wspec((1, D_MODEL)),
            wspec((D_MODEL, D_FF)),
            wspec((D_FF, D_MODEL)),
            wspec((1, D_MODEL)),
        ],
        out_specs=pl.BlockSpec((tm, D_MODEL), row),
        out_shape=jax.ShapeDtypeStruct((tokens, D_MODEL), jnp.float32),
        compiler_params=pltpu.CompilerParams(
            dimension_semantics=("arbitrary",), vmem_limit_bytes=VMEM_LIMIT_BYTES),
        name="mlp",
    )(x2, oa, ob, wout, g, wup, wdown, gf)


def _prep_in_weights(w_in):
    pad = jnp.zeros((D_MODEL, IN_COLS_PAD - w_in.shape[1]), w_in.dtype)
    return jnp.concatenate([w_in, pad], axis=1).astype(jnp.bfloat16)


def _prep_q_weights(w_q_b):
    dqk = QK_NOPE_DIM + QK_ROPE_DIM
    half = QK_ROPE_DIM // 2
    w = w_q_b.reshape(Q_LORA_RANK, N_HEADS_B, dqk)
    nope, rope = w[..., :QK_NOPE_DIM], w[..., QK_NOPE_DIM:]
    slot = jnp.concatenate([nope, rope, -rope[..., half:], rope[..., :half]], axis=-1)
    return slot.reshape(Q_LORA_RANK, QK_WIDTH_B).astype(jnp.bfloat16)


def _prep_kv_weights(w_kv_b):
    half = QK_ROPE_DIM // 2
    w = w_kv_b.reshape(KV_LORA_RANK, N_HEADS_B, QK_NOPE_DIM + V_HEAD_DIM)
    k_nope, v = w[..., :QK_NOPE_DIM], w[..., QK_NOPE_DIM:]
    eye = np.eye(QK_ROPE_DIM, dtype=np.float32)
    rot = np.zeros((QK_ROPE_DIM, QK_ROPE_DIM), np.float32)
    for jcol in range(half):
        rot[half + jcol, jcol] = -1.0
        rot[jcol, half + jcol] = 1.0

    def place(top, rope_block):
        blk = np.zeros((LAT_WIDTH - KV_LORA_RANK, N_HEADS_B, HEAD_SLOT), np.float32)
        for lo in (QK_NOPE_DIM, QK_NOPE_DIM + QK_ROPE_DIM):
            blk[:QK_ROPE_DIM, :, lo:lo + QK_ROPE_DIM] = rope_block[:, None, :]
        top = jnp.concatenate(
            [top, jnp.zeros((KV_LORA_RANK, N_HEADS_B, HEAD_SLOT - QK_NOPE_DIM), top.dtype)], axis=-1)
        full = jnp.concatenate([top, jnp.asarray(blk)], axis=0)
        return full.reshape(LAT_WIDTH, QK_WIDTH_B)

    wk = jnp.concatenate([place(k_nope, eye), place(jnp.zeros_like(k_nope), rot)], axis=1)

    vt_top = jnp.transpose(v, (1, 2, 0))
    vt_top = jnp.concatenate(
        [vt_top, jnp.zeros((N_HEADS_B, VT_ROWS - V_HEAD_DIM, KV_LORA_RANK), v.dtype)], axis=1)
    ones_sel = np.zeros((N_HEADS_B, VT_ROWS, LAT_WIDTH - KV_LORA_RANK), np.float32)
    ones_sel[:, VT_ONES_ROW, -1] = 1.0
    wvt = jnp.concatenate([vt_top, jnp.asarray(ones_sel)], axis=2)
    return wk.astype(jnp.bfloat16), wvt.reshape(N_HEADS_B * VT_ROWS, LAT_WIDTH).astype(jnp.bfloat16)


def _rope_lane_tables(seq):
    inv_freq = ROPE_THETA ** (-jnp.arange(0, QK_ROPE_DIM, 2, dtype=jnp.float32) / QK_ROPE_DIM)
    pos = jnp.arange(seq, dtype=jnp.float32)
    freqs = pos[:, None] * inv_freq[None, :]
    cos, sin = jnp.cos(freqs), jnp.sin(freqs)
    cos2, sin2 = jnp.concatenate([cos, cos], axis=1), jnp.concatenate([sin, sin], axis=1)
    ones = jnp.ones((seq, QK_NOPE_DIM), jnp.float32)
    q_scale = (QK_NOPE_DIM + QK_ROPE_DIM) ** -0.5 * math.log2(math.e)
    q_table = jnp.concatenate([ones, cos2, sin2], axis=1) * q_scale
    k_cos = jnp.concatenate([ones, cos2, cos2], axis=1)
    k_sin = jnp.concatenate([jnp.zeros_like(ones), sin2, sin2], axis=1)
    return q_table, k_cos, k_sin


def kernel(x, mix_norm_g, w_in, q_norm_g, w_q_b, kv_norm_g, w_kv_b, w_out,
           mlp_norm_g, w_up, w_down, rel_bias, final_norm_g):
    batch, seq, _ = x.shape
    depth = w_in.shape[0]
    assert depth == 1, "the final norm is fused into the single layer's MLP kernel"
    tq, ck, sk = _rope_lane_tables(seq)
    biases = _dilated_biases(rel_bias)
    layer = 0
    x2 = x.reshape(batch * seq, D_MODEL)
    wk, wvt = _prep_kv_weights(w_kv_b[layer])
    qa, ka, va, qa4, ka4, va4, qa16, ka16, va16, qm, km, vt = _proj_call(
        x2, mix_norm_g[layer][None], _prep_in_weights(w_in[layer]),
        q_norm_g[layer][None], _prep_q_weights(w_q_b[layer]),
        kv_norm_g[layer][None], wk, wvt, tq, ck, sk, batch, seq)

    shape_a = (batch, 1, seq, WIDTH_A)
    q_blocks = dict(zip([d for (_, d) in DILATED_PATTERNS], DIL_Q_BLOCKS))
    o16, l16 = _dilated_call(qa16, ka16, va16, biases[2], q_blocks=q_blocks[16])
    o4, l4 = _dilated_call(qa4, ka4, va4, biases[1], q_blocks=q_blocks[4])
    oa = _dilated_call(qa.reshape(shape_a), ka.reshape(shape_a), va.reshape(shape_a), biases[0],
                       q_blocks=q_blocks[1], prev=(o4, l4, o16, l16))

    ob = _mla_call(qm.reshape(batch, seq, QK_WIDTH_B), km.reshape(batch, seq, QK_WIDTH_B), vt, seq)

    out = _mlp_call(
        x2, oa.reshape(batch * seq, WIDTH_A), ob.reshape(batch * seq, WIDTH_B),
        w_out[layer].astype(jnp.bfloat16), mlp_norm_g[layer][None],
        w_up[layer].astype(jnp.bfloat16), w_down[layer].astype(jnp.bfloat16),
        final_norm_g[None])
    return out.reshape(batch, seq, D_MODEL)
```

```python
import functools
import math

import jax
import jax.numpy as jnp
import numpy as np
from jax import lax
from jax.experimental import pallas as pl
from jax.experimental.pallas import tpu as pltpu

D_MODEL = 1024
HEAD_DIM = 64
N_HEADS_A = 8
DILATED_PATTERNS = ((128, 1), (512, 4), (2048, 16))
N_HEADS_B = 8
Q_LORA_RANK = 256
KV_LORA_RANK = 128
QK_NOPE_DIM = 64
QK_ROPE_DIM = 32
V_HEAD_DIM = 64
ROPE_THETA = 10000.0
N_BUCKETS = 32
MAX_DISTANCE = 1024
D_FF = 4 * D_MODEL
NORM_EPS = 1e-6
NEG_INF = -1e30
WIDTH_A = N_HEADS_A * HEAD_DIM
WIDTH_B = N_HEADS_B * V_HEAD_DIM

LANES = 128
VMEM_LIMIT_BYTES = 56 * 1024 * 1024

HEAD_SLOT = LANES
QK_WIDTH_B = N_HEADS_B * HEAD_SLOT
VT_ROWS = 80
VT_ONES_ROW = V_HEAD_DIM
IN_COLS_PAD = 2048
CQ_OFF = 3 * WIDTH_A
LAT_WIDTH = 2 * LANES
RADIUS = 64
Q_BLK_A = 2 * RADIUS
K_WIN_A = 4 * RADIUS

PROJ_TM = 512
MLP_TM = 1024
MLA_TQ = 256
DIL_LOOKAHEAD = 3
DIL_BUFFERS = 4
DIL_Q_BLOCKS = (8, 4, 1)
MLA_TK = 512
FF_CHUNK = 1024


def _rms(xf, g):
    return xf * lax.rsqrt(jnp.mean(xf * xf, axis=-1, keepdims=True) + NORM_EPS) * g


def _proj_kernel(x_ref, g_ref, win_ref, qg_ref, wq_ref, kvg_ref, wk_ref, wvt_ref,
                 tq_ref, ck_ref, sk_ref,
                 qa_ref, ka_ref, va_ref, qa4_ref, ka4_ref, va4_ref, qa16_ref, ka16_ref, va16_ref,
                 qm_ref, km_ref, vt_ref, slab_scr, part_scr):
    x = x_ref[...]
    u = _rms(x, g_ref[...]).astype(jnp.bfloat16)

    def in_proj(lo, hi):
        return jnp.dot(u, win_ref[:, lo:hi], preferred_element_type=jnp.float32)

    tm = x.shape[0]
    n_slabs = WIDTH_A // LANES
    scales = (HEAD_DIM ** -0.5 * math.log2(math.e), 1.0, 1.0)
    groups = ((qa_ref, qa4_ref, qa16_ref), (ka_ref, ka4_ref, ka16_ref), (va_ref, va4_ref, va16_ref))
    for a, (nat_ref, *strided_refs) in enumerate(groups):
        t = in_proj(a * WIDTH_A, (a + 1) * WIDTH_A) * scales[a]
        nat_ref[...] = t.astype(jnp.bfloat16)
        ref4, ref16 = strided_refs
        d4, d16 = ref4.shape[1], ref16.shape[1]
        assert d16 == d4 * d4, "the second copy is a stride-d4 pass over the first"
        for g in range(n_slabs):
            cols = slice(g * LANES, (g + 1) * LANES)
            slab_scr[a * n_slabs + g] = t[:, cols]
            for r in range(d4):
                part = slab_scr[a * n_slabs + g, pl.ds(r, tm // d4, stride=d4), :]
                ref4[0, r, :, cols] = part.astype(jnp.bfloat16)
                part_scr[a * n_slabs + g, r] = part
            for r in range(d16):
                part = part_scr[a * n_slabs + g, r % d4, pl.ds(r // d4, tm // d16, stride=d4), :]
                ref16[0, r, :, cols] = part.astype(jnp.bfloat16)
    proj = in_proj(CQ_OFF, IN_COLS_PAD)

    cq = _rms(proj[:, :Q_LORA_RANK], qg_ref[...]).astype(jnp.bfloat16)
    q2 = jnp.dot(cq, wq_ref[...], preferred_element_type=jnp.float32)
    qm_ref[...] = (q2 * jnp.tile(tq_ref[...], (1, N_HEADS_B))).astype(jnp.bfloat16)

    ckv = _rms(proj[:, Q_LORA_RANK:Q_LORA_RANK + KV_LORA_RANK], kvg_ref[...])
    hi = proj[:, Q_LORA_RANK + KV_LORA_RANK:]
    lane = lax.broadcasted_iota(jnp.int32, hi.shape, 1)
    hi = jnp.where(lane == LANES - 1, 1.0, hi)
    lat = jnp.concatenate([ckv, hi], axis=-1).astype(jnp.bfloat16)
    k2 = jnp.dot(lat, wk_ref[...], preferred_element_type=jnp.float32)
    ck_t = jnp.tile(ck_ref[...], (1, N_HEADS_B))
    sk_t = jnp.tile(sk_ref[...], (1, N_HEADS_B))
    km_ref[...] = (k2[:, :QK_WIDTH_B] * ck_t + k2[:, QK_WIDTH_B:] * sk_t).astype(jnp.bfloat16)
    vt = lax.dot_general(wvt_ref[...], lat, (((1,), (1,)), ((), ())),
                         preferred_element_type=jnp.float32)
    vt_ref[0, 0] = vt.astype(jnp.bfloat16)


def _proj_call(x2, g, win, qg, wq, kvg, wk, wvt, tq, ck, sk, batch, seq):
    tokens = x2.shape[0]
    tm = PROJ_TM
    sblk = seq // tm
    row = lambda i: (i, 0)
    const = lambda i: (0, 0)
    pos = lambda i: (i % sblk, 0)

    def wspec(shape):
        return pl.BlockSpec(shape, const, pipeline_mode=pl.Buffered(1))

    strides = [d for (_, d) in DILATED_PATTERNS if d > 1]

    def strided_spec(d):
        return pl.BlockSpec((1, d, tm // d, WIDTH_A), lambda i: (i // sblk, 0, i % sblk, 0))

    bf = jnp.bfloat16
    return pl.pallas_call(
        _proj_kernel,
        grid=(tokens // tm,),
        in_specs=[
            pl.BlockSpec((tm, D_MODEL), row),
            wspec((1, D_MODEL)),
            wspec((D_MODEL, IN_COLS_PAD)),
            wspec((1, Q_LORA_RANK)),
            wspec((Q_LORA_RANK, QK_WIDTH_B)),
            wspec((1, KV_LORA_RANK)),
            wspec((LAT_WIDTH, 2 * QK_WIDTH_B)),
            wspec((N_HEADS_B * VT_ROWS, LAT_WIDTH)),
            pl.BlockSpec((tm, HEAD_SLOT), pos),
            pl.BlockSpec((tm, HEAD_SLOT), pos),
            pl.BlockSpec((tm, HEAD_SLOT), pos),
        ],
        out_specs=[
            pl.BlockSpec((tm, WIDTH_A), row),
            pl.BlockSpec((tm, WIDTH_A), row),
            pl.BlockSpec((tm, WIDTH_A), row),
            *[strided_spec(d) for d in strides for _ in range(3)],
            pl.BlockSpec((tm, QK_WIDTH_B), row),
            pl.BlockSpec((tm, QK_WIDTH_B), row),
            pl.BlockSpec((1, 1, N_HEADS_B * VT_ROWS, tm), lambda i: (i // sblk, i % sblk, 0, 0)),
        ],
        out_shape=[
            jax.ShapeDtypeStruct((tokens, WIDTH_A), bf),
            jax.ShapeDtypeStruct((tokens, WIDTH_A), bf),
            jax.ShapeDtypeStruct((tokens, WIDTH_A), bf),
            *[jax.ShapeDtypeStruct((batch, d, seq // d, WIDTH_A), bf) for d in strides for _ in range(3)],
            jax.ShapeDtypeStruct((tokens, QK_WIDTH_B), bf),
            jax.ShapeDtypeStruct((tokens, QK_WIDTH_B), bf),
            jax.ShapeDtypeStruct((batch, sblk, N_HEADS_B * VT_ROWS, tm), bf),
        ],
        scratch_shapes=[
            pltpu.VMEM((3 * WIDTH_A // LANES, tm, LANES), jnp.float32),
            pltpu.VMEM((3 * WIDTH_A // LANES, strides[0], tm // strides[0], LANES), jnp.float32),
        ],
        compiler_params=pltpu.CompilerParams(
            dimension_semantics=("arbitrary",), vmem_limit_bytes=VMEM_LIMIT_BYTES),
        name="proj",
    )(x2, g, win, qg, wq, kvg, wk, wvt, tq, ck, sk)


def _dilated_kernel(*refs, dilation, final, q_blocks, n_buf):
    if final:
        q_ref, k_ref, v_ref, bias_ref, o2_ref, l2_ref, o3_ref, l3_ref, out_ref = refs[:9]
    else:
        q_ref, k_ref, v_ref, bias_ref, o_ref, l_ref = refs[:6]
    scratch = refs[-2 * n_buf:]
    z_bufs, m_bufs = scratch[:n_buf], scratch[n_buf:]
    assert N_HEADS_A % n_buf == 0 and DIL_LOOKAHEAD < n_buf, "buffer rotation must restart per item"
    assert dilation & (dilation - 1) == 0, "work items are decoded with shifts"
    j = pl.program_id(1)
    sub_len = k_ref.shape[2]
    nblk = sub_len // Q_BLK_A
    n_items = dilation * q_blocks
    lane = lax.broadcasted_iota(jnp.int32, (Q_BLK_A, LANES), 1)
    low_half = lane < HEAD_DIM

    def coords(item):
        if isinstance(item, int):
            return (item // dilation) * Q_BLK_A, item % dilation
        qb = lax.shift_right_logical(item, int(math.log2(dilation)))
        return pl.multiple_of(qb * Q_BLK_A, Q_BLK_A), lax.bitwise_and(item, dilation - 1)

    def window(q0):
        jb = j * q_blocks + q0 // Q_BLK_A
        w0 = pl.multiple_of(jnp.clip(jb * Q_BLK_A - RADIUS, 0, sub_len - K_WIN_A), RADIUS)
        variant = jnp.where(jb == 0, 0, jnp.where(jb == nblk - 1, 2, 1))
        return pl.ds(w0, K_WIN_A), variant

    def scores(item, h):
        q0, r = coords(item)
        win, variant = window(q0)
        g, half = divmod(h, 2)
        cols = slice(g * LANES, (g + 1) * LANES)
        qg = q_ref[0, r, pl.ds(q0, Q_BLK_A), cols]
        keep = low_half if half == 0 else jnp.logical_not(low_half)
        qh = jnp.where(keep, qg, jnp.zeros_like(qg))
        s = lax.dot_general(qh, k_ref[0, r, win, cols], (((1,), (1,)), ((), ())),
                            preferred_element_type=jnp.float32)
        s = s + bias_ref[variant, h]
        m = jnp.max(jnp.maximum(s[:, :LANES], s[:, LANES:]), axis=-1, keepdims=True)
        z_bufs[h % n_buf][...] = s - m
        m_bufs[h % n_buf][...] = jnp.broadcast_to(m, (Q_BLK_A, LANES))

    def attend(item, h):
        q0, r = coords(item)
        win, _ = window(q0)
        cols = slice((h // 2) * LANES, (h // 2 + 1) * LANES)
        p = jnp.exp2(z_bufs[h % n_buf][...])
        den = jnp.sum(p[:, :LANES] + p[:, LANES:], axis=-1, keepdims=True)
        pv = jnp.dot(p.astype(jnp.bfloat16), v_ref[0, r, win, cols],
                     preferred_element_type=jnp.float32)
        return pv, den, m_bufs[h % n_buf][...]

    def work_item(item, carry):
        q0, r = coords(item)
        item_next = jnp.minimum(item + 1, n_items - 1)
        if dilation == 1:
            rows = pl.ds(q0, Q_BLK_A)
        else:
            rows = pl.ds(q0 * dilation + r, Q_BLK_A, stride=dilation)
        halves = []
        for h in range(N_HEADS_A):
            ahead = h + DIL_LOOKAHEAD
            if ahead < N_HEADS_A:
                scores(item, ahead)
            else:
                scores(item_next, ahead - N_HEADS_A)
            halves.append(attend(item, h))
            if h % 2 == 0:
                continue
            g = h // 2
            (pv_lo, den_lo, m_lo), (pv_hi, den_hi, m_hi) = halves
            halves = []
            den_pair = jnp.where(low_half, den_lo, den_hi)
            o_pair = jnp.where(low_half, pv_lo, pv_hi) / den_pair
            l_pair = jnp.where(low_half, m_lo, m_hi) + jnp.log2(den_pair)
            if final:
                l2, l3 = l2_ref[0, g, rows, :], l3_ref[0, g, rows, :]
                top = jnp.maximum(jnp.maximum(l_pair, l2), l3)
                w1, w2, w3 = jnp.exp2(l_pair - top), jnp.exp2(l2 - top), jnp.exp2(l3 - top)
                o2, o3 = (pltpu.unpack_elementwise(
                    ref[0, g // 2, rows, :], index=g % 2, packed_dtype=jnp.bfloat16,
                    unpacked_dtype=jnp.float32) for ref in (o2_ref, o3_ref))
                merged = (w1 * o_pair + w2 * o2 + w3 * o3) / (w1 + w2 + w3)
                out_ref[0, rows, g * LANES:(g + 1) * LANES] = merged.astype(out_ref.dtype)
            else:
                l_ref[0, g, rows, :] = l_pair
                if g % 2 == 0:
                    o_even = o_pair
                else:
                    o_ref[0, g // 2, rows, :] = pltpu.pack_elementwise(
                        [o_even, o_pair], packed_dtype=jnp.bfloat16)
        return carry

    for h in range(DIL_LOOKAHEAD):
        scores(0, h)
    lax.fori_loop(0, n_items, work_item, 0)


def _dilated_call(q, k, v, bias, q_blocks=1, prev=None):
    batch, dilation, sub_len, _ = q.shape
    seq = sub_len * dilation
    final = prev is not None
    tokens = Q_BLK_A * dilation * q_blocks
    n_pairs = N_HEADS_A // 2
    whole = pl.BlockSpec((1, dilation, sub_len, WIDTH_A), lambda b, j: (b, 0, 0, 0))
    in_specs = [
        pl.BlockSpec((1, dilation, q_blocks * Q_BLK_A, WIDTH_A), lambda b, j: (b, 0, j, 0)),
        whole,
        whole,
        pl.BlockSpec((3, N_HEADS_A, Q_BLK_A, K_WIN_A), lambda b, j: (0, 0, 0, 0),
                     pipeline_mode=pl.Buffered(1)),
    ]
    args = [q, k, v, bias]
    def slab_spec(n):
        return pl.BlockSpec((1, n, tokens, LANES), lambda b, j: (b, 0, j, 0))

    slab_specs = [slab_spec(n_pairs // 2), slab_spec(n_pairs)]
    if final:
        in_specs += slab_specs * 2
        args += list(prev)
        out_specs = pl.BlockSpec((1, tokens, WIDTH_A), lambda b, j: (b, j, 0))
        out_shape = jax.ShapeDtypeStruct((batch, seq, WIDTH_A), jnp.bfloat16)
    else:
        out_specs = slab_specs
        out_shape = [jax.ShapeDtypeStruct((batch, n_pairs // 2, seq, LANES), jnp.uint32),
                     jax.ShapeDtypeStruct((batch, n_pairs, seq, LANES), jnp.float32)]
    return pl.pallas_call(
        functools.partial(_dilated_kernel, dilation=dilation, final=final, q_blocks=q_blocks,
                          n_buf=DIL_BUFFERS),
        grid=(batch, sub_len // (q_blocks * Q_BLK_A)),
        in_specs=in_specs,
        out_specs=out_specs,
        out_shape=out_shape,
        scratch_shapes=([pltpu.VMEM((Q_BLK_A, K_WIN_A), jnp.float32)] * DIL_BUFFERS
                        + [pltpu.VMEM((Q_BLK_A, LANES), jnp.float32)] * DIL_BUFFERS),
        compiler_params=pltpu.CompilerParams(
            dimension_semantics=("arbitrary", "arbitrary"),
            vmem_limit_bytes=VMEM_LIMIT_BYTES),
        name=f"dilated_d{dilation}",
    )(*args)


def _t5_buckets(rel):
    nb = N_BUCKETS // 2
    max_exact = nb // 2
    ret = (rel > 0).astype(np.int32) * nb
    n = np.abs(rel)
    large = max_exact + (np.log(np.maximum(n, 1) / max_exact)
                         / np.log(MAX_DISTANCE / max_exact) * (nb - max_exact)).astype(np.int32)
    large = np.minimum(large, nb - 1)
    return (ret + np.where(n < max_exact, n, large)).astype(np.int32)


def _dilated_biases(rel_bias):
    r = np.arange(Q_BLK_A)[:, None]
    c = np.arange(K_WIN_A)[None, :]
    rel = np.stack([c - shift - r for shift in (0, RADIUS, 2 * RADIUS)])
    valid = np.abs(rel) <= RADIUS
    buckets = jnp.asarray(np.stack([_t5_buckets(rel * d) for (_, d) in DILATED_PATTERNS]))
    onehot = (buckets[None] == jnp.arange(N_BUCKETS)[:, None, None, None, None]).astype(jnp.float32)
    b = jnp.einsum('nh,npvrc->pvhrc', rel_bias.astype(jnp.float32), onehot,
                   precision=lax.Precision.HIGHEST)
    tables = jnp.where(jnp.asarray(valid)[None, :, None], b * math.log2(math.e), NEG_INF)
    return [tables[p] for p in range(len(DILATED_PATTERNS))]


def _mla_kernel(q_ref, k_ref, vt_ref, o_ref, s_even, s_odd, p_even, p_odd, m_even, m_odd, acc_scr):
    seq = k_ref.shape[1]
    nk = seq // MLA_TK
    nq = seq // MLA_TQ
    heads = range(2)
    s_bufs, p_bufs, m_bufs = (s_even, s_odd), (p_even, p_odd), (m_even, m_odd)
    assert nq % 2 == 0 and nq >= 4

    def q_rows(t):
        start = t * MLA_TQ
        return pl.ds(start if isinstance(t, int) else pl.multiple_of(start, MLA_TQ), MLA_TQ)

    def keys(c):
        return slice(c * MLA_TK, (c + 1) * MLA_TK)

    def stage(score_t=None, exp_t=None, pv_t=None):
        run_max = [None, None]
        for c in range(nk):
            if score_t is not None:
                t, par = score_t
                for hh in heads:
                    lanes = slice(hh * HEAD_SLOT, (hh + 1) * HEAD_SLOT)
                    s = lax.dot_general(k_ref[0, keys(c), lanes], q_ref[0, q_rows(t), lanes],
                                        (((1,), (1,)), ((), ())),
                                        preferred_element_type=jnp.float32)
                    s_bufs[par][hh, keys(c), :] = s
                    mc = jnp.max(s, axis=0, keepdims=True)
                    run_max[hh] = mc if c == 0 else jnp.maximum(run_max[hh], mc)
            if exp_t is not None:
                _, par = exp_t
                for hh in heads:
                    z = s_bufs[par][hh, keys(c), :] - m_bufs[par][hh]
                    p_bufs[par][hh, keys(c), :] = jnp.exp2(z).astype(jnp.bfloat16)
            if pv_t is not None:
                _, par = pv_t
                blk, off = divmod(c * MLA_TK, vt_ref.shape[3])
                for hh in heads:
                    pv = jnp.dot(vt_ref[0, blk, hh * VT_ROWS:(hh + 1) * VT_ROWS, off:off + MLA_TK],
                                 p_bufs[par][hh, keys(c), :],
                                 preferred_element_type=jnp.float32)
                    acc_scr[hh] = pv if c == 0 else acc_scr[hh] + pv
        if score_t is not None:
            for hh in heads:
                m_bufs[score_t[1]][hh] = run_max[hh]
        if pv_t is not None:
            outs = [acc_scr[hh, :V_HEAD_DIM] / acc_scr[hh, VT_ONES_ROW:VT_ONES_ROW + 1]
                    for hh in heads]
            o_ref[0, q_rows(pv_t[0]), :] = jnp.concatenate(outs, axis=0).T.astype(o_ref.dtype)

    stage(score_t=(0, 0))
    stage(score_t=(1, 1), exp_t=(0, 0))

    def stage_pair(i, carry):
        t = 2 * i + 1
        stage(score_t=(t + 1, 0), exp_t=(t, 1), pv_t=(t - 1, 0))
        stage(score_t=(t + 2, 1), exp_t=(t + 1, 0), pv_t=(t, 1))
        return carry

    lax.fori_loop(0, nq // 2 - 1, stage_pair, 0)
    stage(exp_t=(nq - 1, 1), pv_t=(nq - 2, 0))
    stage(pv_t=(nq - 1, 1))


def _mla_call(qm, km, vt, seq):
    batch, n_blk, _, blk_keys = vt.shape
    assert blk_keys % MLA_TK == 0, "a key chunk must not straddle two V^T blocks"
    pair = lambda b, g: (b, 0, g)
    return pl.pallas_call(
        _mla_kernel,
        grid=(batch, N_HEADS_B // 2),
        in_specs=[
            pl.BlockSpec((1, seq, 2 * HEAD_SLOT), pair),
            pl.BlockSpec((1, seq, 2 * HEAD_SLOT), pair),
            pl.BlockSpec((1, n_blk, 2 * VT_ROWS, blk_keys), lambda b, g: (b, 0, g, 0)),
        ],
        out_specs=pl.BlockSpec((1, seq, 2 * V_HEAD_DIM), pair),
        out_shape=jax.ShapeDtypeStruct((batch, seq, WIDTH_B), jnp.bfloat16),
        scratch_shapes=(
            [pltpu.VMEM((2, seq, MLA_TQ), jnp.float32)] * 2
            + [pltpu.VMEM((2, seq, MLA_TQ), jnp.bfloat16)] * 2
            + [pltpu.VMEM((2, 1, MLA_TQ), jnp.float32)] * 2
            + [pltpu.VMEM((2, VT_ROWS, MLA_TQ), jnp.float32)]),
        compiler_params=pltpu.CompilerParams(
            dimension_semantics=("arbitrary", "arbitrary"),
            vmem_limit_bytes=VMEM_LIMIT_BYTES),
        name="mla",
    )(qm, km, vt)


def _mlp_kernel(x_ref, oa_ref, ob_ref, wout_ref, g_ref, wup_ref, wdown_ref, gf_ref, o_ref):
    o_cat = jnp.concatenate([oa_ref[...], ob_ref[...]], axis=-1)
    h = x_ref[...] + jnp.dot(o_cat, wout_ref[...], preferred_element_type=jnp.float32)
    u = _rms(h, g_ref[...]).astype(jnp.bfloat16)
    acc = h
    for c in range(D_FF // FF_CHUNK):
        a = jnp.dot(u, wup_ref[:, c * FF_CHUNK:(c + 1) * FF_CHUNK],
                    preferred_element_type=jnp.float32)
        a = jnp.square(jnp.maximum(a, 0.0)).astype(jnp.bfloat16)
        acc = acc + jnp.dot(a, wdown_ref[c * FF_CHUNK:(c + 1) * FF_CHUNK, :],
                            preferred_element_type=jnp.float32)
    o_ref[...] = _rms(acc, gf_ref[...])


def _mlp_call(x2, oa, ob, wout, g, wup, wdown, gf):
    tokens = x2.shape[0]
    tm = MLP_TM
    row = lambda i: (i, 0)
    const = lambda i: (0, 0)

    def wspec(shape):
        return pl.BlockSpec(shape, const, pipeline_mode=pl.Buffered(1))

    return pl.pallas_call(
        _mlp_kernel,
        grid=(tokens // tm,),
        in_specs=[
            pl.BlockSpec((tm, D_MODEL), row),
            pl.BlockSpec((tm, WIDTH_A), row),
            pl.BlockSpec((tm, WIDTH_B), row),
            wspec((WIDTH_A + WIDTH_B, D_MODEL)),
            wspec((1, D_MODEL)),
            wspec((D_MODEL, D_FF)),
            wspec((D_FF, D_MODEL)),
            wspec((1, D_MODEL)),
        ],
        out_specs=pl.BlockSpec((tm, D_MODEL), row),
        out_shape=jax.ShapeDtypeStruct((tokens, D_MODEL), jnp.float32),
        compiler_params=pltpu.CompilerParams(
            dimension_semantics=("arbitrary",), vmem_limit_bytes=VMEM_LIMIT_BYTES),
        name="mlp",
    )(x2, oa, ob, wout, g, wup, wdown, gf)


def _prep_in_weights(w_in):
    pad = jnp.zeros((D_MODEL, IN_COLS_PAD - w_in.shape[1]), w_in.dtype)
    return jnp.concatenate([w_in, pad], axis=1).astype(jnp.bfloat16)


def _prep_q_weights(w_q_b):
    dqk = QK_NOPE_DIM + QK_ROPE_DIM
    half = QK_ROPE_DIM // 2
    w = w_q_b.reshape(Q_LORA_RANK, N_HEADS_B, dqk)
    nope, rope = w[..., :QK_NOPE_DIM], w[..., QK_NOPE_DIM:]
    slot = jnp.concatenate([nope, rope, -rope[..., half:], rope[..., :half]], axis=-1)
    return slot.reshape(Q_LORA_RANK, QK_WIDTH_B).astype(jnp.bfloat16)


def _prep_kv_weights(w_kv_b):
    half = QK_ROPE_DIM // 2
    w = w_kv_b.reshape(KV_LORA_RANK, N_HEADS_B, QK_NOPE_DIM + V_HEAD_DIM)
    k_nope, v = w[..., :QK_NOPE_DIM], w[..., QK_NOPE_DIM:]
    eye = np.eye(QK_ROPE_DIM, dtype=np.float32)
    rot = np.zeros((QK_ROPE_DIM, QK_ROPE_DIM), np.float32)
    for jcol in range(half):
        rot[half + jcol, jcol] = -1.0
        rot[jcol, half + jcol] = 1.0

    def place(top, rope_block):
        blk = np.zeros((LAT_WIDTH - KV_LORA_RANK, N_HEADS_B, HEAD_SLOT), np.float32)
        for lo in (QK_NOPE_DIM, QK_NOPE_DIM + QK_ROPE_DIM):
            blk[:QK_ROPE_DIM, :, lo:lo + QK_ROPE_DIM] = rope_block[:, None, :]
        top = jnp.concatenate(
            [top, jnp.zeros((KV_LORA_RANK, N_HEADS_B, HEAD_SLOT - QK_NOPE_DIM), top.dtype)], axis=-1)
        full = jnp.concatenate([top, jnp.asarray(blk)], axis=0)
        return full.reshape(LAT_WIDTH, QK_WIDTH_B)

    wk = jnp.concatenate([place(k_nope, eye), place(jnp.zeros_like(k_nope), rot)], axis=1)

    vt_top = jnp.transpose(v, (1, 2, 0))
    vt_top = jnp.concatenate(
        [vt_top, jnp.zeros((N_HEADS_B, VT_ROWS - V_HEAD_DIM, KV_LORA_RANK), v.dtype)], axis=1)
    ones_sel = np.zeros((N_HEADS_B, VT_ROWS, LAT_WIDTH - KV_LORA_RANK), np.float32)
    ones_sel[:, VT_ONES_ROW, -1] = 1.0
    wvt = jnp.concatenate([vt_top, jnp.asarray(ones_sel)], axis=2)
    return wk.astype(jnp.bfloat16), wvt.reshape(N_HEADS_B * VT_ROWS, LAT_WIDTH).astype(jnp.bfloat16)


def _rope_lane_tables(seq):
    inv_freq = ROPE_THETA ** (-np.arange(0, QK_ROPE_DIM, 2, dtype=np.float64) / QK_ROPE_DIM)
    freqs = np.arange(seq, dtype=np.float64)[:, None] * inv_freq[None, :]
    cos, sin = np.cos(freqs), np.sin(freqs)
    cos2, sin2 = np.concatenate([cos, cos], axis=1), np.concatenate([sin, sin], axis=1)
    ones = np.ones((seq, QK_NOPE_DIM))
    q_scale = (QK_NOPE_DIM + QK_ROPE_DIM) ** -0.5 * math.log2(math.e)
    q_table = np.concatenate([ones, cos2, sin2], axis=1) * q_scale
    k_cos = np.concatenate([ones, cos2, cos2], axis=1)
    k_sin = np.concatenate([np.zeros_like(ones), sin2, sin2], axis=1)
    return tuple(jnp.asarray(t, jnp.float32) for t in (q_table, k_cos, k_sin))


def kernel(x, mix_norm_g, w_in, q_norm_g, w_q_b, kv_norm_g, w_kv_b, w_out,
           mlp_norm_g, w_up, w_down, rel_bias, final_norm_g):
    batch, seq, _ = x.shape
    depth = w_in.shape[0]
    assert depth == 1, "the final norm is fused into the single layer's MLP kernel"
    tq, ck, sk = _rope_lane_tables(seq)
    biases = _dilated_biases(rel_bias)
    layer = 0
    x2 = x.reshape(batch * seq, D_MODEL)
    wk, wvt = _prep_kv_weights(w_kv_b[layer])
    qa, ka, va, qa4, ka4, va4, qa16, ka16, va16, qm, km, vt = _proj_call(
        x2, mix_norm_g[layer][None], _prep_in_weights(w_in[layer]),
        q_norm_g[layer][None], _prep_q_weights(w_q_b[layer]),
        kv_norm_g[layer][None], wk, wvt, tq, ck, sk, batch, seq)

    shape_a = (batch, 1, seq, WIDTH_A)
    q_blocks = dict(zip([d for (_, d) in DILATED_PATTERNS], DIL_Q_BLOCKS))
    o16, l16 = _dilated_call(qa16, ka16, va16, biases[2], q_blocks=q_blocks[16])
    o4, l4 = _dilated_call(qa4, ka4, va4, biases[1], q_blocks=q_blocks[4])
    oa = _dilated_call(qa.reshape(shape_a), ka.reshape(shape_a), va.reshape(shape_a), biases[0],
                       q_blocks=q_blocks[1], prev=(o4, l4, o16, l16))

    ob = _mla_call(qm.reshape(batch, seq, QK_WIDTH_B), km.reshape(batch, seq, QK_WIDTH_B), vt, seq)

    out = _mlp_call(
        x2, oa.reshape(batch * seq, WIDTH_A), ob.reshape(batch * seq, WIDTH_B),
        w_out[layer].astype(jnp.bfloat16), mlp_norm_g[layer][None],
        w_up[layer].astype(jnp.bfloat16), w_down[layer].astype(jnp.bfloat16),
        final_norm_g[None])
    return out.reshape(batch, seq, D_MODEL)
```

```python
import functools
import math

import jax
import jax.numpy as jnp
import numpy as np
from jax import lax
from jax.experimental import pallas as pl
from jax.experimental.pallas import tpu as pltpu

D_MODEL = 1024
HEAD_DIM = 64
N_HEADS_A = 8
DILATED_PATTERNS = ((128, 1), (512, 4), (2048, 16))
N_HEADS_B = 8
Q_LORA_RANK = 256
KV_LORA_RANK = 128
QK_NOPE_DIM = 64
QK_ROPE_DIM = 32
V_HEAD_DIM = 64
ROPE_THETA = 10000.0
N_BUCKETS = 32
MAX_DISTANCE = 1024
D_FF = 4 * D_MODEL
NORM_EPS = 1e-6
NEG_INF = -1e30
WIDTH_A = N_HEADS_A * HEAD_DIM
WIDTH_B = N_HEADS_B * V_HEAD_DIM

LANES = 128
VMEM_LIMIT_BYTES = 56 * 1024 * 1024

HEAD_SLOT = LANES
QK_WIDTH_B = N_HEADS_B * HEAD_SLOT
VT_ROWS = 80
VT_ONES_ROW = V_HEAD_DIM
IN_COLS_PAD = 2048
CQ_OFF = 3 * WIDTH_A
LAT_WIDTH = 2 * LANES
RADIUS = 64
Q_BLK_A = 2 * RADIUS
K_WIN_A = 4 * RADIUS

PROJ_TM = 512
MLP_TM = 1024
MLA_TQ = 256
DIL_LOOKAHEAD = 3
DIL_BUFFERS = 4
DIL_Q_BLOCKS = (8, 4, 1)
MLA_TK = 512
FF_CHUNK = 1024


def _rms(xf, g):
    return xf * lax.rsqrt(jnp.mean(xf * xf, axis=-1, keepdims=True) + NORM_EPS) * g


def _proj_kernel(x_ref, g_ref, win_ref, qg_ref, wq_ref, kvg_ref, wk_ref, wvt_ref,
                 tq_ref, ck_ref, sk_ref,
                 qa_ref, ka_ref, va_ref, qa4_ref, ka4_ref, va4_ref, qa16_ref, ka16_ref, va16_ref,
                 qm_ref, km_ref, vt_ref, slab_scr, part_scr):
    x = x_ref[...]
    u = _rms(x, g_ref[...]).astype(jnp.bfloat16)

    def in_proj(lo, hi):
        return jnp.dot(u, win_ref[:, lo:hi], preferred_element_type=jnp.float32)

    tm = x.shape[0]
    n_slabs = WIDTH_A // LANES
    scales = (HEAD_DIM ** -0.5 * math.log2(math.e), 1.0, 1.0)
    groups = ((qa_ref, qa4_ref, qa16_ref), (ka_ref, ka4_ref, ka16_ref), (va_ref, va4_ref, va16_ref))
    for a, (nat_ref, *strided_refs) in enumerate(groups):
        t = in_proj(a * WIDTH_A, (a + 1) * WIDTH_A) * scales[a]
        nat_ref[...] = t.astype(jnp.bfloat16)
        ref4, ref16 = strided_refs
        d4, d16 = ref4.shape[1], ref16.shape[1]
        assert d16 == d4 * d4, "the second copy is a stride-d4 pass over the first"
        for g in range(n_slabs):
            cols = slice(g * LANES, (g + 1) * LANES)
            slab_scr[a * n_slabs + g] = t[:, cols]
            for r in range(d4):
                part = slab_scr[a * n_slabs + g, pl.ds(r, tm // d4, stride=d4), :]
                ref4[0, r, :, cols] = part.astype(jnp.bfloat16)
                part_scr[a * n_slabs + g, r] = part
            for r in range(d16):
                part = part_scr[a * n_slabs + g, r % d4, pl.ds(r // d4, tm // d16, stride=d4), :]
                ref16[0, r, :, cols] = part.astype(jnp.bfloat16)
    proj = in_proj(CQ_OFF, IN_COLS_PAD)

    cq = _rms(proj[:, :Q_LORA_RANK], qg_ref[...]).astype(jnp.bfloat16)
    q2 = jnp.dot(cq, wq_ref[...], preferred_element_type=jnp.float32)
    qm_ref[...] = (q2 * jnp.tile(tq_ref[...], (1, N_HEADS_B))).astype(jnp.bfloat16)

    ckv = _rms(proj[:, Q_LORA_RANK:Q_LORA_RANK + KV_LORA_RANK], kvg_ref[...])
    hi = proj[:, Q_LORA_RANK + KV_LORA_RANK:]
    lane = lax.broadcasted_iota(jnp.int32, hi.shape, 1)
    hi = jnp.where(lane == LANES - 1, 1.0, hi)
    lat = jnp.concatenate([ckv, hi], axis=-1).astype(jnp.bfloat16)
    k2 = jnp.dot(lat, wk_ref[...], preferred_element_type=jnp.float32)
    ck_t = jnp.tile(ck_ref[...], (1, N_HEADS_B))
    sk_t = jnp.tile(sk_ref[...], (1, N_HEADS_B))
    km_ref[...] = (k2[:, :QK_WIDTH_B] * ck_t + k2[:, QK_WIDTH_B:] * sk_t).astype(jnp.bfloat16)
    vt = lax.dot_general(wvt_ref[...], lat, (((1,), (1,)), ((), ())),
                         preferred_element_type=jnp.float32)
    vt_ref[0, 0] = vt.astype(jnp.bfloat16)


def _proj_call(x2, g, win, qg, wq, kvg, wk, wvt, tq, ck, sk, batch, seq):
    tokens = x2.shape[0]
    tm = PROJ_TM
    sblk = seq // tm
    row = lambda i: (i, 0)
    const = lambda i: (0, 0)
    pos = lambda i: (i % sblk, 0)

    def wspec(shape):
        return pl.BlockSpec(shape, const, pipeline_mode=pl.Buffered(1))

    strides = [d for (_, d) in DILATED_PATTERNS if d > 1]

    def strided_spec(d):
        return pl.BlockSpec((1, d, tm // d, WIDTH_A), lambda i: (i // sblk, 0, i % sblk, 0))

    bf = jnp.bfloat16
    return pl.pallas_call(
        _proj_kernel,
        grid=(tokens // tm,),
        in_specs=[
            pl.BlockSpec((tm, D_MODEL), row),
            wspec((1, D_MODEL)),
            wspec((D_MODEL, IN_COLS_PAD)),
            wspec((1, Q_LORA_RANK)),
            wspec((Q_LORA_RANK, QK_WIDTH_B)),
            wspec((1, KV_LORA_RANK)),
            wspec((LAT_WIDTH, 2 * QK_WIDTH_B)),
            wspec((N_HEADS_B * VT_ROWS, LAT_WIDTH)),
            pl.BlockSpec((tm, HEAD_SLOT), pos),
            pl.BlockSpec((tm, HEAD_SLOT), pos),
            pl.BlockSpec((tm, HEAD_SLOT), pos),
        ],
        out_specs=[
            pl.BlockSpec((tm, WIDTH_A), row),
            pl.BlockSpec((tm, WIDTH_A), row),
            pl.BlockSpec((tm, WIDTH_A), row),
            *[strided_spec(d) for d in strides for _ in range(3)],
            pl.BlockSpec((tm, QK_WIDTH_B), row),
            pl.BlockSpec((tm, QK_WIDTH_B), row),
            pl.BlockSpec((1, 1, N_HEADS_B * VT_ROWS, tm), lambda i: (i // sblk, i % sblk, 0, 0)),
        ],
        out_shape=[
            jax.ShapeDtypeStruct((tokens, WIDTH_A), bf),
            jax.ShapeDtypeStruct((tokens, WIDTH_A), bf),
            jax.ShapeDtypeStruct((tokens, WIDTH_A), bf),
            *[jax.ShapeDtypeStruct((batch, d, seq // d, WIDTH_A), bf) for d in strides for _ in range(3)],
            jax.ShapeDtypeStruct((tokens, QK_WIDTH_B), bf),
            jax.ShapeDtypeStruct((tokens, QK_WIDTH_B), bf),
            jax.ShapeDtypeStruct((batch, sblk, N_HEADS_B * VT_ROWS, tm), bf),
        ],
        scratch_shapes=[
            pltpu.VMEM((3 * WIDTH_A // LANES, tm, LANES), jnp.float32),
            pltpu.VMEM((3 * WIDTH_A // LANES, strides[0], tm // strides[0], LANES), jnp.float32),
        ],
        compiler_params=pltpu.CompilerParams(
            dimension_semantics=("arbitrary",), vmem_limit_bytes=VMEM_LIMIT_BYTES),
        name="proj",
    )(x2, g, win, qg, wq, kvg, wk, wvt, tq, ck, sk)


def _dilated_kernel(*refs, dilation, final, q_blocks, n_buf):
    if final:
        q_ref, k_ref, v_ref, bias_ref, o2_ref, l2_ref, o3_ref, l3_ref, out_ref = refs[:9]
    else:
        q_ref, k_ref, v_ref, bias_ref, o_ref, l_ref = refs[:6]
    scratch = refs[-2 * n_buf:]
    z_bufs, m_bufs = scratch[:n_buf], scratch[n_buf:]
    n_pairs = N_HEADS_A // 2
    assert n_pairs % n_buf == 0 and DIL_LOOKAHEAD < n_buf, "buffer rotation must restart per item"
    assert dilation & (dilation - 1) == 0, "work items are decoded with shifts"
    j = pl.program_id(1)
    sub_len = k_ref.shape[2]
    nblk = sub_len // Q_BLK_A
    n_items = dilation * q_blocks
    lane = lax.broadcasted_iota(jnp.int32, (Q_BLK_A, LANES), 1)
    low_half = lane < HEAD_DIM

    def coords(item):
        if isinstance(item, int):
            return (item // dilation) * Q_BLK_A, item % dilation
        qb = lax.shift_right_logical(item, int(math.log2(dilation)))
        return pl.multiple_of(qb * Q_BLK_A, Q_BLK_A), lax.bitwise_and(item, dilation - 1)

    def window(q0):
        jb = j * q_blocks + q0 // Q_BLK_A
        w0 = pl.multiple_of(jnp.clip(jb * Q_BLK_A - RADIUS, 0, sub_len - K_WIN_A), RADIUS)
        variant = jnp.where(jb == 0, 0, jnp.where(jb == nblk - 1, 2, 1))
        return pl.ds(w0, K_WIN_A), variant

    def scores(item, g):
        q0, r = coords(item)
        win, variant = window(q0)
        cols = slice(g * LANES, (g + 1) * LANES)
        qg = q_ref[0, r, pl.ds(q0, Q_BLK_A), cols]
        zero = jnp.zeros_like(qg)
        q2 = jnp.concatenate([jnp.where(low_half, qg, zero), jnp.where(low_half, zero, qg)], axis=0)
        s = lax.dot_general(q2, k_ref[0, r, win, cols], (((1,), (1,)), ((), ())),
                            preferred_element_type=jnp.float32)
        s = s + bias_ref[variant, 2 * g:2 * g + 2].reshape(2 * Q_BLK_A, K_WIN_A)
        m = jnp.max(jnp.maximum(s[:, :LANES], s[:, LANES:]), axis=-1, keepdims=True)
        z_bufs[g % n_buf][...] = s - m
        m_bufs[g % n_buf][...] = jnp.broadcast_to(m, (2 * Q_BLK_A, LANES))

    def attend(item, g):
        q0, r = coords(item)
        win, _ = window(q0)
        cols = slice(g * LANES, (g + 1) * LANES)
        p = jnp.exp2(z_bufs[g % n_buf][...])
        den = jnp.sum(p[:, :LANES] + p[:, LANES:], axis=-1, keepdims=True)
        pv = jnp.dot(p.astype(jnp.bfloat16), v_ref[0, r, win, cols],
                     preferred_element_type=jnp.float32)
        m = m_bufs[g % n_buf][...]
        den_pair = jnp.where(low_half, den[:Q_BLK_A], den[Q_BLK_A:])
        o_pair = jnp.where(low_half, pv[:Q_BLK_A], pv[Q_BLK_A:]) / den_pair
        l_pair = jnp.where(low_half, m[:Q_BLK_A], m[Q_BLK_A:]) + jnp.log2(den_pair)
        return o_pair, l_pair

    def work_item(item, carry):
        q0, r = coords(item)
        item_next = jnp.minimum(item + 1, n_items - 1)
        if dilation == 1:
            rows = pl.ds(q0, Q_BLK_A)
        else:
            rows = pl.ds(q0 * dilation + r, Q_BLK_A, stride=dilation)
        for g in range(n_pairs):
            ahead = g + DIL_LOOKAHEAD
            if ahead < n_pairs:
                scores(item, ahead)
            else:
                scores(item_next, ahead - n_pairs)
            o_pair, l_pair = attend(item, g)
            if final:
                l2, l3 = l2_ref[0, g, rows, :], l3_ref[0, g, rows, :]
                top = jnp.maximum(jnp.maximum(l_pair, l2), l3)
                w1, w2, w3 = jnp.exp2(l_pair - top), jnp.exp2(l2 - top), jnp.exp2(l3 - top)
                o2, o3 = (pltpu.unpack_elementwise(
                    ref[0, g // 2, rows, :], index=g % 2, packed_dtype=jnp.bfloat16,
                    unpacked_dtype=jnp.float32) for ref in (o2_ref, o3_ref))
                merged = (w1 * o_pair + w2 * o2 + w3 * o3) / (w1 + w2 + w3)
                out_ref[0, rows, g * LANES:(g + 1) * LANES] = merged.astype(out_ref.dtype)
            else:
                l_ref[0, g, rows, :] = l_pair
                if g % 2 == 0:
                    o_even = o_pair
                else:
                    o_ref[0, g // 2, rows, :] = pltpu.pack_elementwise(
                        [o_even, o_pair], packed_dtype=jnp.bfloat16)
        return carry

    for g in range(DIL_LOOKAHEAD):
        scores(0, g)
    lax.fori_loop(0, n_items, work_item, 0)


def _dilated_call(q, k, v, bias, q_blocks=1, prev=None):
    batch, dilation, sub_len, _ = q.shape
    seq = sub_len * dilation
    final = prev is not None
    tokens = Q_BLK_A * dilation * q_blocks
    n_pairs = N_HEADS_A // 2
    whole = pl.BlockSpec((1, dilation, sub_len, WIDTH_A), lambda b, j: (b, 0, 0, 0))
    in_specs = [
        pl.BlockSpec((1, dilation, q_blocks * Q_BLK_A, WIDTH_A), lambda b, j: (b, 0, j, 0)),
        whole,
        whole,
        pl.BlockSpec((3, N_HEADS_A, Q_BLK_A, K_WIN_A), lambda b, j: (0, 0, 0, 0),
                     pipeline_mode=pl.Buffered(1)),
    ]
    args = [q, k, v, bias]
    def slab_spec(n):
        return pl.BlockSpec((1, n, tokens, LANES), lambda b, j: (b, 0, j, 0))

    slab_specs = [slab_spec(n_pairs // 2), slab_spec(n_pairs)]
    if final:
        in_specs += slab_specs * 2
        args += list(prev)
        out_specs = pl.BlockSpec((1, tokens, WIDTH_A), lambda b, j: (b, j, 0))
        out_shape = jax.ShapeDtypeStruct((batch, seq, WIDTH_A), jnp.bfloat16)
    else:
        out_specs = slab_specs
        out_shape = [jax.ShapeDtypeStruct((batch, n_pairs // 2, seq, LANES), jnp.uint32),
                     jax.ShapeDtypeStruct((batch, n_pairs, seq, LANES), jnp.float32)]
    return pl.pallas_call(
        functools.partial(_dilated_kernel, dilation=dilation, final=final, q_blocks=q_blocks,
                          n_buf=DIL_BUFFERS),
        grid=(batch, sub_len // (q_blocks * Q_BLK_A)),
        in_specs=in_specs,
        out_specs=out_specs,
        out_shape=out_shape,
        scratch_shapes=(
            [pltpu.VMEM((2 * Q_BLK_A, K_WIN_A), jnp.float32)] * DIL_BUFFERS
            + [pltpu.VMEM((2 * Q_BLK_A, LANES), jnp.float32)] * DIL_BUFFERS),
        compiler_params=pltpu.CompilerParams(
            dimension_semantics=("arbitrary", "arbitrary"),
            vmem_limit_bytes=VMEM_LIMIT_BYTES),
        name=f"dilated_d{dilation}",
    )(*args)


def _t5_buckets(rel):
    nb = N_BUCKETS // 2
    max_exact = nb // 2
    ret = (rel > 0).astype(np.int32) * nb
    n = np.abs(rel)
    large = max_exact + (np.log(np.maximum(n, 1) / max_exact)
                         / np.log(MAX_DISTANCE / max_exact) * (nb - max_exact)).astype(np.int32)
    large = np.minimum(large, nb - 1)
    return (ret + np.where(n < max_exact, n, large)).astype(np.int32)


def _dilated_biases(rel_bias):
    r = np.arange(Q_BLK_A)[:, None]
    c = np.arange(K_WIN_A)[None, :]
    rel = np.stack([c - shift - r for shift in (0, RADIUS, 2 * RADIUS)])
    valid = np.abs(rel) <= RADIUS
    buckets = jnp.asarray(np.stack([_t5_buckets(rel * d) for (_, d) in DILATED_PATTERNS]))
    onehot = (buckets[None] == jnp.arange(N_BUCKETS)[:, None, None, None, None]).astype(jnp.float32)
    b = jnp.einsum('nh,npvrc->pvhrc', rel_bias.astype(jnp.float32), onehot,
                   precision=lax.Precision.HIGHEST)
    tables = jnp.where(jnp.asarray(valid)[None, :, None], b * math.log2(math.e), NEG_INF)
    return [tables[p] for p in range(len(DILATED_PATTERNS))]


def _mla_kernel(q_ref, k_ref, vt_ref, o_ref, s_even, s_odd, p_even, p_odd, m_even, m_odd, acc_scr):
    seq = k_ref.shape[1]
    nk = seq // MLA_TK
    nq = seq // MLA_TQ
    heads = range(2)
    s_bufs, p_bufs, m_bufs = (s_even, s_odd), (p_even, p_odd), (m_even, m_odd)
    assert nq % 2 == 0 and nq >= 4

    def q_rows(t):
        start = t * MLA_TQ
        return pl.ds(start if isinstance(t, int) else pl.multiple_of(start, MLA_TQ), MLA_TQ)

    def keys(c):
        return slice(c * MLA_TK, (c + 1) * MLA_TK)

    def stage(score_t=None, exp_t=None, pv_t=None):
        run_max = [None, None]
        for c in range(nk):
            if score_t is not None:
                t, par = score_t
                for hh in heads:
                    lanes = slice(hh * HEAD_SLOT, (hh + 1) * HEAD_SLOT)
                    s = lax.dot_general(k_ref[0, keys(c), lanes], q_ref[0, q_rows(t), lanes],
                                        (((1,), (1,)), ((), ())),
                                        preferred_element_type=jnp.float32)
                    s_bufs[par][hh, keys(c), :] = s
                    mc = jnp.max(s, axis=0, keepdims=True)
                    run_max[hh] = mc if c == 0 else jnp.maximum(run_max[hh], mc)
            if exp_t is not None:
                _, par = exp_t
                for hh in heads:
                    z = s_bufs[par][hh, keys(c), :] - m_bufs[par][hh]
                    p_bufs[par][hh, keys(c), :] = jnp.exp2(z).astype(jnp.bfloat16)
            if pv_t is not None:
                _, par = pv_t
                blk, off = divmod(c * MLA_TK, vt_ref.shape[3])
                for hh in heads:
                    pv = jnp.dot(vt_ref[0, blk, hh * VT_ROWS:(hh + 1) * VT_ROWS, off:off + MLA_TK],
                                 p_bufs[par][hh, keys(c), :],
                                 preferred_element_type=jnp.float32)
                    acc_scr[hh] = pv if c == 0 else acc_scr[hh] + pv
        if score_t is not None:
            for hh in heads:
                m_bufs[score_t[1]][hh] = run_max[hh]
        if pv_t is not None:
            outs = [acc_scr[hh, :V_HEAD_DIM] / acc_scr[hh, VT_ONES_ROW:VT_ONES_ROW + 1]
                    for hh in heads]
            o_ref[0, q_rows(pv_t[0]), :] = jnp.concatenate(outs, axis=0).T.astype(o_ref.dtype)

    stage(score_t=(0, 0))
    stage(score_t=(1, 1), exp_t=(0, 0))

    def stage_pair(i, carry):
        t = 2 * i + 1
        stage(score_t=(t + 1, 0), exp_t=(t, 1), pv_t=(t - 1, 0))
        stage(score_t=(t + 2, 1), exp_t=(t + 1, 0), pv_t=(t, 1))
        return carry

    lax.fori_loop(0, nq // 2 - 1, stage_pair, 0)
    stage(exp_t=(nq - 1, 1), pv_t=(nq - 2, 0))
    stage(pv_t=(nq - 1, 1))


def _mla_call(qm, km, vt, seq):
    batch, n_blk, _, blk_keys = vt.shape
    assert blk_keys % MLA_TK == 0, "a key chunk must not straddle two V^T blocks"
    pair = lambda b, g: (b, 0, g)
    return pl.pallas_call(
        _mla_kernel,
        grid=(batch, N_HEADS_B // 2),
        in_specs=[
            pl.BlockSpec((1, seq, 2 * HEAD_SLOT), pair),
            pl.BlockSpec((1, seq, 2 * HEAD_SLOT), pair),
            pl.BlockSpec((1, n_blk, 2 * VT_ROWS, blk_keys), lambda b, g: (b, 0, g, 0)),
        ],
        out_specs=pl.BlockSpec((1, seq, 2 * V_HEAD_DIM), pair),
        out_shape=jax.ShapeDtypeStruct((batch, seq, WIDTH_B), jnp.bfloat16),
        scratch_shapes=(
            [pltpu.VMEM((2, seq, MLA_TQ), jnp.float32)] * 2
            + [pltpu.VMEM((2, seq, MLA_TQ), jnp.bfloat16)] * 2
            + [pltpu.VMEM((2, 1, MLA_TQ), jnp.float32)] * 2
            + [pltpu.VMEM((2, VT_ROWS, MLA_TQ), jnp.float32)]),
        compiler_params=pltpu.CompilerParams(
            dimension_semantics=("arbitrary", "arbitrary"),
            vmem_limit_bytes=VMEM_LIMIT_BYTES),
        name="mla",
    )(qm, km, vt)


def _mlp_kernel(x_ref, oa_ref, ob_ref, wout_ref, g_ref, wup_ref, wdown_ref, gf_ref, o_ref):
    o_cat = jnp.concatenate([oa_ref[...], ob_ref[...]], axis=-1)
    h = x_ref[...] + jnp.dot(o_cat, wout_ref[...], preferred_element_type=jnp.float32)
    u = _rms(h, g_ref[...]).astype(jnp.bfloat16)
    acc = h
    for c in range(D_FF // FF_CHUNK):
        a = jnp.dot(u, wup_ref[:, c * FF_CHUNK:(c + 1) * FF_CHUNK],
                    preferred_element_type=jnp.float32)
        a = jnp.square(jnp.maximum(a, 0.0)).astype(jnp.bfloat16)
        acc = acc + jnp.dot(a, wdown_ref[c * FF_CHUNK:(c + 1) * FF_CHUNK, :],
                            preferred_element_type=jnp.float32)
    o_ref[...] = _rms(acc, gf_ref[...])


def _mlp_call(x2, oa, ob, wout, g, wup, wdown, gf):
    tokens = x2.shape[0]
    tm = MLP_TM
    row = lambda i: (i, 0)
    const = lambda i: (0, 0)

    def wspec(shape):
        return pl.BlockSpec(shape, const, pipeline_mode=pl.Buffered(1))

    return pl.pallas_call(
        _mlp_kernel,
        grid=(tokens // tm,),
        in_specs=[
            pl.BlockSpec((tm, D_MODEL), row),
            pl.BlockSpec((tm, WIDTH_A), row),
            pl.BlockSpec((tm, WIDTH_B), row),
            wspec((WIDTH_A + WIDTH_B, D_MODEL)),
            wspec((1, D_MODEL)),
            wspec((D_MODEL, D_FF)),
            wspec((D_FF, D_MODEL)),
            wspec((1, D_MODEL)),
        ],
        out_specs=pl.BlockSpec((tm, D_MODEL), row),
        out_shape=jax.ShapeDtypeStruct((tokens, D_MODEL), jnp.float32),
        compiler_params=pltpu.CompilerParams(
            dimension_semantics=("arbitrary",), vmem_limit_bytes=VMEM_LIMIT_BYTES),
        name="mlp",
    )(x2, oa, ob, wout, g, wup, wdown, gf)


def _prep_in_weights(w_in):
    pad = jnp.zeros((D_MODEL, IN_COLS_PAD - w_in.shape[1]), w_in.dtype)
    return jnp.concatenate([w_in, pad], axis=1).astype(jnp.bfloat16)


def _prep_q_weights(w_q_b):
    dqk = QK_NOPE_DIM + QK_ROPE_DIM
    half = QK_ROPE_DIM // 2
    w = w_q_b.reshape(Q_LORA_RANK, N_HEADS_B, dqk)
    nope, rope = w[..., :QK_NOPE_DIM], w[..., QK_NOPE_DIM:]
    slot = jnp.concatenate([nope, rope, -rope[..., half:], rope[..., :half]], axis=-1)
    return slot.reshape(Q_LORA_RANK, QK_WIDTH_B).astype(jnp.bfloat16)


def _prep_kv_weights(w_kv_b):
    half = QK_ROPE_DIM // 2
    w = w_kv_b.reshape(KV_LORA_RANK, N_HEADS_B, QK_NOPE_DIM + V_HEAD_DIM)
    k_nope, v = w[..., :QK_NOPE_DIM], w[..., QK_NOPE_DIM:]
    eye = np.eye(QK_ROPE_DIM, dtype=np.float32)
    rot = np.zeros((QK_ROPE_DIM, QK_ROPE_DIM), np.float32)
    for jcol in range(half):
        rot[half + jcol, jcol] = -1.0
        rot[jcol, half + jcol] = 1.0

    def place(top, rope_block):
        blk = np.zeros((LAT_WIDTH - KV_LORA_RANK, N_HEADS_B, HEAD_SLOT), np.float32)
        for lo in (QK_NOPE_DIM, QK_NOPE_DIM + QK_ROPE_DIM):
            blk[:QK_ROPE_DIM, :, lo:lo + QK_ROPE_DIM] = rope_block[:, None, :]
        top = jnp.concatenate(
            [top, jnp.zeros((KV_LORA_RANK, N_HEADS_B, HEAD_SLOT - QK_NOPE_DIM), top.dtype)], axis=-1)
        full = jnp.concatenate([top, jnp.asarray(blk)], axis=0)
        return full.reshape(LAT_WIDTH, QK_WIDTH_B)

    wk = jnp.concatenate([place(k_nope, eye), place(jnp.zeros_like(k_nope), rot)], axis=1)

    vt_top = jnp.transpose(v, (1, 2, 0))
    vt_top = jnp.concatenate(
        [vt_top, jnp.zeros((N_HEADS_B, VT_ROWS - V_HEAD_DIM, KV_LORA_RANK), v.dtype)], axis=1)
    ones_sel = np.zeros((N_HEADS_B, VT_ROWS, LAT_WIDTH - KV_LORA_RANK), np.float32)
    ones_sel[:, VT_ONES_ROW, -1] = 1.0
    wvt = jnp.concatenate([vt_top, jnp.asarray(ones_sel)], axis=2)
    return wk.astype(jnp.bfloat16), wvt.reshape(N_HEADS_B * VT_ROWS, LAT_WIDTH).astype(jnp.bfloat16)


def _rope_lane_tables(seq):
    inv_freq = ROPE_THETA ** (-np.arange(0, QK_ROPE_DIM, 2, dtype=np.float64) / QK_ROPE_DIM)
    freqs = np.arange(seq, dtype=np.float64)[:, None] * inv_freq[None, :]
    cos, sin = np.cos(freqs), np.sin(freqs)
    cos2, sin2 = np.concatenate([cos, cos], axis=1), np.concatenate([sin, sin], axis=1)
    ones = np.ones((seq, QK_NOPE_DIM))
    q_scale = (QK_NOPE_DIM + QK_ROPE_DIM) ** -0.5 * math.log2(math.e)
    q_table = np.concatenate([ones, cos2, sin2], axis=1) * q_scale
    k_cos = np.concatenate([ones, cos2, cos2], axis=1)
    k_sin = np.concatenate([np.zeros_like(ones), sin2, sin2], axis=1)
    return tuple(jnp.asarray(t, jnp.float32) for t in (q_table, k_cos, k_sin))


def kernel(x, mix_norm_g, w_in, q_norm_g, w_q_b, kv_norm_g, w_kv_b, w_out,
           mlp_norm_g, w_up, w_down, rel_bias, final_norm_g):
    batch, seq, _ = x.shape
    depth = w_in.shape[0]
    assert depth == 1, "the final norm is fused into the single layer's MLP kernel"
    tq, ck, sk = _rope_lane_tables(seq)
    biases = _dilated_biases(rel_bias)
    layer = 0
    x2 = x.reshape(batch * seq, D_MODEL)
    wk, wvt = _prep_kv_weights(w_kv_b[layer])
    qa, ka, va, qa4, ka4, va4, qa16, ka16, va16, qm, km, vt = _proj_call(
        x2, mix_norm_g[layer][None], _prep_in_weights(w_in[layer]),
        q_norm_g[layer][None], _prep_q_weights(w_q_b[layer]),
        kv_norm_g[layer][None], wk, wvt, tq, ck, sk, batch, seq)

    shape_a = (batch, 1, seq, WIDTH_A)
    q_blocks = dict(zip([d for (_, d) in DILATED_PATTERNS], DIL_Q_BLOCKS))
    o16, l16 = _dilated_call(qa16, ka16, va16, biases[2], q_blocks=q_blocks[16])
    o4, l4 = _dilated_call(qa4, ka4, va4, biases[1], q_blocks=q_blocks[4])
    oa = _dilated_call(qa.reshape(shape_a), ka.reshape(shape_a), va.reshape(shape_a), biases[0],
                       q_blocks=q_blocks[1], prev=(o4, l4, o16, l16))

    ob = _mla_call(qm.reshape(batch, seq, QK_WIDTH_B), km.reshape(batch, seq, QK_WIDTH_B), vt, seq)

    out = _mlp_call(
        x2, oa.reshape(batch * seq, WIDTH_A), ob.reshape(batch * seq, WIDTH_B),
        w_out[layer].astype(jnp.bfloat16), mlp_norm_g[layer][None],
        w_up[layer].astype(jnp.bfloat16), w_down[layer].astype(jnp.bfloat16),
        final_norm_g[None])
    return out.reshape(batch, seq, D_MODEL)
```

```python
import functools
import math

import jax
import jax.numpy as jnp
import numpy as np
from jax import lax
from jax.experimental import pallas as pl
from jax.experimental.pallas import tpu as pltpu

D_MODEL = 1024
HEAD_DIM = 64
N_HEADS_A = 8
DILATED_PATTERNS = ((128, 1), (512, 4), (2048, 16))
N_HEADS_B = 8
Q_LORA_RANK = 256
KV_LORA_RANK = 128
QK_NOPE_DIM = 64
QK_ROPE_DIM = 32
V_HEAD_DIM = 64
ROPE_THETA = 10000.0
N_BUCKETS = 32
MAX_DISTANCE = 1024
D_FF = 4 * D_MODEL
NORM_EPS = 1e-6
NEG_INF = -1e30
WIDTH_A = N_HEADS_A * HEAD_DIM
WIDTH_B = N_HEADS_B * V_HEAD_DIM

LANES = 128
VMEM_LIMIT_BYTES = 56 * 1024 * 1024

HEAD_SLOT = LANES
QK_WIDTH_B = N_HEADS_B * HEAD_SLOT
VT_ROWS = 80
VT_ONES_ROW = V_HEAD_DIM
IN_COLS_PAD = 2048
CQ_OFF = 3 * WIDTH_A
LAT_WIDTH = 2 * LANES
RADIUS = 64
Q_BLK_A = 2 * RADIUS
K_WIN_A = 4 * RADIUS

PROJ_TM = 512
MLP_TM = 1024
MLA_TQ = 256
DIL_LOOKAHEAD = 3
DIL_BUFFERS = 4
DIL_Q_BLOCKS = (8, 4, 1)
MLA_TK = 512
FF_CHUNK = 1024


def _rms(xf, g):
    return xf * lax.rsqrt(jnp.mean(xf * xf, axis=-1, keepdims=True) + NORM_EPS) * g


def _proj_kernel(x_ref, g_ref, win_ref, qg_ref, wq_ref, kvg_ref, wk_ref, wvt_ref,
                 tq_ref, ck_ref, sk_ref,
                 qa_ref, ka_ref, va_ref, qa4_ref, ka4_ref, va4_ref, qa16_ref, ka16_ref, va16_ref,
                 qm_ref, km_ref, vt_ref, slab_scr, part_scr):
    x = x_ref[...]
    u = _rms(x, g_ref[...]).astype(jnp.bfloat16)

    def in_proj(lo, hi):
        return jnp.dot(u, win_ref[:, lo:hi], preferred_element_type=jnp.float32)

    tm = x.shape[0]
    n_slabs = WIDTH_A // LANES
    scales = (HEAD_DIM ** -0.5 * math.log2(math.e), 1.0, 1.0)
    groups = ((qa_ref, qa4_ref, qa16_ref), (ka_ref, ka4_ref, ka16_ref), (va_ref, va4_ref, va16_ref))
    for a, (nat_ref, *strided_refs) in enumerate(groups):
        t = in_proj(a * WIDTH_A, (a + 1) * WIDTH_A) * scales[a]
        nat_ref[...] = t.astype(jnp.bfloat16)
        ref4, ref16 = strided_refs
        d4, d16 = ref4.shape[1], ref16.shape[1]
        assert d16 == d4 * d4, "the second copy is a stride-d4 pass over the first"
        for g in range(n_slabs):
            cols = slice(g * LANES, (g + 1) * LANES)
            slab_scr[a * n_slabs + g] = t[:, cols]
            for r in range(d4):
                part = slab_scr[a * n_slabs + g, pl.ds(r, tm // d4, stride=d4), :]
                ref4[0, r, :, cols] = part.astype(jnp.bfloat16)
                part_scr[a * n_slabs + g, r] = part
            for r in range(d16):
                part = part_scr[a * n_slabs + g, r % d4, pl.ds(r // d4, tm // d16, stride=d4), :]
                ref16[0, r, :, cols] = part.astype(jnp.bfloat16)
    proj = in_proj(CQ_OFF, IN_COLS_PAD)

    cq = _rms(proj[:, :Q_LORA_RANK], qg_ref[...]).astype(jnp.bfloat16)
    q2 = jnp.dot(cq, wq_ref[...], preferred_element_type=jnp.float32)
    qm_ref[...] = (q2 * jnp.tile(tq_ref[...], (1, N_HEADS_B))).astype(jnp.bfloat16)

    ckv = _rms(proj[:, Q_LORA_RANK:Q_LORA_RANK + KV_LORA_RANK], kvg_ref[...])
    hi = proj[:, Q_LORA_RANK + KV_LORA_RANK:]
    lane = lax.broadcasted_iota(jnp.int32, hi.shape, 1)
    hi = jnp.where(lane == LANES - 1, 1.0, hi)
    lat = jnp.concatenate([ckv, hi], axis=-1).astype(jnp.bfloat16)
    k2 = jnp.dot(lat, wk_ref[...], preferred_element_type=jnp.float32)
    ck_t = jnp.tile(ck_ref[...], (1, N_HEADS_B))
    sk_t = jnp.tile(sk_ref[...], (1, N_HEADS_B))
    km_ref[...] = (k2[:, :QK_WIDTH_B] * ck_t + k2[:, QK_WIDTH_B:] * sk_t).astype(jnp.bfloat16)
    vt = lax.dot_general(wvt_ref[...], lat, (((1,), (1,)), ((), ())),
                         preferred_element_type=jnp.float32)
    vt_ref[0, 0] = vt.astype(jnp.bfloat16)


def _proj_call(x2, g, win, qg, wq, kvg, wk, wvt, tq, ck, sk, batch, seq):
    tokens = x2.shape[0]
    tm = PROJ_TM
    sblk = seq // tm
    row = lambda i: (i, 0)
    const = lambda i: (0, 0)
    pos = lambda i: (i % sblk, 0)

    def wspec(shape):
        return pl.BlockSpec(shape, const, pipeline_mode=pl.Buffered(1))

    strides = [d for (_, d) in DILATED_PATTERNS if d > 1]

    def strided_spec(d):
        return pl.BlockSpec((1, d, tm // d, WIDTH_A), lambda i: (i // sblk, 0, i % sblk, 0))

    bf = jnp.bfloat16
    return pl.pallas_call(
        _proj_kernel,
        grid=(tokens // tm,),
        in_specs=[
            pl.BlockSpec((tm, D_MODEL), row),
            wspec((1, D_MODEL)),
            wspec((D_MODEL, IN_COLS_PAD)),
            wspec((1, Q_LORA_RANK)),
            wspec((Q_LORA_RANK, QK_WIDTH_B)),
            wspec((1, KV_LORA_RANK)),
            wspec((LAT_WIDTH, 2 * QK_WIDTH_B)),
            wspec((N_HEADS_B * VT_ROWS, LAT_WIDTH)),
            pl.BlockSpec((tm, HEAD_SLOT), pos),
            pl.BlockSpec((tm, HEAD_SLOT), pos),
            pl.BlockSpec((tm, HEAD_SLOT), pos),
        ],
        out_specs=[
            pl.BlockSpec((tm, WIDTH_A), row),
            pl.BlockSpec((tm, WIDTH_A), row),
            pl.BlockSpec((tm, WIDTH_A), row),
            *[strided_spec(d) for d in strides for _ in range(3)],
            pl.BlockSpec((tm, QK_WIDTH_B), row),
            pl.BlockSpec((tm, QK_WIDTH_B), row),
            pl.BlockSpec((1, 1, N_HEADS_B * VT_ROWS, tm), lambda i: (i // sblk, i % sblk, 0, 0)),
        ],
        out_shape=[
            jax.ShapeDtypeStruct((tokens, WIDTH_A), bf),
            jax.ShapeDtypeStruct((tokens, WIDTH_A), bf),
            jax.ShapeDtypeStruct((tokens, WIDTH_A), bf),
            *[jax.ShapeDtypeStruct((batch, d, seq // d, WIDTH_A), bf) for d in strides for _ in range(3)],
            jax.ShapeDtypeStruct((tokens, QK_WIDTH_B), bf),
            jax.ShapeDtypeStruct((tokens, QK_WIDTH_B), bf),
            jax.ShapeDtypeStruct((batch, sblk, N_HEADS_B * VT_ROWS, tm), bf),
        ],
        scratch_shapes=[
            pltpu.VMEM((3 * WIDTH_A // LANES, tm, LANES), jnp.float32),
            pltpu.VMEM((3 * WIDTH_A // LANES, strides[0], tm // strides[0], LANES), jnp.float32),
        ],
        compiler_params=pltpu.CompilerParams(
            dimension_semantics=("arbitrary",), vmem_limit_bytes=VMEM_LIMIT_BYTES),
        name="proj",
    )(x2, g, win, qg, wq, kvg, wk, wvt, tq, ck, sk)


def _dilated_kernel(*refs, dilation, final, q_blocks, n_buf):
    if final:
        q_ref, k_ref, v_ref, bias_ref, o2_ref, l2_ref, o3_ref, l3_ref, out_ref = refs[:9]
    else:
        q_ref, k_ref, v_ref, bias_ref, o_ref, l_ref = refs[:6]
    scratch = refs[-2 * n_buf:]
    z_bufs, m_bufs = scratch[:n_buf], scratch[n_buf:]
    n_pairs = N_HEADS_A // 2
    assert n_pairs % n_buf == 0 and DIL_LOOKAHEAD < n_buf, "buffer rotation must restart per item"
    assert dilation & (dilation - 1) == 0, "work items are decoded with shifts"
    j = pl.program_id(1)
    sub_len = k_ref.shape[2]
    nblk = sub_len // Q_BLK_A
    n_items = dilation * q_blocks
    lane = lax.broadcasted_iota(jnp.int32, (Q_BLK_A, LANES), 1)
    low_half = lane < HEAD_DIM

    def coords(item):
        if isinstance(item, int):
            return (item // dilation) * Q_BLK_A, item % dilation
        qb = lax.shift_right_logical(item, int(math.log2(dilation)))
        return pl.multiple_of(qb * Q_BLK_A, Q_BLK_A), lax.bitwise_and(item, dilation - 1)

    def window(q0):
        jb = j * q_blocks + q0 // Q_BLK_A
        w0 = pl.multiple_of(jnp.clip(jb * Q_BLK_A - RADIUS, 0, sub_len - K_WIN_A), RADIUS)
        variant = jnp.where(jb == 0, 0, jnp.where(jb == nblk - 1, 2, 1))
        return pl.ds(w0, K_WIN_A), variant

    def scores(item, g):
        q0, r = coords(item)
        win, variant = window(q0)
        cols = slice(g * LANES, (g + 1) * LANES)
        qg = q_ref[0, r, pl.ds(q0, Q_BLK_A), cols]
        zero = jnp.zeros_like(qg)
        q2 = jnp.concatenate([jnp.where(low_half, qg, zero), jnp.where(low_half, zero, qg)], axis=0)
        s = lax.dot_general(q2, k_ref[0, r, win, cols], (((1,), (1,)), ((), ())),
                            preferred_element_type=jnp.float32)
        s = s + bias_ref[variant, 2 * g:2 * g + 2].reshape(2 * Q_BLK_A, K_WIN_A)
        m = jnp.max(jnp.maximum(s[:, :LANES], s[:, LANES:]), axis=-1, keepdims=True)
        z_bufs[g % n_buf][...] = s - m
        m_bufs[g % n_buf][...] = jnp.broadcast_to(m, (2 * Q_BLK_A, LANES))

    def attend(item, g):
        q0, r = coords(item)
        win, _ = window(q0)
        cols = slice(g * LANES, (g + 1) * LANES)
        p = jnp.exp2(z_bufs[g % n_buf][...])
        den = jnp.sum(p[:, :LANES] + p[:, LANES:], axis=-1, keepdims=True)
        pv = jnp.dot(p.astype(jnp.bfloat16), v_ref[0, r, win, cols],
                     preferred_element_type=jnp.float32)
        m = m_bufs[g % n_buf][...]
        den_pair = jnp.where(low_half, den[:Q_BLK_A], den[Q_BLK_A:])
        o_pair = jnp.where(low_half, pv[:Q_BLK_A], pv[Q_BLK_A:]) / den_pair
        l_pair = jnp.where(low_half, m[:Q_BLK_A], m[Q_BLK_A:]) + jnp.log2(den_pair)
        return o_pair, l_pair

    def work_item(item, carry):
        q0, r = coords(item)
        item_next = jnp.minimum(item + 1, n_items - 1)
        if dilation == 1:
            rows = pl.ds(q0, Q_BLK_A)
        else:
            rows = pl.ds(q0 * dilation + r, Q_BLK_A, stride=dilation)
        for g in range(n_pairs):
            ahead = g + DIL_LOOKAHEAD
            if ahead < n_pairs:
                scores(item, ahead)
            else:
                scores(item_next, ahead - n_pairs)
            o_pair, l_pair = attend(item, g)
            if final:
                l2, l3 = l2_ref[0, g, rows, :], l3_ref[0, g, rows, :]
                top = jnp.maximum(jnp.maximum(l_pair, l2), l3)
                w1, w2, w3 = jnp.exp2(l_pair - top), jnp.exp2(l2 - top), jnp.exp2(l3 - top)
                o2, o3 = (pltpu.unpack_elementwise(
                    ref[0, g // 2, rows, :], index=g % 2, packed_dtype=jnp.bfloat16,
                    unpacked_dtype=jnp.float32) for ref in (o2_ref, o3_ref))
                merged = (w1 * o_pair + w2 * o2 + w3 * o3) / (w1 + w2 + w3)
                out_ref[0, rows, g * LANES:(g + 1) * LANES] = merged.astype(out_ref.dtype)
            else:
                l_ref[0, g, rows, :] = l_pair
                if g % 2 == 0:
                    o_even = o_pair
                else:
                    o_ref[0, g // 2, rows, :] = pltpu.pack_elementwise(
                        [o_even, o_pair], packed_dtype=jnp.bfloat16)
        return carry

    for g in range(DIL_LOOKAHEAD):
        scores(0, g)
    lax.fori_loop(0, n_items, work_item, 0)


def _dilated_call(q, k, v, bias, q_blocks=1, prev=None):
    batch, dilation, sub_len, _ = q.shape
    seq = sub_len * dilation
    final = prev is not None
    tokens = Q_BLK_A * dilation * q_blocks
    n_pairs = N_HEADS_A // 2
    whole = pl.BlockSpec((1, dilation, sub_len, WIDTH_A), lambda b, j: (b, 0, 0, 0))
    in_specs = [
        pl.BlockSpec((1, dilation, q_blocks * Q_BLK_A, WIDTH_A), lambda b, j: (b, 0, j, 0)),
        whole,
        whole,
        pl.BlockSpec((3, N_HEADS_A, Q_BLK_A, K_WIN_A), lambda b, j: (0, 0, 0, 0),
                     pipeline_mode=pl.Buffered(1)),
    ]
    args = [q, k, v, bias]
    def slab_spec(n):
        return pl.BlockSpec((1, n, tokens, LANES), lambda b, j: (b, 0, j, 0))

    slab_specs = [slab_spec(n_pairs // 2), slab_spec(n_pairs)]
    if final:
        in_specs += slab_specs * 2
        args += list(prev)
        out_specs = pl.BlockSpec((1, tokens, WIDTH_A), lambda b, j: (b, j, 0))
        out_shape = jax.ShapeDtypeStruct((batch, seq, WIDTH_A), jnp.bfloat16)
    else:
        out_specs = slab_specs
        out_shape = [jax.ShapeDtypeStruct((batch, n_pairs // 2, seq, LANES), jnp.uint32),
                     jax.ShapeDtypeStruct((batch, n_pairs, seq, LANES), jnp.float32)]
    return pl.pallas_call(
        functools.partial(_dilated_kernel, dilation=dilation, final=final, q_blocks=q_blocks,
                          n_buf=DIL_BUFFERS),
        grid=(batch, sub_len // (q_blocks * Q_BLK_A)),
        in_specs=in_specs,
        out_specs=out_specs,
        out_shape=out_shape,
        scratch_shapes=(
            [pltpu.VMEM((2 * Q_BLK_A, K_WIN_A), jnp.float32)] * DIL_BUFFERS
            + [pltpu.VMEM((2 * Q_BLK_A, LANES), jnp.float32)] * DIL_BUFFERS),
        compiler_params=pltpu.CompilerParams(
            dimension_semantics=("arbitrary", "arbitrary"),
            vmem_limit_bytes=VMEM_LIMIT_BYTES),
        name=f"dilated_d{dilation}",
    )(*args)


def _t5_buckets(rel):
    nb = N_BUCKETS // 2
    max_exact = nb // 2
    ret = (rel > 0).astype(np.int32) * nb
    n = np.abs(rel)
    large = max_exact + (np.log(np.maximum(n, 1) / max_exact)
                         / np.log(MAX_DISTANCE / max_exact) * (nb - max_exact)).astype(np.int32)
    large = np.minimum(large, nb - 1)
    return (ret + np.where(n < max_exact, n, large)).astype(np.int32)


def _dilated_biases(rel_bias):
    r = np.arange(Q_BLK_A)[:, None]
    c = np.arange(K_WIN_A)[None, :]
    rel = np.stack([c - shift - r for shift in (0, RADIUS, 2 * RADIUS)])
    valid = np.abs(rel) <= RADIUS
    buckets = jnp.asarray(np.stack([_t5_buckets(rel * d) for (_, d) in DILATED_PATTERNS]))
    onehot = (buckets[None] == jnp.arange(N_BUCKETS)[:, None, None, None, None]).astype(jnp.float32)
    b = jnp.einsum('nh,npvrc->pvhrc', rel_bias.astype(jnp.float32), onehot,
                   precision=lax.Precision.HIGHEST)
    tables = jnp.where(jnp.asarray(valid)[None, :, None], b * math.log2(math.e), NEG_INF)
    return [tables[p] for p in range(len(DILATED_PATTERNS))]


def _mla_kernel(q_ref, k_ref, vt_ref, o_ref, s_even, s_odd, p_even, p_odd, m_even, m_odd):
    seq = k_ref.shape[1]
    nk = seq // MLA_TK
    nq = seq // MLA_TQ
    heads = range(2)
    s_bufs, p_bufs, m_bufs = (s_even, s_odd), (p_even, p_odd), (m_even, m_odd)
    assert nq % 2 == 0 and nq >= 4

    def q_rows(t):
        start = t * MLA_TQ
        return pl.ds(start if isinstance(t, int) else pl.multiple_of(start, MLA_TQ), MLA_TQ)

    def keys(c):
        return slice(c * MLA_TK, (c + 1) * MLA_TK)

    def stage(score_t=None, exp_t=None, pv_t=None):
        run_max = [None, None]
        for c in range(nk):
            if score_t is not None:
                t, par = score_t
                for hh in heads:
                    lanes = slice(hh * HEAD_SLOT, (hh + 1) * HEAD_SLOT)
                    s = lax.dot_general(k_ref[0, keys(c), lanes], q_ref[0, q_rows(t), lanes],
                                        (((1,), (1,)), ((), ())),
                                        preferred_element_type=jnp.float32)
                    s_bufs[par][hh, keys(c), :] = s
                    mc = jnp.max(s, axis=0, keepdims=True)
                    run_max[hh] = mc if c == 0 else jnp.maximum(run_max[hh], mc)
            if exp_t is not None:
                _, par = exp_t
                for hh in heads:
                    z = s_bufs[par][hh, keys(c), :] - m_bufs[par][hh]
                    p_bufs[par][hh, keys(c), :] = jnp.exp2(z).astype(jnp.bfloat16)
        if score_t is not None:
            for hh in heads:
                m_bufs[score_t[1]][hh] = run_max[hh]
        if pv_t is not None:
            _, par = pv_t
            n_blk = vt_ref.shape[1]
            outs = []
            for hh in heads:
                vt = jnp.concatenate([vt_ref[0, b, hh * VT_ROWS:(hh + 1) * VT_ROWS, :]
                                      for b in range(n_blk)], axis=1)
                acc = jnp.dot(vt, p_bufs[par][hh], preferred_element_type=jnp.float32)
                outs.append(acc[:V_HEAD_DIM] / acc[VT_ONES_ROW:VT_ONES_ROW + 1])
            o_ref[0, q_rows(pv_t[0]), :] = jnp.concatenate(outs, axis=0).T.astype(o_ref.dtype)

    stage(score_t=(0, 0))
    stage(score_t=(1, 1), exp_t=(0, 0))

    def stage_pair(i, carry):
        t = 2 * i + 1
        stage(score_t=(t + 1, 0), exp_t=(t, 1), pv_t=(t - 1, 0))
        stage(score_t=(t + 2, 1), exp_t=(t + 1, 0), pv_t=(t, 1))
        return carry

    lax.fori_loop(0, nq // 2 - 1, stage_pair, 0)
    stage(exp_t=(nq - 1, 1), pv_t=(nq - 2, 0))
    stage(pv_t=(nq - 1, 1))


def _mla_call(qm, km, vt, seq):
    batch, n_blk, _, blk_keys = vt.shape
    assert blk_keys % MLA_TK == 0, "a key chunk must not straddle two V^T blocks"
    pair = lambda b, g: (b, 0, g)
    return pl.pallas_call(
        _mla_kernel,
        grid=(batch, N_HEADS_B // 2),
        in_specs=[
            pl.BlockSpec((1, seq, 2 * HEAD_SLOT), pair),
            pl.BlockSpec((1, seq, 2 * HEAD_SLOT), pair),
            pl.BlockSpec((1, n_blk, 2 * VT_ROWS, blk_keys), lambda b, g: (b, 0, g, 0)),
        ],
        out_specs=pl.BlockSpec((1, seq, 2 * V_HEAD_DIM), pair),
        out_shape=jax.ShapeDtypeStruct((batch, seq, WIDTH_B), jnp.bfloat16),
        scratch_shapes=(
            [pltpu.VMEM((2, seq, MLA_TQ), jnp.float32)] * 2
            + [pltpu.VMEM((2, seq, MLA_TQ), jnp.bfloat16)] * 2
            + [pltpu.VMEM((2, 1, MLA_TQ), jnp.float32)] * 2),
        compiler_params=pltpu.CompilerParams(
            dimension_semantics=("arbitrary", "arbitrary"),
            vmem_limit_bytes=VMEM_LIMIT_BYTES),
        name="mla",
    )(qm, km, vt)


def _mlp_kernel(x_ref, oa_ref, ob_ref, wout_ref, g_ref, wup_ref, wdown_ref, gf_ref, o_ref):
    o_cat = jnp.concatenate([oa_ref[...], ob_ref[...]], axis=-1)
    h = x_ref[...] + jnp.dot(o_cat, wout_ref[...], preferred_element_type=jnp.float32)
    u = _rms(h, g_ref[...]).astype(jnp.bfloat16)
    acc = h
    for c in range(D_FF // FF_CHUNK):
        a = jnp.dot(u, wup_ref[:, c * FF_CHUNK:(c + 1) * FF_CHUNK],
                    preferred_element_type=jnp.float32)
        a = jnp.square(jnp.maximum(a, 0.0)).astype(jnp.bfloat16)
        acc = acc + jnp.dot(a, wdown_ref[c * FF_CHUNK:(c + 1) * FF_CHUNK, :],
                            preferred_element_type=jnp.float32)
    o_ref[...] = _rms(acc, gf_ref[...])


def _mlp_call(x2, oa, ob, wout, g, wup, wdown, gf):
    tokens = x2.shape[0]
    tm = MLP_TM
    row = lambda i: (i, 0)
    const = lambda i: (0, 0)

    def wspec(shape):
        return pl.BlockSpec(shape, const, pipeline_mode=pl.Buffered(1))

    return pl.pallas_call(
        _mlp_kernel,
        grid=(tokens // tm,),
        in_specs=[
            pl.BlockSpec((tm, D_MODEL), row),
            pl.BlockSpec((tm, WIDTH_A), row),
            pl.BlockSpec((tm, WIDTH_B), row),
            wspec((WIDTH_A + WIDTH_B, D_MODEL)),
            wspec((1, D_MODEL)),
            wspec((D_MODEL, D_FF)),
            wspec((D_FF, D_MODEL)),
            wspec((1, D_MODEL)),
        ],
        out_specs=pl.BlockSpec((tm, D_MODEL), row),
        out_shape=jax.ShapeDtypeStruct((tokens, D_MODEL), jnp.float32),
        compiler_params=pltpu.CompilerParams(
            dimension_semantics=("arbitrary",), vmem_limit_bytes=VMEM_LIMIT_BYTES),
        name="mlp",
    )(x2, oa, ob, wout, g, wup, wdown, gf)


def _prep_in_weights(w_in):
    pad = jnp.zeros((D_MODEL, IN_COLS_PAD - w_in.shape[1]), w_in.dtype)
    return jnp.concatenate([w_in, pad], axis=1).astype(jnp.bfloat16)


def _prep_q_weights(w_q_b):
    dqk = QK_NOPE_DIM + QK_ROPE_DIM
    half = QK_ROPE_DIM // 2
    w = w_q_b.reshape(Q_LORA_RANK, N_HEADS_B, dqk)
    nope, rope = w[..., :QK_NOPE_DIM], w[..., QK_NOPE_DIM:]
    slot = jnp.concatenate([nope, rope, -rope[..., half:], rope[..., :half]], axis=-1)
    return slot.reshape(Q_LORA_RANK, QK_WIDTH_B).astype(jnp.bfloat16)


def _prep_kv_weights(w_kv_b):
    half = QK_ROPE_DIM // 2
    w = w_kv_b.reshape(KV_LORA_RANK, N_HEADS_B, QK_NOPE_DIM + V_HEAD_DIM)
    k_nope, v = w[..., :QK_NOPE_DIM], w[..., QK_NOPE_DIM:]
    eye = np.eye(QK_ROPE_DIM, dtype=np.float32)
    rot = np.zeros((QK_ROPE_DIM, QK_ROPE_DIM), np.float32)
    for jcol in range(half):
        rot[half + jcol, jcol] = -1.0
        rot[jcol, half + jcol] = 1.0

    def place(top, rope_block):
        blk = np.zeros((LAT_WIDTH - KV_LORA_RANK, N_HEADS_B, HEAD_SLOT), np.float32)
        for lo in (QK_NOPE_DIM, QK_NOPE_DIM + QK_ROPE_DIM):
            blk[:QK_ROPE_DIM, :, lo:lo + QK_ROPE_DIM] = rope_block[:, None, :]
        top = jnp.concatenate(
            [top, jnp.zeros((KV_LORA_RANK, N_HEADS_B, HEAD_SLOT - QK_NOPE_DIM), top.dtype)], axis=-1)
        full = jnp.concatenate([top, jnp.asarray(blk)], axis=0)
        return full.reshape(LAT_WIDTH, QK_WIDTH_B)

    wk = jnp.concatenate([place(k_nope, eye), place(jnp.zeros_like(k_nope), rot)], axis=1)

    vt_top = jnp.transpose(v, (1, 2, 0))
    vt_top = jnp.concatenate(
        [vt_top, jnp.zeros((N_HEADS_B, VT_ROWS - V_HEAD_DIM, KV_LORA_RANK), v.dtype)], axis=1)
    ones_sel = np.zeros((N_HEADS_B, VT_ROWS, LAT_WIDTH - KV_LORA_RANK), np.float32)
    ones_sel[:, VT_ONES_ROW, -1] = 1.0
    wvt = jnp.concatenate([vt_top, jnp.asarray(ones_sel)], axis=2)
    return wk.astype(jnp.bfloat16), wvt.reshape(N_HEADS_B * VT_ROWS, LAT_WIDTH).astype(jnp.bfloat16)


def _rope_lane_tables(seq):
    inv_freq = ROPE_THETA ** (-np.arange(0, QK_ROPE_DIM, 2, dtype=np.float64) / QK_ROPE_DIM)
    freqs = np.arange(seq, dtype=np.float64)[:, None] * inv_freq[None, :]
    cos, sin = np.cos(freqs), np.sin(freqs)
    cos2, sin2 = np.concatenate([cos, cos], axis=1), np.concatenate([sin, sin], axis=1)
    ones = np.ones((seq, QK_NOPE_DIM))
    q_scale = (QK_NOPE_DIM + QK_ROPE_DIM) ** -0.5 * math.log2(math.e)
    q_table = np.concatenate([ones, cos2, sin2], axis=1) * q_scale
    k_cos = np.concatenate([ones, cos2, cos2], axis=1)
    k_sin = np.concatenate([np.zeros_like(ones), sin2, sin2], axis=1)
    return tuple(jnp.asarray(t, jnp.float32) for t in (q_table, k_cos, k_sin))


def kernel(x, mix_norm_g, w_in, q_norm_g, w_q_b, kv_norm_g, w_kv_b, w_out,
           mlp_norm_g, w_up, w_down, rel_bias, final_norm_g):
    batch, seq, _ = x.shape
    depth = w_in.shape[0]
    assert depth == 1, "the final norm is fused into the single layer's MLP kernel"
    tq, ck, sk = _rope_lane_tables(seq)
    biases = _dilated_biases(rel_bias)
    layer = 0
    x2 = x.reshape(batch * seq, D_MODEL)
    wk, wvt = _prep_kv_weights(w_kv_b[layer])
    qa, ka, va, qa4, ka4, va4, qa16, ka16, va16, qm, km, vt = _proj_call(
        x2, mix_norm_g[layer][None], _prep_in_weights(w_in[layer]),
        q_norm_g[layer][None], _prep_q_weights(w_q_b[layer]),
        kv_norm_g[layer][None], wk, wvt, tq, ck, sk, batch, seq)

    shape_a = (batch, 1, seq, WIDTH_A)
    q_blocks = dict(zip([d for (_, d) in DILATED_PATTERNS], DIL_Q_BLOCKS))
    o16, l16 = _dilated_call(qa16, ka16, va16, biases[2], q_blocks=q_blocks[16])
    o4, l4 = _dilated_call(qa4, ka4, va4, biases[1], q_blocks=q_blocks[4])
    oa = _dilated_call(qa.reshape(shape_a), ka.reshape(shape_a), va.reshape(shape_a), biases[0],
                       q_blocks=q_blocks[1], prev=(o4, l4, o16, l16))

    ob = _mla_call(qm.reshape(batch, seq, QK_WIDTH_B), km.reshape(batch, seq, QK_WIDTH_B), vt, seq)

    out = _mlp_call(
        x2, oa.reshape(batch * seq, WIDTH_A), ob.reshape(batch * seq, WIDTH_B),
        w_out[layer].astype(jnp.bfloat16), mlp_norm_g[layer][None],
        w_up[layer].astype(jnp.bfloat16), w_down[layer].astype(jnp.bfloat16),
        final_norm_g[None])
    return out.reshape(batch, seq, D_MODEL)
```

```python
import functools
import math

import jax
import jax.numpy as jnp
import numpy as np
from jax import lax
from jax.experimental import pallas as pl
from jax.experimental.pallas import tpu as pltpu

D_MODEL = 1024
HEAD_DIM = 64
N_HEADS_A = 8
DILATED_PATTERNS = ((128, 1), (512, 4), (2048, 16))
N_HEADS_B = 8
Q_LORA_RANK = 256
KV_LORA_RANK = 128
QK_NOPE_DIM = 64
QK_ROPE_DIM = 32
V_HEAD_DIM = 64
ROPE_THETA = 10000.0
N_BUCKETS = 32
MAX_DISTANCE = 1024
D_FF = 4 * D_MODEL
NORM_EPS = 1e-6
NEG_INF = -1e30
WIDTH_A = N_HEADS_A * HEAD_DIM
WIDTH_B = N_HEADS_B * V_HEAD_DIM

LANES = 128
VMEM_LIMIT_BYTES = 56 * 1024 * 1024

HEAD_SLOT = LANES
QK_WIDTH_B = N_HEADS_B * HEAD_SLOT
VT_ROWS = 80
VT_ONES_ROW = V_HEAD_DIM
IN_COLS_PAD = 2048
CQ_OFF = 3 * WIDTH_A
LAT_WIDTH = 2 * LANES
RADIUS = 64
Q_BLK_A = 2 * RADIUS
K_WIN_A = 4 * RADIUS

PROJ_TM = 512
MLP_TM = 1024
MLA_TQ = 256
DIL_LOOKAHEAD = 3
DIL_BUFFERS = 4
DIL_Q_BLOCKS = (16, 8, 2)
MLA_TK = 512
FF_CHUNK = 1024


def _rms(xf, g):
    return xf * lax.rsqrt(jnp.mean(xf * xf, axis=-1, keepdims=True) + NORM_EPS) * g


def _proj_kernel(x_ref, g_ref, win_ref, qg_ref, wq_ref, kvg_ref, wk_ref, wvt_ref,
                 tq_ref, ck_ref, sk_ref,
                 qa_ref, ka_ref, va_ref, qa4_ref, ka4_ref, va4_ref, qa16_ref, ka16_ref, va16_ref,
                 qm_ref, km_ref, vt_ref, slab_scr, part_scr):
    x = x_ref[...]
    u = _rms(x, g_ref[...]).astype(jnp.bfloat16)

    def in_proj(lo, hi):
        return jnp.dot(u, win_ref[:, lo:hi], preferred_element_type=jnp.float32)

    tm = x.shape[0]
    n_slabs = WIDTH_A // LANES
    scales = (HEAD_DIM ** -0.5 * math.log2(math.e), 1.0, 1.0)
    groups = ((qa_ref, qa4_ref, qa16_ref), (ka_ref, ka4_ref, ka16_ref), (va_ref, va4_ref, va16_ref))
    for a, (nat_ref, *strided_refs) in enumerate(groups):
        t = in_proj(a * WIDTH_A, (a + 1) * WIDTH_A) * scales[a]
        nat_ref[...] = t.astype(jnp.bfloat16)
        ref4, ref16 = strided_refs
        d4, d16 = ref4.shape[1], ref16.shape[1]
        assert d16 == d4 * d4, "the second copy is a stride-d4 pass over the first"
        for g in range(n_slabs):
            cols = slice(g * LANES, (g + 1) * LANES)
            slab_scr[a * n_slabs + g] = t[:, cols]
            for r in range(d4):
                part = slab_scr[a * n_slabs + g, pl.ds(r, tm // d4, stride=d4), :]
                ref4[0, r, :, cols] = part.astype(jnp.bfloat16)
                part_scr[a * n_slabs + g, r] = part
            for r in range(d16):
                part = part_scr[a * n_slabs + g, r % d4, pl.ds(r // d4, tm // d16, stride=d4), :]
                ref16[0, r, :, cols] = part.astype(jnp.bfloat16)
    proj = in_proj(CQ_OFF, IN_COLS_PAD)

    cq = _rms(proj[:, :Q_LORA_RANK], qg_ref[...]).astype(jnp.bfloat16)
    q2 = jnp.dot(cq, wq_ref[...], preferred_element_type=jnp.float32)
    qm_ref[...] = (q2 * jnp.tile(tq_ref[...], (1, N_HEADS_B))).astype(jnp.bfloat16)

    ckv = _rms(proj[:, Q_LORA_RANK:Q_LORA_RANK + KV_LORA_RANK], kvg_ref[...])
    hi = proj[:, Q_LORA_RANK + KV_LORA_RANK:]
    lane = lax.broadcasted_iota(jnp.int32, hi.shape, 1)
    hi = jnp.where(lane == LANES - 1, 1.0, hi)
    lat = jnp.concatenate([ckv, hi], axis=-1).astype(jnp.bfloat16)
    k2 = jnp.dot(lat, wk_ref[...], preferred_element_type=jnp.float32)
    ck_t = jnp.tile(ck_ref[...], (1, N_HEADS_B))
    sk_t = jnp.tile(sk_ref[...], (1, N_HEADS_B))
    km_ref[...] = (k2[:, :QK_WIDTH_B] * ck_t + k2[:, QK_WIDTH_B:] * sk_t).astype(jnp.bfloat16)
    vt = lax.dot_general(wvt_ref[...], lat, (((1,), (1,)), ((), ())),
                         preferred_element_type=jnp.float32)
    vt_ref[0, 0] = vt.astype(jnp.bfloat16)


def _proj_call(x2, g, win, qg, wq, kvg, wk, wvt, tq, ck, sk, batch, seq):
    tokens = x2.shape[0]
    tm = PROJ_TM
    sblk = seq // tm
    row = lambda i: (i, 0)
    const = lambda i: (0, 0)
    pos = lambda i: (i % sblk, 0)

    def wspec(shape):
        return pl.BlockSpec(shape, const, pipeline_mode=pl.Buffered(1))

    strides = [d for (_, d) in DILATED_PATTERNS if d > 1]

    def strided_spec(d):
        return pl.BlockSpec((1, d, tm // d, WIDTH_A), lambda i: (i // sblk, 0, i % sblk, 0))

    bf = jnp.bfloat16
    return pl.pallas_call(
        _proj_kernel,
        grid=(tokens // tm,),
        in_specs=[
            pl.BlockSpec((tm, D_MODEL), row),
            wspec((1, D_MODEL)),
            wspec((D_MODEL, IN_COLS_PAD)),
            wspec((1, Q_LORA_RANK)),
            wspec((Q_LORA_RANK, QK_WIDTH_B)),
            wspec((1, KV_LORA_RANK)),
            wspec((LAT_WIDTH, 2 * QK_WIDTH_B)),
            wspec((N_HEADS_B * VT_ROWS, LAT_WIDTH)),
            pl.BlockSpec((tm, HEAD_SLOT), pos),
            pl.BlockSpec((tm, HEAD_SLOT), pos),
            pl.BlockSpec((tm, HEAD_SLOT), pos),
        ],
        out_specs=[
            pl.BlockSpec((tm, WIDTH_A), row),
            pl.BlockSpec((tm, WIDTH_A), row),
            pl.BlockSpec((tm, WIDTH_A), row),
            *[strided_spec(d) for d in strides for _ in range(3)],
            pl.BlockSpec((tm, QK_WIDTH_B), row),
            pl.BlockSpec((tm, QK_WIDTH_B), row),
            pl.BlockSpec((1, 1, N_HEADS_B * VT_ROWS, tm), lambda i: (i // sblk, i % sblk, 0, 0)),
        ],
        out_shape=[
            jax.ShapeDtypeStruct((tokens, WIDTH_A), bf),
            jax.ShapeDtypeStruct((tokens, WIDTH_A), bf),
            jax.ShapeDtypeStruct((tokens, WIDTH_A), bf),
            *[jax.ShapeDtypeStruct((batch, d, seq // d, WIDTH_A), bf) for d in strides for _ in range(3)],
            jax.ShapeDtypeStruct((tokens, QK_WIDTH_B), bf),
            jax.ShapeDtypeStruct((tokens, QK_WIDTH_B), bf),
            jax.ShapeDtypeStruct((batch, sblk, N_HEADS_B * VT_ROWS, tm), bf),
        ],
        scratch_shapes=[
            pltpu.VMEM((3 * WIDTH_A // LANES, tm, LANES), jnp.float32),
            pltpu.VMEM((3 * WIDTH_A // LANES, strides[0], tm // strides[0], LANES), jnp.float32),
        ],
        compiler_params=pltpu.CompilerParams(
            dimension_semantics=("arbitrary",), vmem_limit_bytes=VMEM_LIMIT_BYTES),
        name="proj",
    )(x2, g, win, qg, wq, kvg, wk, wvt, tq, ck, sk)


def _dilated_kernel(*refs, dilation, final, q_blocks, n_buf):
    if final:
        q_ref, k_ref, v_ref, bias_ref, o2_ref, l2_ref, o3_ref, l3_ref, out_ref = refs[:9]
    else:
        q_ref, k_ref, v_ref, bias_ref, o_ref, l_ref = refs[:6]
    scratch = refs[-2 * n_buf:]
    z_bufs, m_bufs = scratch[:n_buf], scratch[n_buf:]
    n_pairs = N_HEADS_A // 2
    assert n_pairs % n_buf == 0 and DIL_LOOKAHEAD < n_buf, "buffer rotation must restart per item"
    assert dilation & (dilation - 1) == 0, "work items are decoded with shifts"
    j = pl.program_id(1)
    sub_len = k_ref.shape[2]
    nblk = sub_len // Q_BLK_A
    n_items = dilation * q_blocks
    lane = lax.broadcasted_iota(jnp.int32, (Q_BLK_A, LANES), 1)
    low_half = lane < HEAD_DIM

    def coords(item):
        if isinstance(item, int):
            return (item // dilation) * Q_BLK_A, item % dilation
        qb = lax.shift_right_logical(item, int(math.log2(dilation)))
        return pl.multiple_of(qb * Q_BLK_A, Q_BLK_A), lax.bitwise_and(item, dilation - 1)

    def window(q0):
        jb = j * q_blocks + q0 // Q_BLK_A
        w0 = pl.multiple_of(jnp.clip(jb * Q_BLK_A - RADIUS, 0, sub_len - K_WIN_A), RADIUS)
        variant = jnp.where(jb == 0, 0, jnp.where(jb == nblk - 1, 2, 1))
        return pl.ds(w0, K_WIN_A), variant

    def scores(item, g):
        q0, r = coords(item)
        win, variant = window(q0)
        cols = slice(g * LANES, (g + 1) * LANES)
        qg = q_ref[0, r, pl.ds(q0, Q_BLK_A), cols]
        zero = jnp.zeros_like(qg)
        q2 = jnp.concatenate([jnp.where(low_half, qg, zero), jnp.where(low_half, zero, qg)], axis=0)
        s = lax.dot_general(q2, k_ref[0, r, win, cols], (((1,), (1,)), ((), ())),
                            preferred_element_type=jnp.float32)
        s = s + bias_ref[variant, 2 * g:2 * g + 2].reshape(2 * Q_BLK_A, K_WIN_A)
        m = jnp.max(jnp.maximum(s[:, :LANES], s[:, LANES:]), axis=-1, keepdims=True)
        z_bufs[g % n_buf][...] = s - m
        m_bufs[g % n_buf][...] = jnp.broadcast_to(m, (2 * Q_BLK_A, LANES))

    def attend(item, g):
        q0, r = coords(item)
        win, _ = window(q0)
        cols = slice(g * LANES, (g + 1) * LANES)
        p = jnp.exp2(z_bufs[g % n_buf][...])
        den = jnp.sum(p[:, :LANES] + p[:, LANES:], axis=-1, keepdims=True)
        pv = jnp.dot(p.astype(jnp.bfloat16), v_ref[0, r, win, cols],
                     preferred_element_type=jnp.float32)
        m = m_bufs[g % n_buf][...]
        den_pair = jnp.where(low_half, den[:Q_BLK_A], den[Q_BLK_A:])
        o_pair = jnp.where(low_half, pv[:Q_BLK_A], pv[Q_BLK_A:]) / den_pair
        l_pair = jnp.where(low_half, m[:Q_BLK_A], m[Q_BLK_A:]) + jnp.log2(den_pair)
        return o_pair, l_pair

    def work_item(item, carry):
        q0, r = coords(item)
        item_next = jnp.minimum(item + 1, n_items - 1)
        if dilation == 1:
            rows = pl.ds(q0, Q_BLK_A)
        else:
            rows = pl.ds(q0 * dilation + r, Q_BLK_A, stride=dilation)
        for g in range(n_pairs):
            ahead = g + DIL_LOOKAHEAD
            if ahead < n_pairs:
                scores(item, ahead)
            else:
                scores(item_next, ahead - n_pairs)
            o_pair, l_pair = attend(item, g)
            if final:
                l2, l3 = l2_ref[0, g, rows, :], l3_ref[0, g, rows, :]
                top = jnp.maximum(jnp.maximum(l_pair, l2), l3)
                w1, w2, w3 = jnp.exp2(l_pair - top), jnp.exp2(l2 - top), jnp.exp2(l3 - top)
                o2, o3 = (pltpu.unpack_elementwise(
                    ref[0, g // 2, rows, :], index=g % 2, packed_dtype=jnp.bfloat16,
                    unpacked_dtype=jnp.float32) for ref in (o2_ref, o3_ref))
                merged = (w1 * o_pair + w2 * o2 + w3 * o3) / (w1 + w2 + w3)
                out_ref[0, rows, g * LANES:(g + 1) * LANES] = merged.astype(out_ref.dtype)
            else:
                l_ref[0, g, rows, :] = l_pair
                if g % 2 == 0:
                    o_even = o_pair
                else:
                    o_ref[0, g // 2, rows, :] = pltpu.pack_elementwise(
                        [o_even, o_pair], packed_dtype=jnp.bfloat16)
        return carry

    for g in range(DIL_LOOKAHEAD):
        scores(0, g)
    lax.fori_loop(0, n_items, work_item, 0)


def _dilated_call(q, k, v, bias, q_blocks=1, prev=None):
    batch, dilation, sub_len, _ = q.shape
    seq = sub_len * dilation
    final = prev is not None
    tokens = Q_BLK_A * dilation * q_blocks
    n_pairs = N_HEADS_A // 2
    whole = pl.BlockSpec((1, dilation, sub_len, WIDTH_A), lambda b, j: (b, 0, 0, 0))
    in_specs = [
        pl.BlockSpec((1, dilation, q_blocks * Q_BLK_A, WIDTH_A), lambda b, j: (b, 0, j, 0)),
        whole,
        whole,
        pl.BlockSpec((3, N_HEADS_A, Q_BLK_A, K_WIN_A), lambda b, j: (0, 0, 0, 0),
                     pipeline_mode=pl.Buffered(1)),
    ]
    args = [q, k, v, bias]
    def slab_spec(n):
        return pl.BlockSpec((1, n, tokens, LANES), lambda b, j: (b, 0, j, 0))

    slab_specs = [slab_spec(n_pairs // 2), slab_spec(n_pairs)]
    if final:
        in_specs += slab_specs * 2
        args += list(prev)
        out_specs = pl.BlockSpec((1, tokens, WIDTH_A), lambda b, j: (b, j, 0))
        out_shape = jax.ShapeDtypeStruct((batch, seq, WIDTH_A), jnp.bfloat16)
    else:
        out_specs = slab_specs
        out_shape = [jax.ShapeDtypeStruct((batch, n_pairs // 2, seq, LANES), jnp.uint32),
                     jax.ShapeDtypeStruct((batch, n_pairs, seq, LANES), jnp.float32)]
    return pl.pallas_call(
        functools.partial(_dilated_kernel, dilation=dilation, final=final, q_blocks=q_blocks,
                          n_buf=DIL_BUFFERS),
        grid=(batch, sub_len // (q_blocks * Q_BLK_A)),
        in_specs=in_specs,
        out_specs=out_specs,
        out_shape=out_shape,
        scratch_shapes=(
            [pltpu.VMEM((2 * Q_BLK_A, K_WIN_A), jnp.float32)] * DIL_BUFFERS
            + [pltpu.VMEM((2 * Q_BLK_A, LANES), jnp.float32)] * DIL_BUFFERS),
        compiler_params=pltpu.CompilerParams(
            dimension_semantics=("arbitrary", "arbitrary"),
            vmem_limit_bytes=VMEM_LIMIT_BYTES),
        name=f"dilated_d{dilation}",
    )(*args)


def _t5_buckets(rel):
    nb = N_BUCKETS // 2
    max_exact = nb // 2
    ret = (rel > 0).astype(np.int32) * nb
    n = np.abs(rel)
    large = max_exact + (np.log(np.maximum(n, 1) / max_exact)
                         / np.log(MAX_DISTANCE / max_exact) * (nb - max_exact)).astype(np.int32)
    large = np.minimum(large, nb - 1)
    return (ret + np.where(n < max_exact, n, large)).astype(np.int32)


def _dilated_biases(rel_bias):
    r = np.arange(Q_BLK_A)[:, None]
    c = np.arange(K_WIN_A)[None, :]
    rel = np.stack([c - shift - r for shift in (0, RADIUS, 2 * RADIUS)])
    valid = np.abs(rel) <= RADIUS
    buckets = jnp.asarray(np.stack([_t5_buckets(rel * d) for (_, d) in DILATED_PATTERNS]))
    onehot = (buckets[None] == jnp.arange(N_BUCKETS)[:, None, None, None, None]).astype(jnp.float32)
    b = jnp.einsum('nh,npvrc->pvhrc', rel_bias.astype(jnp.float32), onehot,
                   precision=lax.Precision.HIGHEST)
    tables = jnp.where(jnp.asarray(valid)[None, :, None], b * math.log2(math.e), NEG_INF)
    return [tables[p] for p in range(len(DILATED_PATTERNS))]


def _mla_kernel(q_ref, k_ref, vt_ref, o_ref, s_even, s_odd, p_even, p_odd, m_even, m_odd):
    seq = k_ref.shape[1]
    nk = seq // MLA_TK
    nq = seq // MLA_TQ
    heads = range(2)
    s_bufs, p_bufs, m_bufs = (s_even, s_odd), (p_even, p_odd), (m_even, m_odd)
    assert nq % 2 == 0 and nq >= 4

    def q_rows(t):
        start = t * MLA_TQ
        return pl.ds(start if isinstance(t, int) else pl.multiple_of(start, MLA_TQ), MLA_TQ)

    def keys(c):
        return slice(c * MLA_TK, (c + 1) * MLA_TK)

    def stage(score_t=None, exp_t=None, pv_t=None):
        run_max = [None, None]
        for c in range(nk):
            if score_t is not None:
                t, par = score_t
                for hh in heads:
                    lanes = slice(hh * HEAD_SLOT, (hh + 1) * HEAD_SLOT)
                    s = lax.dot_general(k_ref[0, keys(c), lanes], q_ref[0, q_rows(t), lanes],
                                        (((1,), (1,)), ((), ())),
                                        preferred_element_type=jnp.float32)
                    s_bufs[par][hh, keys(c), :] = s
                    mc = jnp.max(s, axis=0, keepdims=True)
                    run_max[hh] = mc if c == 0 else jnp.maximum(run_max[hh], mc)
            if exp_t is not None:
                _, par = exp_t
                for hh in heads:
                    z = s_bufs[par][hh, keys(c), :] - m_bufs[par][hh]
                    p_bufs[par][hh, keys(c), :] = jnp.exp2(z).astype(jnp.bfloat16)
        if score_t is not None:
            for hh in heads:
                m_bufs[score_t[1]][hh] = run_max[hh]
        if pv_t is not None:
            _, par = pv_t
            n_blk = vt_ref.shape[1]
            outs = []
            for hh in heads:
                vt = jnp.concatenate([vt_ref[0, b, hh * VT_ROWS:(hh + 1) * VT_ROWS, :]
                                      for b in range(n_blk)], axis=1)
                acc = jnp.dot(vt, p_bufs[par][hh], preferred_element_type=jnp.float32)
                outs.append(acc[:V_HEAD_DIM] / acc[VT_ONES_ROW:VT_ONES_ROW + 1])
            o_ref[0, q_rows(pv_t[0]), :] = jnp.concatenate(outs, axis=0).T.astype(o_ref.dtype)

    stage(score_t=(0, 0))
    stage(score_t=(1, 1), exp_t=(0, 0))

    def stage_pair(i, carry):
        t = 2 * i + 1
        stage(score_t=(t + 1, 0), exp_t=(t, 1), pv_t=(t - 1, 0))
        stage(score_t=(t + 2, 1), exp_t=(t + 1, 0), pv_t=(t, 1))
        return carry

    lax.fori_loop(0, nq // 2 - 1, stage_pair, 0)
    stage(exp_t=(nq - 1, 1), pv_t=(nq - 2, 0))
    stage(pv_t=(nq - 1, 1))


def _mla_call(qm, km, vt, seq):
    batch, n_blk, _, blk_keys = vt.shape
    assert blk_keys % MLA_TK == 0, "a key chunk must not straddle two V^T blocks"
    pair = lambda b, g: (b, 0, g)
    return pl.pallas_call(
        _mla_kernel,
        grid=(batch, N_HEADS_B // 2),
        in_specs=[
            pl.BlockSpec((1, seq, 2 * HEAD_SLOT), pair),
            pl.BlockSpec((1, seq, 2 * HEAD_SLOT), pair),
            pl.BlockSpec((1, n_blk, 2 * VT_ROWS, blk_keys), lambda b, g: (b, 0, g, 0)),
        ],
        out_specs=pl.BlockSpec((1, seq, 2 * V_HEAD_DIM), pair),
        out_shape=jax.ShapeDtypeStruct((batch, seq, WIDTH_B), jnp.bfloat16),
        scratch_shapes=(
            [pltpu.VMEM((2, seq, MLA_TQ), jnp.float32)] * 2
            + [pltpu.VMEM((2, seq, MLA_TQ), jnp.bfloat16)] * 2
            + [pltpu.VMEM((2, 1, MLA_TQ), jnp.float32)] * 2),
        compiler_params=pltpu.CompilerParams(
            dimension_semantics=("arbitrary", "arbitrary"),
            vmem_limit_bytes=VMEM_LIMIT_BYTES),
        name="mla",
    )(qm, km, vt)


def _mlp_kernel(x_ref, oa_ref, ob_ref, wout_ref, g_ref, wup_ref, wdown_ref, gf_ref, o_ref):
    o_cat = jnp.concatenate([oa_ref[...], ob_ref[...]], axis=-1)
    h = x_ref[...] + jnp.dot(o_cat, wout_ref[...], preferred_element_type=jnp.float32)
    u = _rms(h, g_ref[...]).astype(jnp.bfloat16)
    acc = h
    for c in range(D_FF // FF_CHUNK):
        a = jnp.dot(u, wup_ref[:, c * FF_CHUNK:(c + 1) * FF_CHUNK],
                    preferred_element_type=jnp.float32)
        a = jnp.square(jnp.maximum(a, 0.0)).astype(jnp.bfloat16)
        acc = acc + jnp.dot(a, wdown_ref[c * FF_CHUNK:(c + 1) * FF_CHUNK, :],
                            preferred_element_type=jnp.float32)
    o_ref[...] = _rms(acc, gf_ref[...])


def _mlp_call(x2, oa, ob, wout, g, wup, wdown, gf):
    tokens = x2.shape[0]
    tm = MLP_TM
    row = lambda i: (i, 0)
    const = lambda i: (0, 0)

    def wspec(shape):
        return pl.BlockSpec(shape, const, pipeline_mode=pl.Buffered(1))

    return pl.pallas_call(
        _mlp_kernel,
        grid=(tokens // tm,),
        in_specs=[
            pl.BlockSpec((tm, D_MODEL), row),
            pl.BlockSpec((tm, WIDTH_A), row),
            pl.BlockSpec((tm, WIDTH_B), row),
            wspec((WIDTH_A + WIDTH_B, D_MODEL)),
            wspec((1, D_MODEL)),
            wspec((D_MODEL, D_FF)),
            wspec((D_FF, D_MODEL)),
            wspec((1, D_MODEL)),
        ],
        out_specs=pl.BlockSpec((tm, D_MODEL), row),
        out_shape=jax.ShapeDtypeStruct((tokens, D_MODEL), jnp.float32),
        compiler_params=pltpu.CompilerParams(
            dimension_semantics=("arbitrary",), vmem_limit_bytes=VMEM_LIMIT_BYTES),
        name="mlp",
    )(x2, oa, ob, wout, g, wup, wdown, gf)


def _prep_in_weights(w_in):
    pad = jnp.zeros((D_MODEL, IN_COLS_PAD - w_in.shape[1]), w_in.dtype)
    return jnp.concatenate([w_in, pad], axis=1).astype(jnp.bfloat16)


def _prep_q_weights(w_q_b):
    dqk = QK_NOPE_DIM + QK_ROPE_DIM
    half = QK_ROPE_DIM // 2
    w = w_q_b.reshape(Q_LORA_RANK, N_HEADS_B, dqk)
    nope, rope = w[..., :QK_NOPE_DIM], w[..., QK_NOPE_DIM:]
    slot = jnp.concatenate([nope, rope, -rope[..., half:], rope[..., :half]], axis=-1)
    return slot.reshape(Q_LORA_RANK, QK_WIDTH_B).astype(jnp.bfloat16)


def _prep_kv_weights(w_kv_b):
    half = QK_ROPE_DIM // 2
    w = w_kv_b.reshape(KV_LORA_RANK, N_HEADS_B, QK_NOPE_DIM + V_HEAD_DIM)
    k_nope, v = w[..., :QK_NOPE_DIM], w[..., QK_NOPE_DIM:]
    eye = np.eye(QK_ROPE_DIM, dtype=np.float32)
    rot = np.zeros((QK_ROPE_DIM, QK_ROPE_DIM), np.float32)
    for jcol in range(half):
        rot[half + jcol, jcol] = -1.0
        rot[jcol, half + jcol] = 1.0

    def place(top, rope_block):
        blk = np.zeros((LAT_WIDTH - KV_LORA_RANK, N_HEADS_B, HEAD_SLOT), np.float32)
        for lo in (QK_NOPE_DIM, QK_NOPE_DIM + QK_ROPE_DIM):
            blk[:QK_ROPE_DIM, :, lo:lo + QK_ROPE_DIM] = rope_block[:, None, :]
        top = jnp.concatenate(
            [top, jnp.zeros((KV_LORA_RANK, N_HEADS_B, HEAD_SLOT - QK_NOPE_DIM), top.dtype)], axis=-1)
        full = jnp.concatenate([top, jnp.asarray(blk)], axis=0)
        return full.reshape(LAT_WIDTH, QK_WIDTH_B)

    wk = jnp.concatenate([place(k_nope, eye), place(jnp.zeros_like(k_nope), rot)], axis=1)

    vt_top = jnp.transpose(v, (1, 2, 0))
    vt_top = jnp.concatenate(
        [vt_top, jnp.zeros((N_HEADS_B, VT_ROWS - V_HEAD_DIM, KV_LORA_RANK), v.dtype)], axis=1)
    ones_sel = np.zeros((N_HEADS_B, VT_ROWS, LAT_WIDTH - KV_LORA_RANK), np.float32)
    ones_sel[:, VT_ONES_ROW, -1] = 1.0
    wvt = jnp.concatenate([vt_top, jnp.asarray(ones_sel)], axis=2)
    return wk.astype(jnp.bfloat16), wvt.reshape(N_HEADS_B * VT_ROWS, LAT_WIDTH).astype(jnp.bfloat16)


def _rope_lane_tables(seq):
    inv_freq = ROPE_THETA ** (-np.arange(0, QK_ROPE_DIM, 2, dtype=np.float64) / QK_ROPE_DIM)
    freqs = np.arange(seq, dtype=np.float64)[:, None] * inv_freq[None, :]
    cos, sin = np.cos(freqs), np.sin(freqs)
    cos2, sin2 = np.concatenate([cos, cos], axis=1), np.concatenate([sin, sin], axis=1)
    ones = np.ones((seq, QK_NOPE_DIM))
    q_scale = (QK_NOPE_DIM + QK_ROPE_DIM) ** -0.5 * math.log2(math.e)
    q_table = np.concatenate([ones, cos2, sin2], axis=1) * q_scale
    k_cos = np.concatenate([ones, cos2, cos2], axis=1)
    k_sin = np.concatenate([np.zeros_like(ones), sin2, sin2], axis=1)
    return tuple(jnp.asarray(t, jnp.float32) for t in (q_table, k_cos, k_sin))


def kernel(x, mix_norm_g, w_in, q_norm_g, w_q_b, kv_norm_g, w_kv_b, w_out,
           mlp_norm_g, w_up, w_down, rel_bias, final_norm_g):
    batch, seq, _ = x.shape
    depth = w_in.shape[0]
    assert depth == 1, "the final norm is fused into the single layer's MLP kernel"
    tq, ck, sk = _rope_lane_tables(seq)
    biases = _dilated_biases(rel_bias)
    layer = 0
    x2 = x.reshape(batch * seq, D_MODEL)
    wk, wvt = _prep_kv_weights(w_kv_b[layer])
    qa, ka, va, qa4, ka4, va4, qa16, ka16, va16, qm, km, vt = _proj_call(
        x2, mix_norm_g[layer][None], _prep_in_weights(w_in[layer]),
        q_norm_g[layer][None], _prep_q_weights(w_q_b[layer]),
        kv_norm_g[layer][None], wk, wvt, tq, ck, sk, batch, seq)

    shape_a = (batch, 1, seq, WIDTH_A)
    q_blocks = dict(zip([d for (_, d) in DILATED_PATTERNS], DIL_Q_BLOCKS))
    o16, l16 = _dilated_call(qa16, ka16, va16, biases[2], q_blocks=q_blocks[16])
    o4, l4 = _dilated_call(qa4, ka4, va4, biases[1], q_blocks=q_blocks[4])
    oa = _dilated_call(qa.reshape(shape_a), ka.reshape(shape_a), va.reshape(shape_a), biases[0],
                       q_blocks=q_blocks[1], prev=(o4, l4, o16, l16))

    ob = _mla_call(qm.reshape(batch, seq, QK_WIDTH_B), km.reshape(batch, seq, QK_WIDTH_B), vt, seq)

    out = _mlp_call(
        x2, oa.reshape(batch * seq, WIDTH_A), ob.reshape(batch * seq, WIDTH_B),
        w_out[layer].astype(jnp.bfloat16), mlp_norm_g[layer][None],
        w_up[layer].astype(jnp.bfloat16), w_down[layer].astype(jnp.bfloat16),
        final_norm_g[None])
    return out.reshape(batch, seq, D_MODEL)
```

```python
import functools
import math

import jax
import jax.numpy as jnp
import numpy as np
from jax import lax
from jax.experimental import pallas as pl
from jax.experimental.pallas import tpu as pltpu

D_MODEL = 1024
HEAD_DIM = 64
N_HEADS_A = 8
DILATED_PATTERNS = ((128, 1), (512, 4), (2048, 16))
N_HEADS_B = 8
Q_LORA_RANK = 256
KV_LORA_RANK = 128
QK_NOPE_DIM = 64
QK_ROPE_DIM = 32
V_HEAD_DIM = 64
ROPE_THETA = 10000.0
N_BUCKETS = 32
MAX_DISTANCE = 1024
D_FF = 4 * D_MODEL
NORM_EPS = 1e-6
NEG_INF = -1e30
WIDTH_A = N_HEADS_A * HEAD_DIM
WIDTH_B = N_HEADS_B * V_HEAD_DIM

LANES = 128
VMEM_LIMIT_BYTES = 56 * 1024 * 1024

HEAD_SLOT = LANES
QK_WIDTH_B = N_HEADS_B * HEAD_SLOT
VT_ROWS = 80
VT_ONES_ROW = V_HEAD_DIM
IN_COLS_PAD = 2048
CQ_OFF = 3 * WIDTH_A
LAT_WIDTH = 2 * LANES
RADIUS = 64
Q_BLK_A = 2 * RADIUS
K_WIN_A = 4 * RADIUS

PROJ_TM = 512
MLP_TM = 1024
MLA_TQ = 256
DIL_LOOKAHEAD = 3
DIL_BUFFERS = 4
DIL_Q_BLOCKS = (8, 4, 1)
MLA_TK = 512
FF_CHUNK = 1024


def _rms(xf, g):
    return xf * lax.rsqrt(jnp.mean(xf * xf, axis=-1, keepdims=True) + NORM_EPS) * g


def _proj_kernel(x_ref, g_ref, win_ref, qg_ref, wq_ref, kvg_ref, wk_ref, wvt_ref,
                 tq_ref, ck_ref, sk_ref,
                 qa_ref, ka_ref, va_ref, qa4_ref, ka4_ref, va4_ref, qa16_ref, ka16_ref, va16_ref,
                 qm_ref, km_ref, vt_ref, slab_scr, part_scr):
    x = x_ref[...]
    u = _rms(x, g_ref[...]).astype(jnp.bfloat16)

    def in_proj(lo, hi):
        return jnp.dot(u, win_ref[:, lo:hi], preferred_element_type=jnp.float32)

    tm = x.shape[0]
    n_slabs = WIDTH_A // LANES
    scales = (HEAD_DIM ** -0.5 * math.log2(math.e), 1.0, 1.0)
    groups = ((qa_ref, qa4_ref, qa16_ref), (ka_ref, ka4_ref, ka16_ref), (va_ref, va4_ref, va16_ref))

    def dilated_group(a):
        nat_ref, *strided_refs = groups[a]
        t = in_proj(a * WIDTH_A, (a + 1) * WIDTH_A) * scales[a]
        nat_ref[...] = t.astype(jnp.bfloat16)
        ref4, ref16 = strided_refs
        d4, d16 = ref4.shape[1], ref16.shape[1]
        assert d16 == d4 * d4, "the second copy is a stride-d4 pass over the first"
        for g in range(n_slabs):
            cols = slice(g * LANES, (g + 1) * LANES)
            slab_scr[a * n_slabs + g] = t[:, cols]
            for r in range(d4):
                part = slab_scr[a * n_slabs + g, pl.ds(r, tm // d4, stride=d4), :]
                ref4[0, r, :, cols] = part.astype(jnp.bfloat16)
                part_scr[a * n_slabs + g, r] = part
            for r in range(d16):
                part = part_scr[a * n_slabs + g, r % d4, pl.ds(r // d4, tm // d16, stride=d4), :]
                ref16[0, r, :, cols] = part.astype(jnp.bfloat16)

    dilated_group(0)
    proj = in_proj(CQ_OFF, IN_COLS_PAD)
    dilated_group(1)

    cq = _rms(proj[:, :Q_LORA_RANK], qg_ref[...]).astype(jnp.bfloat16)
    q2 = jnp.dot(cq, wq_ref[...], preferred_element_type=jnp.float32)
    qm_ref[...] = (q2 * jnp.tile(tq_ref[...], (1, N_HEADS_B))).astype(jnp.bfloat16)
    dilated_group(2)

    ckv = _rms(proj[:, Q_LORA_RANK:Q_LORA_RANK + KV_LORA_RANK], kvg_ref[...])
    hi = proj[:, Q_LORA_RANK + KV_LORA_RANK:]
    lane = lax.broadcasted_iota(jnp.int32, hi.shape, 1)
    hi = jnp.where(lane == LANES - 1, 1.0, hi)
    lat = jnp.concatenate([ckv, hi], axis=-1).astype(jnp.bfloat16)
    k2 = jnp.dot(lat, wk_ref[...], preferred_element_type=jnp.float32)
    rope = (k2[:, QK_WIDTH_B:QK_WIDTH_B + HEAD_SLOT] * ck_ref[...]
            + k2[:, QK_WIDTH_B + HEAD_SLOT:] * sk_ref[...])
    nope_lane = lax.broadcasted_iota(jnp.int32, rope.shape, 1) < QK_NOPE_DIM
    for h in range(N_HEADS_B):
        slot = slice(h * HEAD_SLOT, (h + 1) * HEAD_SLOT)
        km_ref[:, slot] = jnp.where(nope_lane, k2[:, slot], rope).astype(jnp.bfloat16)
    vt = lax.dot_general(wvt_ref[...], lat, (((1,), (1,)), ((), ())),
                         preferred_element_type=jnp.float32)
    vt_ref[0, 0] = vt.astype(jnp.bfloat16)


def _proj_call(x2, g, win, qg, wq, kvg, wk, wvt, tq, ck, sk, batch, seq):
    tokens = x2.shape[0]
    tm = PROJ_TM
    sblk = seq // tm
    row = lambda i: (i, 0)
    const = lambda i: (0, 0)
    pos = lambda i: (i % sblk, 0)

    def wspec(shape):
        return pl.BlockSpec(shape, const, pipeline_mode=pl.Buffered(1))

    strides = [d for (_, d) in DILATED_PATTERNS if d > 1]

    def strided_spec(d):
        return pl.BlockSpec((1, d, tm // d, WIDTH_A), lambda i: (i // sblk, 0, i % sblk, 0))

    bf = jnp.bfloat16
    return pl.pallas_call(
        _proj_kernel,
        grid=(tokens // tm,),
        in_specs=[
            pl.BlockSpec((tm, D_MODEL), row),
            wspec((1, D_MODEL)),
            wspec((D_MODEL, IN_COLS_PAD)),
            wspec((1, Q_LORA_RANK)),
            wspec((Q_LORA_RANK, QK_WIDTH_B)),
            wspec((1, KV_LORA_RANK)),
            wspec((LAT_WIDTH, QK_WIDTH_B + 2 * HEAD_SLOT)),
            wspec((N_HEADS_B * VT_ROWS, LAT_WIDTH)),
            pl.BlockSpec((tm, HEAD_SLOT), pos),
            pl.BlockSpec((tm, HEAD_SLOT), pos),
            pl.BlockSpec((tm, HEAD_SLOT), pos),
        ],
        out_specs=[
            pl.BlockSpec((tm, WIDTH_A), row),
            pl.BlockSpec((tm, WIDTH_A), row),
            pl.BlockSpec((tm, WIDTH_A), row),
            *[strided_spec(d) for d in strides for _ in range(3)],
            pl.BlockSpec((tm, QK_WIDTH_B), row),
            pl.BlockSpec((tm, QK_WIDTH_B), row),
            pl.BlockSpec((1, 1, N_HEADS_B * VT_ROWS, tm), lambda i: (i // sblk, i % sblk, 0, 0)),
        ],
        out_shape=[
            jax.ShapeDtypeStruct((tokens, WIDTH_A), bf),
            jax.ShapeDtypeStruct((tokens, WIDTH_A), bf),
            jax.ShapeDtypeStruct((tokens, WIDTH_A), bf),
            *[jax.ShapeDtypeStruct((batch, d, seq // d, WIDTH_A), bf) for d in strides for _ in range(3)],
            jax.ShapeDtypeStruct((tokens, QK_WIDTH_B), bf),
            jax.ShapeDtypeStruct((tokens, QK_WIDTH_B), bf),
            jax.ShapeDtypeStruct((batch, sblk, N_HEADS_B * VT_ROWS, tm), bf),
        ],
        scratch_shapes=[
            pltpu.VMEM((3 * WIDTH_A // LANES, tm, LANES), jnp.float32),
            pltpu.VMEM((3 * WIDTH_A // LANES, strides[0], tm // strides[0], LANES), jnp.float32),
        ],
        compiler_params=pltpu.CompilerParams(
            dimension_semantics=("arbitrary",), vmem_limit_bytes=VMEM_LIMIT_BYTES),
        name="proj",
    )(x2, g, win, qg, wq, kvg, wk, wvt, tq, ck, sk)


def _dilated_kernel(*refs, dilation, final, q_blocks, n_buf):
    if final:
        q_ref, k_ref, v_ref, bias_ref, o2_ref, l2_ref, o3_ref, l3_ref, out_ref = refs[:9]
    else:
        q_ref, k_ref, v_ref, bias_ref, o_ref, l_ref = refs[:6]
    scratch = refs[-2 * n_buf:]
    z_bufs, m_bufs = scratch[:n_buf], scratch[n_buf:]
    n_pairs = N_HEADS_A // 2
    assert n_pairs % n_buf == 0 and DIL_LOOKAHEAD < n_buf, "buffer rotation must restart per item"
    assert dilation & (dilation - 1) == 0, "work items are decoded with shifts"
    j = pl.program_id(1)
    sub_len = k_ref.shape[2]
    nblk = sub_len // Q_BLK_A
    n_items = dilation * q_blocks
    lane = lax.broadcasted_iota(jnp.int32, (Q_BLK_A, LANES), 1)
    low_half = lane < HEAD_DIM

    def coords(item):
        if isinstance(item, int):
            return (item // dilation) * Q_BLK_A, item % dilation
        qb = lax.shift_right_logical(item, int(math.log2(dilation)))
        return pl.multiple_of(qb * Q_BLK_A, Q_BLK_A), lax.bitwise_and(item, dilation - 1)

    def window(q0):
        jb = j * q_blocks + q0 // Q_BLK_A
        w0 = pl.multiple_of(jnp.clip(jb * Q_BLK_A - RADIUS, 0, sub_len - K_WIN_A), RADIUS)
        variant = jnp.where(jb == 0, 0, jnp.where(jb == nblk - 1, 2, 1))
        return pl.ds(w0, K_WIN_A), variant

    def scores(item, g):
        q0, r = coords(item)
        win, variant = window(q0)
        cols = slice(g * LANES, (g + 1) * LANES)
        qg = q_ref[0, r, pl.ds(q0, Q_BLK_A), cols]
        zero = jnp.zeros_like(qg)
        q2 = jnp.concatenate([jnp.where(low_half, qg, zero), jnp.where(low_half, zero, qg)], axis=0)
        s = lax.dot_general(q2, k_ref[0, r, win, cols], (((1,), (1,)), ((), ())),
                            preferred_element_type=jnp.float32)
        s = s + bias_ref[variant, 2 * g:2 * g + 2].reshape(2 * Q_BLK_A, K_WIN_A)
        m = jnp.max(jnp.maximum(s[:, :LANES], s[:, LANES:]), axis=-1, keepdims=True)
        z_bufs[g % n_buf][...] = s - m
        m_bufs[g % n_buf][...] = jnp.broadcast_to(m, (2 * Q_BLK_A, LANES))

    def attend(item, g):
        q0, r = coords(item)
        win, _ = window(q0)
        cols = slice(g * LANES, (g + 1) * LANES)
        p = jnp.exp2(z_bufs[g % n_buf][...])
        den = jnp.sum(p[:, :LANES] + p[:, LANES:], axis=-1, keepdims=True)
        pv = jnp.dot(p.astype(jnp.bfloat16), v_ref[0, r, win, cols],
                     preferred_element_type=jnp.float32)
        m = m_bufs[g % n_buf][...]
        den_pair = jnp.where(low_half, den[:Q_BLK_A], den[Q_BLK_A:])
        o_pair = jnp.where(low_half, pv[:Q_BLK_A], pv[Q_BLK_A:]) / den_pair
        l_pair = jnp.where(low_half, m[:Q_BLK_A], m[Q_BLK_A:]) + jnp.log2(den_pair)
        return o_pair, l_pair

    def work_item(item, carry):
        q0, r = coords(item)
        item_next = jnp.minimum(item + 1, n_items - 1)
        if dilation == 1:
            rows = pl.ds(q0, Q_BLK_A)
        else:
            rows = pl.ds(q0 * dilation + r, Q_BLK_A, stride=dilation)
        for g in range(n_pairs):
            ahead = g + DIL_LOOKAHEAD
            if ahead < n_pairs:
                scores(item, ahead)
            else:
                scores(item_next, ahead - n_pairs)
            o_pair, l_pair = attend(item, g)
            if final:
                l2, l3 = l2_ref[0, g, rows, :], l3_ref[0, g, rows, :]
                top = jnp.maximum(jnp.maximum(l_pair, l2), l3)
                w1, w2, w3 = jnp.exp2(l_pair - top), jnp.exp2(l2 - top), jnp.exp2(l3 - top)
                o2, o3 = (pltpu.unpack_elementwise(
                    ref[0, g // 2, rows, :], index=g % 2, packed_dtype=jnp.bfloat16,
                    unpacked_dtype=jnp.float32) for ref in (o2_ref, o3_ref))
                merged = (w1 * o_pair + w2 * o2 + w3 * o3) / (w1 + w2 + w3)
                out_ref[0, rows, g * LANES:(g + 1) * LANES] = merged.astype(out_ref.dtype)
            else:
                l_ref[0, g, rows, :] = l_pair
                if g % 2 == 0:
                    o_even = o_pair
                else:
                    o_ref[0, g // 2, rows, :] = pltpu.pack_elementwise(
                        [o_even, o_pair], packed_dtype=jnp.bfloat16)
        return carry

    for g in range(DIL_LOOKAHEAD):
        scores(0, g)
    lax.fori_loop(0, n_items, work_item, 0)


def _dilated_call(q, k, v, bias, q_blocks=1, prev=None):
    batch, dilation, sub_len, _ = q.shape
    seq = sub_len * dilation
    final = prev is not None
    tokens = Q_BLK_A * dilation * q_blocks
    n_pairs = N_HEADS_A // 2
    whole = pl.BlockSpec((1, dilation, sub_len, WIDTH_A), lambda b, j: (b, 0, 0, 0))
    in_specs = [
        pl.BlockSpec((1, dilation, q_blocks * Q_BLK_A, WIDTH_A), lambda b, j: (b, 0, j, 0)),
        whole,
        whole,
        pl.BlockSpec((3, N_HEADS_A, Q_BLK_A, K_WIN_A), lambda b, j: (0, 0, 0, 0),
                     pipeline_mode=pl.Buffered(1)),
    ]
    args = [q, k, v, bias]
    def slab_spec(n):
        return pl.BlockSpec((1, n, tokens, LANES), lambda b, j: (b, 0, j, 0))

    slab_specs = [slab_spec(n_pairs // 2), slab_spec(n_pairs)]
    if final:
        in_specs += slab_specs * 2
        args += list(prev)
        out_specs = pl.BlockSpec((1, tokens, WIDTH_A), lambda b, j: (b, j, 0))
        out_shape = jax.ShapeDtypeStruct((batch, seq, WIDTH_A), jnp.bfloat16)
    else:
        out_specs = slab_specs
        out_shape = [jax.ShapeDtypeStruct((batch, n_pairs // 2, seq, LANES), jnp.uint32),
                     jax.ShapeDtypeStruct((batch, n_pairs, seq, LANES), jnp.float32)]
    return pl.pallas_call(
        functools.partial(_dilated_kernel, dilation=dilation, final=final, q_blocks=q_blocks,
                          n_buf=DIL_BUFFERS),
        grid=(batch, sub_len // (q_blocks * Q_BLK_A)),
        in_specs=in_specs,
        out_specs=out_specs,
        out_shape=out_shape,
        scratch_shapes=(
            [pltpu.VMEM((2 * Q_BLK_A, K_WIN_A), jnp.float32)] * DIL_BUFFERS
            + [pltpu.VMEM((2 * Q_BLK_A, LANES), jnp.float32)] * DIL_BUFFERS),
        compiler_params=pltpu.CompilerParams(
            dimension_semantics=("arbitrary", "arbitrary"),
            vmem_limit_bytes=VMEM_LIMIT_BYTES),
        name=f"dilated_d{dilation}",
    )(*args)


def _t5_buckets(rel):
    nb = N_BUCKETS // 2
    max_exact = nb // 2
    ret = (rel > 0).astype(np.int32) * nb
    n = np.abs(rel)
    large = max_exact + (np.log(np.maximum(n, 1) / max_exact)
                         / np.log(MAX_DISTANCE / max_exact) * (nb - max_exact)).astype(np.int32)
    large = np.minimum(large, nb - 1)
    return (ret + np.where(n < max_exact, n, large)).astype(np.int32)


def _dilated_biases(rel_bias):
    r = np.arange(Q_BLK_A)[:, None]
    c = np.arange(K_WIN_A)[None, :]
    rel = np.stack([c - shift - r for shift in (0, RADIUS, 2 * RADIUS)])
    valid = np.abs(rel) <= RADIUS
    buckets = jnp.asarray(np.stack([_t5_buckets(rel * d) for (_, d) in DILATED_PATTERNS]))
    onehot = (buckets[None] == jnp.arange(N_BUCKETS)[:, None, None, None, None]).astype(jnp.float32)
    b = jnp.einsum('nh,npvrc->pvhrc', rel_bias.astype(jnp.float32), onehot,
                   precision=lax.Precision.HIGHEST)
    tables = jnp.where(jnp.asarray(valid)[None, :, None], b * math.log2(math.e), NEG_INF)
    return [tables[p] for p in range(len(DILATED_PATTERNS))]


def _mla_kernel(q_ref, k_ref, vt_ref, o_ref, s_even, s_odd, p_even, p_odd, m_even, m_odd):
    seq = k_ref.shape[1]
    nk = seq // MLA_TK
    nq = seq // MLA_TQ
    heads = range(2)
    s_bufs, p_bufs, m_bufs = (s_even, s_odd), (p_even, p_odd), (m_even, m_odd)
    assert nq % 2 == 0 and nq >= 4

    def q_rows(t):
        start = t * MLA_TQ
        return pl.ds(start if isinstance(t, int) else pl.multiple_of(start, MLA_TQ), MLA_TQ)

    def keys(c):
        return slice(c * MLA_TK, (c + 1) * MLA_TK)

    def stage(score_t=None, exp_t=None, pv_t=None):
        run_max = [None, None]
        for c in range(nk):
            if score_t is not None:
                t, par = score_t
                for hh in heads:
                    lanes = slice(hh * HEAD_SLOT, (hh + 1) * HEAD_SLOT)
                    s = lax.dot_general(k_ref[0, keys(c), lanes], q_ref[0, q_rows(t), lanes],
                                        (((1,), (1,)), ((), ())),
                                        preferred_element_type=jnp.float32)
                    s_bufs[par][hh, keys(c), :] = s
                    mc = jnp.max(s, axis=0, keepdims=True)
                    run_max[hh] = mc if c == 0 else jnp.maximum(run_max[hh], mc)
            if exp_t is not None:
                _, par = exp_t
                for hh in heads:
                    z = s_bufs[par][hh, keys(c), :] - m_bufs[par][hh]
                    p_bufs[par][hh, keys(c), :] = jnp.exp2(z).astype(jnp.bfloat16)
        if score_t is not None:
            for hh in heads:
                m_bufs[score_t[1]][hh] = run_max[hh]
        if pv_t is not None:
            _, par = pv_t
            n_blk = vt_ref.shape[1]
            outs = []
            for hh in heads:
                vt = jnp.concatenate([vt_ref[0, b, hh * VT_ROWS:(hh + 1) * VT_ROWS, :]
                                      for b in range(n_blk)], axis=1)
                acc = jnp.dot(vt, p_bufs[par][hh], preferred_element_type=jnp.float32)
                outs.append(acc[:V_HEAD_DIM] / acc[VT_ONES_ROW:VT_ONES_ROW + 1])
            o_ref[0, q_rows(pv_t[0]), :] = jnp.concatenate(outs, axis=0).T.astype(o_ref.dtype)

    stage(score_t=(0, 0))
    stage(score_t=(1, 1), exp_t=(0, 0))

    def stage_pair(i, carry):
        t = 2 * i + 1
        stage(score_t=(t + 1, 0), exp_t=(t, 1), pv_t=(t - 1, 0))
        stage(score_t=(t + 2, 1), exp_t=(t + 1, 0), pv_t=(t, 1))
        return carry

    lax.fori_loop(0, nq // 2 - 1, stage_pair, 0)
    stage(exp_t=(nq - 1, 1), pv_t=(nq - 2, 0))
    stage(pv_t=(nq - 1, 1))


def _mla_call(qm, km, vt, seq):
    batch, n_blk, _, blk_keys = vt.shape
    assert blk_keys % MLA_TK == 0, "a key chunk must not straddle two V^T blocks"
    pair = lambda b, g: (b, 0, g)
    return pl.pallas_call(
        _mla_kernel,
        grid=(batch, N_HEADS_B // 2),
        in_specs=[
            pl.BlockSpec((1, seq, 2 * HEAD_SLOT), pair),
            pl.BlockSpec((1, seq, 2 * HEAD_SLOT), pair),
            pl.BlockSpec((1, n_blk, 2 * VT_ROWS, blk_keys), lambda b, g: (b, 0, g, 0)),
        ],
        out_specs=pl.BlockSpec((1, seq, 2 * V_HEAD_DIM), pair),
        out_shape=jax.ShapeDtypeStruct((batch, seq, WIDTH_B), jnp.bfloat16),
        scratch_shapes=(
            [pltpu.VMEM((2, seq, MLA_TQ), jnp.float32)] * 2
            + [pltpu.VMEM((2, seq, MLA_TQ), jnp.bfloat16)] * 2
            + [pltpu.VMEM((2, 1, MLA_TQ), jnp.float32)] * 2),
        compiler_params=pltpu.CompilerParams(
            dimension_semantics=("arbitrary", "arbitrary"),
            vmem_limit_bytes=VMEM_LIMIT_BYTES),
        name="mla",
    )(qm, km, vt)


def _mlp_kernel(x_ref, oa_ref, ob_ref, wout_ref, g_ref, wup_ref, wdown_ref, gf_ref, o_ref):
    o_cat = jnp.concatenate([oa_ref[...], ob_ref[...]], axis=-1)
    h = x_ref[...] + jnp.dot(o_cat, wout_ref[...], preferred_element_type=jnp.float32)
    u = _rms(h, g_ref[...]).astype(jnp.bfloat16)
    acc = h
    for c in range(D_FF // FF_CHUNK):
        a = jnp.dot(u, wup_ref[:, c * FF_CHUNK:(c + 1) * FF_CHUNK],
                    preferred_element_type=jnp.float32)
        a = jnp.square(jnp.maximum(a, 0.0)).astype(jnp.bfloat16)
        acc = acc + jnp.dot(a, wdown_ref[c * FF_CHUNK:(c + 1) * FF_CHUNK, :],
                            preferred_element_type=jnp.float32)
    o_ref[...] = _rms(acc, gf_ref[...])


def _mlp_call(x2, oa, ob, wout, g, wup, wdown, gf):
    tokens = x2.shape[0]
    tm = MLP_TM
    row = lambda i: (i, 0)
    const = lambda i: (0, 0)

    def wspec(shape):
        return pl.BlockSpec(shape, const, pipeline_mode=pl.Buffered(1))

    return pl.pallas_call(
        _mlp_kernel,
        grid=(tokens // tm,),
        in_specs=[
            pl.BlockSpec((tm, D_MODEL), row),
            pl.BlockSpec((tm, WIDTH_A), row),
            pl.BlockSpec((tm, WIDTH_B), row),
            wspec((WIDTH_A + WIDTH_B, D_MODEL)),
            wspec((1, D_MODEL)),
            wspec((D_MODEL, D_FF)),
            wspec((D_FF, D_MODEL)),
            wspec((1, D_MODEL)),
        ],
        out_specs=pl.BlockSpec((tm, D_MODEL), row),
        out_shape=jax.ShapeDtypeStruct((tokens, D_MODEL), jnp.float32),
        compiler_params=pltpu.CompilerParams(
            dimension_semantics=("arbitrary",), vmem_limit_bytes=VMEM_LIMIT_BYTES),
        name="mlp",
    )(x2, oa, ob, wout, g, wup, wdown, gf)


def _prep_in_weights(w_in):
    pad = jnp.zeros((D_MODEL, IN_COLS_PAD - w_in.shape[1]), w_in.dtype)
    return jnp.concatenate([w_in, pad], axis=1).astype(jnp.bfloat16)


def _prep_q_weights(w_q_b):
    dqk = QK_NOPE_DIM + QK_ROPE_DIM
    half = QK_ROPE_DIM // 2
    w = w_q_b.reshape(Q_LORA_RANK, N_HEADS_B, dqk)
    nope, rope = w[..., :QK_NOPE_DIM], w[..., QK_NOPE_DIM:]
    slot = jnp.concatenate([nope, rope, -rope[..., half:], rope[..., :half]], axis=-1)
    return slot.reshape(Q_LORA_RANK, QK_WIDTH_B).astype(jnp.bfloat16)


def _prep_kv_weights(w_kv_b):
    half = QK_ROPE_DIM // 2
    w = w_kv_b.reshape(KV_LORA_RANK, N_HEADS_B, QK_NOPE_DIM + V_HEAD_DIM)
    k_nope, v = w[..., :QK_NOPE_DIM], w[..., QK_NOPE_DIM:]
    eye = np.eye(QK_ROPE_DIM, dtype=np.float32)
    rot = np.zeros((QK_ROPE_DIM, QK_ROPE_DIM), np.float32)
    for jcol in range(half):
        rot[half + jcol, jcol] = -1.0
        rot[jcol, half + jcol] = 1.0

    nope_cols = jnp.concatenate(
        [k_nope, jnp.zeros((KV_LORA_RANK, N_HEADS_B, HEAD_SLOT - QK_NOPE_DIM), k_nope.dtype)], axis=-1)
    nope_cols = jnp.concatenate(
        [nope_cols.reshape(KV_LORA_RANK, QK_WIDTH_B),
         jnp.zeros((LAT_WIDTH - KV_LORA_RANK, QK_WIDTH_B), k_nope.dtype)], axis=0)
    rope_cols = np.zeros((LAT_WIDTH, 2, HEAD_SLOT), np.float32)
    for which, rope_block in enumerate((eye, rot)):
        for lo in (QK_NOPE_DIM, QK_NOPE_DIM + QK_ROPE_DIM):
            rope_cols[KV_LORA_RANK:KV_LORA_RANK + QK_ROPE_DIM, which, lo:lo + QK_ROPE_DIM] = rope_block
    wk = jnp.concatenate([nope_cols, jnp.asarray(rope_cols.reshape(LAT_WIDTH, 2 * HEAD_SLOT))], axis=1)

    vt_top = jnp.transpose(v, (1, 2, 0))
    vt_top = jnp.concatenate(
        [vt_top, jnp.zeros((N_HEADS_B, VT_ROWS - V_HEAD_DIM, KV_LORA_RANK), v.dtype)], axis=1)
    ones_sel = np.zeros((N_HEADS_B, VT_ROWS, LAT_WIDTH - KV_LORA_RANK), np.float32)
    ones_sel[:, VT_ONES_ROW, -1] = 1.0
    wvt = jnp.concatenate([vt_top, jnp.asarray(ones_sel)], axis=2)
    return wk.astype(jnp.bfloat16), wvt.reshape(N_HEADS_B * VT_ROWS, LAT_WIDTH).astype(jnp.bfloat16)


def _rope_lane_tables(seq):
    inv_freq = ROPE_THETA ** (-np.arange(0, QK_ROPE_DIM, 2, dtype=np.float64) / QK_ROPE_DIM)
    freqs = np.arange(seq, dtype=np.float64)[:, None] * inv_freq[None, :]
    cos, sin = np.cos(freqs), np.sin(freqs)
    cos2, sin2 = np.concatenate([cos, cos], axis=1), np.concatenate([sin, sin], axis=1)
    ones = np.ones((seq, QK_NOPE_DIM))
    q_scale = (QK_NOPE_DIM + QK_ROPE_DIM) ** -0.5 * math.log2(math.e)
    q_table = np.concatenate([ones, cos2, sin2], axis=1) * q_scale
    k_cos = np.concatenate([ones, cos2, cos2], axis=1)
    k_sin = np.concatenate([np.zeros_like(ones), sin2, sin2], axis=1)
    return tuple(jnp.asarray(t, jnp.float32) for t in (q_table, k_cos, k_sin))


def kernel(x, mix_norm_g, w_in, q_norm_g, w_q_b, kv_norm_g, w_kv_b, w_out,
           mlp_norm_g, w_up, w_down, rel_bias, final_norm_g):
    batch, seq, _ = x.shape
    depth = w_in.shape[0]
    assert depth == 1, "the final norm is fused into the single layer's MLP kernel"
    tq, ck, sk = _rope_lane_tables(seq)
    biases = _dilated_biases(rel_bias)
    layer = 0
    x2 = x.reshape(batch * seq, D_MODEL)
    wk, wvt = _prep_kv_weights(w_kv_b[layer])
    qa, ka, va, qa4, ka4, va4, qa16, ka16, va16, qm, km, vt = _proj_call(
        x2, mix_norm_g[layer][None], _prep_in_weights(w_in[layer]),
        q_norm_g[layer][None], _prep_q_weights(w_q_b[layer]),
        kv_norm_g[layer][None], wk, wvt, tq, ck, sk, batch, seq)

    shape_a = (batch, 1, seq, WIDTH_A)
    q_blocks = dict(zip([d for (_, d) in DILATED_PATTERNS], DIL_Q_BLOCKS))
    o16, l16 = _dilated_call(qa16, ka16, va16, biases[2], q_blocks=q_blocks[16])
    o4, l4 = _dilated_call(qa4, ka4, va4, biases[1], q_blocks=q_blocks[4])
    oa = _dilated_call(qa.reshape(shape_a), ka.reshape(shape_a), va.reshape(shape_a), biases[0],
                       q_blocks=q_blocks[1], prev=(o4, l4, o16, l16))

    ob = _mla_call(qm.reshape(batch, seq, QK_WIDTH_B), km.reshape(batch, seq, QK_WIDTH_B), vt, seq)

    out = _mlp_call(
        x2, oa.reshape(batch * seq, WIDTH_A), ob.reshape(batch * seq, WIDTH_B),
        w_out[layer].astype(jnp.bfloat16), mlp_norm_g[layer][None],
        w_up[layer].astype(jnp.bfloat16), w_down[layer].astype(jnp.bfloat16),
        final_norm_g[None])
    return out.reshape(batch, seq, D_MODEL)
```

```python
import functools
import math

import jax
import jax.numpy as jnp
import numpy as np
from jax import lax
from jax.experimental import pallas as pl
from jax.experimental.pallas import tpu as pltpu

D_MODEL = 1024
HEAD_DIM = 64
N_HEADS_A = 8
DILATED_PATTERNS = ((128, 1), (512, 4), (2048, 16))
N_HEADS_B = 8
Q_LORA_RANK = 256
KV_LORA_RANK = 128
QK_NOPE_DIM = 64
QK_ROPE_DIM = 32
V_HEAD_DIM = 64
ROPE_THETA = 10000.0
N_BUCKETS = 32
MAX_DISTANCE = 1024
D_FF = 4 * D_MODEL
NORM_EPS = 1e-6
NEG_INF = -1e30
WIDTH_A = N_HEADS_A * HEAD_DIM
WIDTH_B = N_HEADS_B * V_HEAD_DIM

LANES = 128
VMEM_LIMIT_BYTES = 56 * 1024 * 1024

HEAD_SLOT = LANES
QK_WIDTH_B = N_HEADS_B * HEAD_SLOT
VT_ROWS = 80
VT_ONES_ROW = V_HEAD_DIM
IN_COLS_PAD = 2048
CQ_OFF = 3 * WIDTH_A
LAT_WIDTH = 2 * LANES
RADIUS = 64
Q_BLK_A = 2 * RADIUS
K_WIN_A = 4 * RADIUS
BIAS_SPAN = 2 * K_WIN_A

PROJ_TM = 512
MLP_TM = 1024
MLA_TQ = 256
DIL_LOOKAHEAD = 3
DIL_BUFFERS = 4
DIL_Q_BLOCKS = (8, 4, 1)
MLA_TK = 512
FF_CHUNK = 1024


def _rms(xf, g):
    return xf * lax.rsqrt(jnp.mean(xf * xf, axis=-1, keepdims=True) + NORM_EPS) * g


def _proj_kernel(x_ref, g_ref, win_ref, qg_ref, wq_ref, kvg_ref, wk_ref, wvt_ref,
                 tq_ref, ck_ref, sk_ref,
                 qa_ref, ka_ref, va_ref, qa4_ref, ka4_ref, va4_ref, qa16_ref, ka16_ref, va16_ref,
                 qm_ref, km_ref, vt_ref, slab_scr, part_scr):
    x = x_ref[...]
    u = _rms(x, g_ref[...]).astype(jnp.bfloat16)

    def in_proj(lo, hi):
        return jnp.dot(u, win_ref[:, lo:hi], preferred_element_type=jnp.float32)

    tm = x.shape[0]
    n_slabs = WIDTH_A // LANES
    scales = (HEAD_DIM ** -0.5 * math.log2(math.e), 1.0, 1.0)
    groups = ((qa_ref, qa4_ref, qa16_ref), (ka_ref, ka4_ref, ka16_ref), (va_ref, va4_ref, va16_ref))

    def dilated_group(a):
        nat_ref, *strided_refs = groups[a]
        t = in_proj(a * WIDTH_A, (a + 1) * WIDTH_A) * scales[a]
        nat_ref[...] = t.astype(jnp.bfloat16)
        ref4, ref16 = strided_refs
        d4, d16 = ref4.shape[1], ref16.shape[1]
        assert d16 == d4 * d4, "the second copy is a stride-d4 pass over the first"
        for g in range(n_slabs):
            cols = slice(g * LANES, (g + 1) * LANES)
            slab_scr[a * n_slabs + g] = t[:, cols]
            for r in range(d4):
                part = slab_scr[a * n_slabs + g, pl.ds(r, tm // d4, stride=d4), :]
                ref4[0, r, :, cols] = part.astype(jnp.bfloat16)
                part_scr[a * n_slabs + g, r] = part
            for r in range(d16):
                part = part_scr[a * n_slabs + g, r % d4, pl.ds(r // d4, tm // d16, stride=d4), :]
                ref16[0, r, :, cols] = part.astype(jnp.bfloat16)

    dilated_group(0)
    proj = in_proj(CQ_OFF, IN_COLS_PAD)
    dilated_group(1)

    cq = _rms(proj[:, :Q_LORA_RANK], qg_ref[...]).astype(jnp.bfloat16)
    q2 = jnp.dot(cq, wq_ref[...], preferred_element_type=jnp.float32)
    qm_ref[...] = (q2 * jnp.tile(tq_ref[...], (1, N_HEADS_B))).astype(jnp.bfloat16)
    dilated_group(2)

    ckv = _rms(proj[:, Q_LORA_RANK:Q_LORA_RANK + KV_LORA_RANK], kvg_ref[...])
    hi = proj[:, Q_LORA_RANK + KV_LORA_RANK:]
    lane = lax.broadcasted_iota(jnp.int32, hi.shape, 1)
    hi = jnp.where(lane == LANES - 1, 1.0, hi)
    lat = jnp.concatenate([ckv, hi], axis=-1).astype(jnp.bfloat16)
    k2 = jnp.dot(lat, wk_ref[...], preferred_element_type=jnp.float32)
    rope = (k2[:, QK_WIDTH_B:QK_WIDTH_B + HEAD_SLOT] * ck_ref[...]
            + k2[:, QK_WIDTH_B + HEAD_SLOT:] * sk_ref[...])
    nope_lane = lax.broadcasted_iota(jnp.int32, rope.shape, 1) < QK_NOPE_DIM
    for h in range(N_HEADS_B):
        slot = slice(h * HEAD_SLOT, (h + 1) * HEAD_SLOT)
        km_ref[:, slot] = jnp.where(nope_lane, k2[:, slot], rope).astype(jnp.bfloat16)
    vt = lax.dot_general(wvt_ref[...], lat, (((1,), (1,)), ((), ())),
                         preferred_element_type=jnp.float32)
    vt_ref[0, 0] = vt.astype(jnp.bfloat16)


def _proj_call(x2, g, win, qg, wq, kvg, wk, wvt, tq, ck, sk, batch, seq):
    tokens = x2.shape[0]
    tm = PROJ_TM
    sblk = seq // tm
    row = lambda i: (i, 0)
    const = lambda i: (0, 0)
    pos = lambda i: (i % sblk, 0)

    def wspec(shape):
        return pl.BlockSpec(shape, const, pipeline_mode=pl.Buffered(1))

    strides = [d for (_, d) in DILATED_PATTERNS if d > 1]

    def strided_spec(d):
        return pl.BlockSpec((1, d, tm // d, WIDTH_A), lambda i: (i // sblk, 0, i % sblk, 0))

    bf = jnp.bfloat16
    return pl.pallas_call(
        _proj_kernel,
        grid=(tokens // tm,),
        in_specs=[
            pl.BlockSpec((tm, D_MODEL), row),
            wspec((1, D_MODEL)),
            wspec((D_MODEL, IN_COLS_PAD)),
            wspec((1, Q_LORA_RANK)),
            wspec((Q_LORA_RANK, QK_WIDTH_B)),
            wspec((1, KV_LORA_RANK)),
            wspec((LAT_WIDTH, QK_WIDTH_B + 2 * HEAD_SLOT)),
            wspec((N_HEADS_B * VT_ROWS, LAT_WIDTH)),
            pl.BlockSpec((tm, HEAD_SLOT), pos),
            pl.BlockSpec((tm, HEAD_SLOT), pos),
            pl.BlockSpec((tm, HEAD_SLOT), pos),
        ],
        out_specs=[
            pl.BlockSpec((tm, WIDTH_A), row),
            pl.BlockSpec((tm, WIDTH_A), row),
            pl.BlockSpec((tm, WIDTH_A), row),
            *[strided_spec(d) for d in strides for _ in range(3)],
            pl.BlockSpec((tm, QK_WIDTH_B), row),
            pl.BlockSpec((tm, QK_WIDTH_B), row),
            pl.BlockSpec((1, 1, N_HEADS_B * VT_ROWS, tm), lambda i: (i // sblk, i % sblk, 0, 0)),
        ],
        out_shape=[
            jax.ShapeDtypeStruct((tokens, WIDTH_A), bf),
            jax.ShapeDtypeStruct((tokens, WIDTH_A), bf),
            jax.ShapeDtypeStruct((tokens, WIDTH_A), bf),
            *[jax.ShapeDtypeStruct((batch, d, seq // d, WIDTH_A), bf) for d in strides for _ in range(3)],
            jax.ShapeDtypeStruct((tokens, QK_WIDTH_B), bf),
            jax.ShapeDtypeStruct((tokens, QK_WIDTH_B), bf),
            jax.ShapeDtypeStruct((batch, sblk, N_HEADS_B * VT_ROWS, tm), bf),
        ],
        scratch_shapes=[
            pltpu.VMEM((3 * WIDTH_A // LANES, tm, LANES), jnp.float32),
            pltpu.VMEM((3 * WIDTH_A // LANES, strides[0], tm // strides[0], LANES), jnp.float32),
        ],
        compiler_params=pltpu.CompilerParams(
            dimension_semantics=("arbitrary",), vmem_limit_bytes=VMEM_LIMIT_BYTES),
        name="proj",
    )(x2, g, win, qg, wq, kvg, wk, wvt, tq, ck, sk)


def _dilated_kernel(*refs, dilation, final, q_blocks, n_buf):
    if final:
        q_ref, k_ref, v_ref, bias_ref, o2_ref, l2_ref, o3_ref, l3_ref, out_ref = refs[:9]
    else:
        q_ref, k_ref, v_ref, bias_ref, o_ref, l_ref = refs[:6]
    bias_tab = refs[-2 * n_buf - 1]
    scratch = refs[-2 * n_buf:]
    z_bufs, m_bufs = scratch[:n_buf], scratch[n_buf:]
    n_pairs = N_HEADS_A // 2
    assert n_pairs % n_buf == 0 and DIL_LOOKAHEAD < n_buf, "buffer rotation must restart per item"
    assert dilation & (dilation - 1) == 0, "work items are decoded with shifts"
    j = pl.program_id(1)
    sub_len = k_ref.shape[2]
    nblk = sub_len // Q_BLK_A
    n_items = dilation * q_blocks
    lane = lax.broadcasted_iota(jnp.int32, (Q_BLK_A, LANES), 1)
    low_half = lane < HEAD_DIM

    @pl.when((pl.program_id(0) == 0) & (j == 0))
    def _():
        for h in range(N_HEADS_A):
            base = jnp.broadcast_to(bias_ref[h:h + 1, :], (Q_BLK_A, BIAS_SPAN))
            for variant in range(3):
                skew = pltpu.roll(base, variant * RADIUS, 1, stride=1, stride_axis=0)
                bias_tab[variant, h] = skew[:, :K_WIN_A]

    def coords(item):
        if isinstance(item, int):
            return (item // dilation) * Q_BLK_A, item % dilation
        qb = lax.shift_right_logical(item, int(math.log2(dilation)))
        return pl.multiple_of(qb * Q_BLK_A, Q_BLK_A), lax.bitwise_and(item, dilation - 1)

    def window(q0):
        jb = j * q_blocks + q0 // Q_BLK_A
        w0 = pl.multiple_of(jnp.clip(jb * Q_BLK_A - RADIUS, 0, sub_len - K_WIN_A), RADIUS)
        variant = jnp.where(jb == 0, 0, jnp.where(jb == nblk - 1, 2, 1))
        return pl.ds(w0, K_WIN_A), variant

    def scores(item, g):
        q0, r = coords(item)
        win, variant = window(q0)
        cols = slice(g * LANES, (g + 1) * LANES)
        qg = q_ref[0, r, pl.ds(q0, Q_BLK_A), cols]
        zero = jnp.zeros_like(qg)
        q2 = jnp.concatenate([jnp.where(low_half, qg, zero), jnp.where(low_half, zero, qg)], axis=0)
        s = lax.dot_general(q2, k_ref[0, r, win, cols], (((1,), (1,)), ((), ())),
                            preferred_element_type=jnp.float32)
        s = s + bias_tab[variant, 2 * g:2 * g + 2].reshape(2 * Q_BLK_A, K_WIN_A)
        m = jnp.max(jnp.maximum(s[:, :LANES], s[:, LANES:]), axis=-1, keepdims=True)
        z_bufs[g % n_buf][...] = s - m
        m_bufs[g % n_buf][...] = jnp.broadcast_to(m, (2 * Q_BLK_A, LANES))

    def attend(item, g):
        q0, r = coords(item)
        win, _ = window(q0)
        cols = slice(g * LANES, (g + 1) * LANES)
        p = jnp.exp2(z_bufs[g % n_buf][...])
        den = jnp.sum(p[:, :LANES] + p[:, LANES:], axis=-1, keepdims=True)
        pv = jnp.dot(p.astype(jnp.bfloat16), v_ref[0, r, win, cols],
                     preferred_element_type=jnp.float32)
        m = m_bufs[g % n_buf][...]
        den_pair = jnp.where(low_half, den[:Q_BLK_A], den[Q_BLK_A:])
        o_pair = jnp.where(low_half, pv[:Q_BLK_A], pv[Q_BLK_A:]) / den_pair
        l_pair = jnp.where(low_half, m[:Q_BLK_A], m[Q_BLK_A:]) + jnp.log2(den_pair)
        return o_pair, l_pair

    def work_item(item, carry):
        q0, r = coords(item)
        item_next = jnp.minimum(item + 1, n_items - 1)
        if dilation == 1:
            rows = pl.ds(q0, Q_BLK_A)
        else:
            rows = pl.ds(q0 * dilation + r, Q_BLK_A, stride=dilation)
        for g in range(n_pairs):
            ahead = g + DIL_LOOKAHEAD
            if ahead < n_pairs:
                scores(item, ahead)
            else:
                scores(item_next, ahead - n_pairs)
            o_pair, l_pair = attend(item, g)
            if final:
                l2, l3 = l2_ref[0, g, rows, :], l3_ref[0, g, rows, :]
                top = jnp.maximum(jnp.maximum(l_pair, l2), l3)
                w1, w2, w3 = jnp.exp2(l_pair - top), jnp.exp2(l2 - top), jnp.exp2(l3 - top)
                o2, o3 = (pltpu.unpack_elementwise(
                    ref[0, g // 2, rows, :], index=g % 2, packed_dtype=jnp.bfloat16,
                    unpacked_dtype=jnp.float32) for ref in (o2_ref, o3_ref))
                merged = (w1 * o_pair + w2 * o2 + w3 * o3) / (w1 + w2 + w3)
                out_ref[0, rows, g * LANES:(g + 1) * LANES] = merged.astype(out_ref.dtype)
            else:
                l_ref[0, g, rows, :] = l_pair
                if g % 2 == 0:
                    o_even = o_pair
                else:
                    o_ref[0, g // 2, rows, :] = pltpu.pack_elementwise(
                        [o_even, o_pair], packed_dtype=jnp.bfloat16)
        return carry

    for g in range(DIL_LOOKAHEAD):
        scores(0, g)
    lax.fori_loop(0, n_items, work_item, 0)


def _dilated_call(q, k, v, bias, q_blocks=1, prev=None):
    batch, dilation, sub_len, _ = q.shape
    seq = sub_len * dilation
    final = prev is not None
    tokens = Q_BLK_A * dilation * q_blocks
    n_pairs = N_HEADS_A // 2
    whole = pl.BlockSpec((1, dilation, sub_len, WIDTH_A), lambda b, j: (b, 0, 0, 0))
    in_specs = [
        pl.BlockSpec((1, dilation, q_blocks * Q_BLK_A, WIDTH_A), lambda b, j: (b, 0, j, 0)),
        whole,
        whole,
        pl.BlockSpec((N_HEADS_A, BIAS_SPAN), lambda b, j: (0, 0), pipeline_mode=pl.Buffered(1)),
    ]
    args = [q, k, v, bias]
    def slab_spec(n):
        return pl.BlockSpec((1, n, tokens, LANES), lambda b, j: (b, 0, j, 0))

    slab_specs = [slab_spec(n_pairs // 2), slab_spec(n_pairs)]
    if final:
        in_specs += slab_specs * 2
        args += list(prev)
        out_specs = pl.BlockSpec((1, tokens, WIDTH_A), lambda b, j: (b, j, 0))
        out_shape = jax.ShapeDtypeStruct((batch, seq, WIDTH_A), jnp.bfloat16)
    else:
        out_specs = slab_specs
        out_shape = [jax.ShapeDtypeStruct((batch, n_pairs // 2, seq, LANES), jnp.uint32),
                     jax.ShapeDtypeStruct((batch, n_pairs, seq, LANES), jnp.float32)]
    return pl.pallas_call(
        functools.partial(_dilated_kernel, dilation=dilation, final=final, q_blocks=q_blocks,
                          n_buf=DIL_BUFFERS),
        grid=(batch, sub_len // (q_blocks * Q_BLK_A)),
        in_specs=in_specs,
        out_specs=out_specs,
        out_shape=out_shape,
        scratch_shapes=(
            [pltpu.VMEM((3, N_HEADS_A, Q_BLK_A, K_WIN_A), jnp.float32)]
            + [pltpu.VMEM((2 * Q_BLK_A, K_WIN_A), jnp.float32)] * DIL_BUFFERS
            + [pltpu.VMEM((2 * Q_BLK_A, LANES), jnp.float32)] * DIL_BUFFERS),
        compiler_params=pltpu.CompilerParams(
            dimension_semantics=("arbitrary", "arbitrary"),
            vmem_limit_bytes=VMEM_LIMIT_BYTES),
        name=f"dilated_d{dilation}",
    )(*args)


def _t5_buckets(rel):
    nb = N_BUCKETS // 2
    max_exact = nb // 2
    ret = (rel > 0).astype(np.int32) * nb
    n = np.abs(rel)
    large = max_exact + (np.log(np.maximum(n, 1) / max_exact)
                         / np.log(MAX_DISTANCE / max_exact) * (nb - max_exact)).astype(np.int32)
    large = np.minimum(large, nb - 1)
    return (ret + np.where(n < max_exact, n, large)).astype(np.int32)


def _dilated_biases(rel_bias):
    idx = np.arange(BIAS_SPAN)
    rel = np.where(idx < BIAS_SPAN // 2, idx, idx - BIAS_SPAN)
    valid = np.abs(rel) <= RADIUS
    buckets = np.stack([_t5_buckets(rel * d) for (_, d) in DILATED_PATTERNS])
    onehot = jnp.asarray((buckets[None] == np.arange(N_BUCKETS)[:, None, None]).astype(np.float32))
    b = jnp.einsum('nh,npj->phj', rel_bias.astype(jnp.float32), onehot,
                   precision=lax.Precision.HIGHEST)
    vectors = jnp.where(jnp.asarray(valid)[None, None, :], b * math.log2(math.e), NEG_INF)
    return [vectors[p] for p in range(len(DILATED_PATTERNS))]


def _mla_kernel(q_ref, k_ref, vt_ref, o_ref, s_even, s_odd, p_even, p_odd, m_even, m_odd):
    seq = k_ref.shape[1]
    nk = seq // MLA_TK
    nq = seq // MLA_TQ
    heads = range(2)
    s_bufs, p_bufs, m_bufs = (s_even, s_odd), (p_even, p_odd), (m_even, m_odd)
    assert nq % 2 == 0 and nq >= 4

    def q_rows(t):
        start = t * MLA_TQ
        return pl.ds(start if isinstance(t, int) else pl.multiple_of(start, MLA_TQ), MLA_TQ)

    def keys(c):
        return slice(c * MLA_TK, (c + 1) * MLA_TK)

    def stage(score_t=None, exp_t=None, pv_t=None):
        run_max = [None, None]
        for c in range(nk):
            if score_t is not None:
                t, par = score_t
                for hh in heads:
                    lanes = slice(hh * HEAD_SLOT, (hh + 1) * HEAD_SLOT)
                    s = lax.dot_general(k_ref[0, keys(c), lanes], q_ref[0, q_rows(t), lanes],
                                        (((1,), (1,)), ((), ())),
                                        preferred_element_type=jnp.float32)
                    s_bufs[par][hh, keys(c), :] = s
                    mc = jnp.max(s, axis=0, keepdims=True)
                    run_max[hh] = mc if c == 0 else jnp.maximum(run_max[hh], mc)
            if exp_t is not None:
                _, par = exp_t
                for hh in heads:
                    z = s_bufs[par][hh, keys(c), :] - m_bufs[par][hh]
                    p_bufs[par][hh, keys(c), :] = jnp.exp2(z).astype(jnp.bfloat16)
        if score_t is not None:
            for hh in heads:
                m_bufs[score_t[1]][hh] = run_max[hh]
        if pv_t is not None:
            _, par = pv_t
            n_blk = vt_ref.shape[1]
            outs = []
            for hh in heads:
                vt = jnp.concatenate([vt_ref[0, b, hh * VT_ROWS:(hh + 1) * VT_ROWS, :]
                                      for b in range(n_blk)], axis=1)
                acc = jnp.dot(vt, p_bufs[par][hh], preferred_element_type=jnp.float32)
                outs.append(acc[:V_HEAD_DIM] / acc[VT_ONES_ROW:VT_ONES_ROW + 1])
            o_ref[0, q_rows(pv_t[0]), :] = jnp.concatenate(outs, axis=0).T.astype(o_ref.dtype)

    stage(score_t=(0, 0))
    stage(score_t=(1, 1), exp_t=(0, 0))

    def stage_pair(i, carry):
        t = 2 * i + 1
        stage(score_t=(t + 1, 0), exp_t=(t, 1), pv_t=(t - 1, 0))
        stage(score_t=(t + 2, 1), exp_t=(t + 1, 0), pv_t=(t, 1))
        return carry

    lax.fori_loop(0, nq // 2 - 1, stage_pair, 0)
    stage(exp_t=(nq - 1, 1), pv_t=(nq - 2, 0))
    stage(pv_t=(nq - 1, 1))


def _mla_call(qm, km, vt, seq):
    batch, n_blk, _, blk_keys = vt.shape
    assert blk_keys % MLA_TK == 0, "a key chunk must not straddle two V^T blocks"
    pair = lambda b, g: (b, 0, g)
    return pl.pallas_call(
        _mla_kernel,
        grid=(batch, N_HEADS_B // 2),
        in_specs=[
            pl.BlockSpec((1, seq, 2 * HEAD_SLOT), pair),
            pl.BlockSpec((1, seq, 2 * HEAD_SLOT), pair),
            pl.BlockSpec((1, n_blk, 2 * VT_ROWS, blk_keys), lambda b, g: (b, 0, g, 0)),
        ],
        out_specs=pl.BlockSpec((1, seq, 2 * V_HEAD_DIM), pair),
        out_shape=jax.ShapeDtypeStruct((batch, seq, WIDTH_B), jnp.bfloat16),
        scratch_shapes=(
            [pltpu.VMEM((2, seq, MLA_TQ), jnp.float32)] * 2
            + [pltpu.VMEM((2, seq, MLA_TQ), jnp.bfloat16)] * 2
            + [pltpu.VMEM((2, 1, MLA_TQ), jnp.float32)] * 2),
        compiler_params=pltpu.CompilerParams(
            dimension_semantics=("arbitrary", "arbitrary"),
            vmem_limit_bytes=VMEM_LIMIT_BYTES),
        name="mla",
    )(qm, km, vt)


def _mlp_kernel(x_ref, oa_ref, ob_ref, wout_ref, g_ref, wup_ref, wdown_ref, gf_ref, o_ref):
    o_cat = jnp.concatenate([oa_ref[...], ob_ref[...]], axis=-1)
    h = x_ref[...] + jnp.dot(o_cat, wout_ref[...], preferred_element_type=jnp.float32)
    u = _rms(h, g_ref[...]).astype(jnp.bfloat16)
    acc = h
    for c in range(D_FF // FF_CHUNK):
        a = jnp.dot(u, wup_ref[:, c * FF_CHUNK:(c + 1) * FF_CHUNK],
                    preferred_element_type=jnp.float32)
        a = jnp.square(jnp.maximum(a, 0.0)).astype(jnp.bfloat16)
        acc = acc + jnp.dot(a, wdown_ref[c * FF_CHUNK:(c + 1) * FF_CHUNK, :],
                            preferred_element_type=jnp.float32)
    o_ref[...] = _rms(acc, gf_ref[...])


def _mlp_call(x2, oa, ob, wout, g, wup, wdown, gf):
    tokens = x2.shape[0]
    tm = MLP_TM
    row = lambda i: (i, 0)
    const = lambda i: (0, 0)

    def wspec(shape):
        return pl.BlockSpec(shape, const, pipeline_mode=pl.Buffered(1))

    return pl.pallas_call(
        _mlp_kernel,
        grid=(tokens // tm,),
        in_specs=[
            pl.BlockSpec((tm, D_MODEL), row),
            pl.BlockSpec((tm, WIDTH_A), row),
            pl.BlockSpec((tm, WIDTH_B), row),
            wspec((WIDTH_A + WIDTH_B, D_MODEL)),
            wspec((1, D_MODEL)),
            wspec((D_MODEL, D_FF)),
            wspec((D_FF, D_MODEL)),
            wspec((1, D_MODEL)),
        ],
        out_specs=pl.BlockSpec((tm, D_MODEL), row),
        out_shape=jax.ShapeDtypeStruct((tokens, D_MODEL), jnp.float32),
        compiler_params=pltpu.CompilerParams(
            dimension_semantics=("arbitrary",), vmem_limit_bytes=VMEM_LIMIT_BYTES),
        name="mlp",
    )(x2, oa, ob, wout, g, wup, wdown, gf)


def _prep_in_weights(w_in):
    pad = jnp.zeros((D_MODEL, IN_COLS_PAD - w_in.shape[1]), w_in.dtype)
    return jnp.concatenate([w_in, pad], axis=1).astype(jnp.bfloat16)


def _prep_q_weights(w_q_b):
    dqk = QK_NOPE_DIM + QK_ROPE_DIM
    half = QK_ROPE_DIM // 2
    w = w_q_b.reshape(Q_LORA_RANK, N_HEADS_B, dqk)
    nope, rope = w[..., :QK_NOPE_DIM], w[..., QK_NOPE_DIM:]
    slot = jnp.concatenate([nope, rope, -rope[..., half:], rope[..., :half]], axis=-1)
    return slot.reshape(Q_LORA_RANK, QK_WIDTH_B).astype(jnp.bfloat16)


def _prep_kv_weights(w_kv_b):
    half = QK_ROPE_DIM // 2
    w = w_kv_b.reshape(KV_LORA_RANK, N_HEADS_B, QK_NOPE_DIM + V_HEAD_DIM)
    k_nope, v = w[..., :QK_NOPE_DIM], w[..., QK_NOPE_DIM:]
    eye = np.eye(QK_ROPE_DIM, dtype=np.float32)
    rot = np.zeros((QK_ROPE_DIM, QK_ROPE_DIM), np.float32)
    for jcol in range(half):
        rot[half + jcol, jcol] = -1.0
        rot[jcol, half + jcol] = 1.0

    nope_cols = jnp.concatenate(
        [k_nope, jnp.zeros((KV_LORA_RANK, N_HEADS_B, HEAD_SLOT - QK_NOPE_DIM), k_nope.dtype)], axis=-1)
    nope_cols = jnp.concatenate(
        [nope_cols.reshape(KV_LORA_RANK, QK_WIDTH_B),
         jnp.zeros((LAT_WIDTH - KV_LORA_RANK, QK_WIDTH_B), k_nope.dtype)], axis=0)
    rope_cols = np.zeros((LAT_WIDTH, 2, HEAD_SLOT), np.float32)
    for which, rope_block in enumerate((eye, rot)):
        for lo in (QK_NOPE_DIM, QK_NOPE_DIM + QK_ROPE_DIM):
            rope_cols[KV_LORA_RANK:KV_LORA_RANK + QK_ROPE_DIM, which, lo:lo + QK_ROPE_DIM] = rope_block
    wk = jnp.concatenate([nope_cols, jnp.asarray(rope_cols.reshape(LAT_WIDTH, 2 * HEAD_SLOT))], axis=1)

    vt_top = jnp.transpose(v, (1, 2, 0))
    vt_top = jnp.concatenate(
        [vt_top, jnp.zeros((N_HEADS_B, VT_ROWS - V_HEAD_DIM, KV_LORA_RANK), v.dtype)], axis=1)
    ones_sel = np.zeros((N_HEADS_B, VT_ROWS, LAT_WIDTH - KV_LORA_RANK), np.float32)
    ones_sel[:, VT_ONES_ROW, -1] = 1.0
    wvt = jnp.concatenate([vt_top, jnp.asarray(ones_sel)], axis=2)
    return wk.astype(jnp.bfloat16), wvt.reshape(N_HEADS_B * VT_ROWS, LAT_WIDTH).astype(jnp.bfloat16)


def _rope_lane_tables(seq):
    inv_freq = ROPE_THETA ** (-np.arange(0, QK_ROPE_DIM, 2, dtype=np.float64) / QK_ROPE_DIM)
    freqs = np.arange(seq, dtype=np.float64)[:, None] * inv_freq[None, :]
    cos, sin = np.cos(freqs), np.sin(freqs)
    cos2, sin2 = np.concatenate([cos, cos], axis=1), np.concatenate([sin, sin], axis=1)
    ones = np.ones((seq, QK_NOPE_DIM))
    q_scale = (QK_NOPE_DIM + QK_ROPE_DIM) ** -0.5 * math.log2(math.e)
    q_table = np.concatenate([ones, cos2, sin2], axis=1) * q_scale
    k_cos = np.concatenate([ones, cos2, cos2], axis=1)
    k_sin = np.concatenate([np.zeros_like(ones), sin2, sin2], axis=1)
    return tuple(jnp.asarray(t, jnp.float32) for t in (q_table, k_cos, k_sin))


def kernel(x, mix_norm_g, w_in, q_norm_g, w_q_b, kv_norm_g, w_kv_b, w_out,
           mlp_norm_g, w_up, w_down, rel_bias, final_norm_g):
    batch, seq, _ = x.shape
    depth = w_in.shape[0]
    assert depth == 1, "the final norm is fused into the single layer's MLP kernel"
    tq, ck, sk = _rope_lane_tables(seq)
    biases = _dilated_biases(rel_bias)
    layer = 0
    x2 = x.reshape(batch * seq, D_MODEL)
    wk, wvt = _prep_kv_weights(w_kv_b[layer])
    qa, ka, va, qa4, ka4, va4, qa16, ka16, va16, qm, km, vt = _proj_call(
        x2, mix_norm_g[layer][None], _prep_in_weights(w_in[layer]),
        q_norm_g[layer][None], _prep_q_weights(w_q_b[layer]),
        kv_norm_g[layer][None], wk, wvt, tq, ck, sk, batch, seq)

    shape_a = (batch, 1, seq, WIDTH_A)
    q_blocks = dict(zip([d for (_, d) in DILATED_PATTERNS], DIL_Q_BLOCKS))
    o16, l16 = _dilated_call(qa16, ka16, va16, biases[2], q_blocks=q_blocks[16])
    o4, l4 = _dilated_call(qa4, ka4, va4, biases[1], q_blocks=q_blocks[4])
    oa = _dilated_call(qa.reshape(shape_a), ka.reshape(shape_a), va.reshape(shape_a), biases[0],
                       q_blocks=q_blocks[1], prev=(o4, l4, o16, l16))

    ob = _mla_call(qm.reshape(batch, seq, QK_WIDTH_B), km.reshape(batch, seq, QK_WIDTH_B), vt, seq)

    out = _mlp_call(
        x2, oa.reshape(batch * seq, WIDTH_A), ob.reshape(batch * seq, WIDTH_B),
        w_out[layer].astype(jnp.bfloat16), mlp_norm_g[layer][None],
        w_up[layer].astype(jnp.bfloat16), w_down[layer].astype(jnp.bfloat16),
        final_norm_g[None])
    return out.reshape(batch, seq, D_MODEL)
```

```python
import functools
import math

import jax
import jax.numpy as jnp
import numpy as np
from jax import lax
from jax.experimental import pallas as pl
from jax.experimental.pallas import tpu as pltpu

D_MODEL = 1024
HEAD_DIM = 64
N_HEADS_A = 8
DILATED_PATTERNS = ((128, 1), (512, 4), (2048, 16))
N_HEADS_B = 8
Q_LORA_RANK = 256
KV_LORA_RANK = 128
QK_NOPE_DIM = 64
QK_ROPE_DIM = 32
V_HEAD_DIM = 64
ROPE_THETA = 10000.0
N_BUCKETS = 32
MAX_DISTANCE = 1024
D_FF = 4 * D_MODEL
NORM_EPS = 1e-6
NEG_INF = -1e30
WIDTH_A = N_HEADS_A * HEAD_DIM
WIDTH_B = N_HEADS_B * V_HEAD_DIM

LANES = 128
VMEM_LIMIT_BYTES = 56 * 1024 * 1024

HEAD_SLOT = LANES
QK_WIDTH_B = N_HEADS_B * HEAD_SLOT
VT_ROWS = 80
VT_ONES_ROW = V_HEAD_DIM
IN_COLS_PAD = 2048
CQ_OFF = 3 * WIDTH_A
LAT_WIDTH = 2 * LANES
RADIUS = 64
Q_BLK_A = 2 * RADIUS
K_WIN_A = 4 * RADIUS
BIAS_SPAN = 2 * K_WIN_A

PROJ_TM = 512
PROJ_X_SLOTS = 3
MLP_TM = 1024
MLA_TQ = 256
DIL_LOOKAHEAD = 3
DIL_BUFFERS = 4
DIL_Q_BLOCKS = (8, 4, 1)
MLA_TK = 512
FF_CHUNK = 1024


def _rms(xf, g):
    return xf * lax.rsqrt(jnp.mean(xf * xf, axis=-1, keepdims=True) + NORM_EPS) * g


def _proj_kernel(x_ref, g_ref, win_ref, qg_ref, wq_ref, kvg_ref, wk_ref, wvt_ref,
                 tq_ref, ck_ref, sk_ref,
                 qa_ref, ka_ref, va_ref, qa4_ref, ka4_ref, va4_ref, qa16_ref, ka16_ref, va16_ref,
                 qm_ref, km_ref, vt_ref, slab_scr, part_scr, x_buf, x_sem):
    i = pl.program_id(0)
    n_steps = pl.num_programs(0)

    def x_copy(step):
        start = step * PROJ_TM
        if not isinstance(step, int):
            start = pl.multiple_of(start, PROJ_TM)
        slot = step % PROJ_X_SLOTS
        return pltpu.make_async_copy(x_ref.at[pl.ds(start, PROJ_TM), :], x_buf.at[slot], x_sem.at[slot])

    @pl.when(i == 0)
    def _():
        for step in range(PROJ_X_SLOTS - 1):
            x_copy(step).start()

    @pl.when(i + PROJ_X_SLOTS - 1 < n_steps)
    def _():
        x_copy(i + PROJ_X_SLOTS - 1).start()

    x_copy(i).wait()
    x = x_buf[i % PROJ_X_SLOTS]
    u = _rms(x, g_ref[...]).astype(jnp.bfloat16)

    def in_proj(lo, hi):
        return jnp.dot(u, win_ref[:, lo:hi], preferred_element_type=jnp.float32)

    tm = x.shape[0]
    n_slabs = WIDTH_A // LANES
    scales = (HEAD_DIM ** -0.5 * math.log2(math.e), 1.0, 1.0)
    groups = ((qa_ref, qa4_ref, qa16_ref), (ka_ref, ka4_ref, ka16_ref), (va_ref, va4_ref, va16_ref))

    def dilated_group(a):
        nat_ref, *strided_refs = groups[a]
        t = in_proj(a * WIDTH_A, (a + 1) * WIDTH_A) * scales[a]
        nat_ref[...] = t.astype(jnp.bfloat16)
        ref4, ref16 = strided_refs
        d4, d16 = ref4.shape[1], ref16.shape[1]
        assert d16 == d4 * d4, "the second copy is a stride-d4 pass over the first"
        for g in range(n_slabs):
            cols = slice(g * LANES, (g + 1) * LANES)
            slab_scr[a * n_slabs + g] = t[:, cols]
            for r in range(d4):
                part = slab_scr[a * n_slabs + g, pl.ds(r, tm // d4, stride=d4), :]
                ref4[0, r, :, cols] = part.astype(jnp.bfloat16)
                part_scr[a * n_slabs + g, r] = part
            for r in range(d16):
                part = part_scr[a * n_slabs + g, r % d4, pl.ds(r // d4, tm // d16, stride=d4), :]
                ref16[0, r, :, cols] = part.astype(jnp.bfloat16)

    dilated_group(0)
    proj = in_proj(CQ_OFF, IN_COLS_PAD)
    dilated_group(1)

    cq = _rms(proj[:, :Q_LORA_RANK], qg_ref[...]).astype(jnp.bfloat16)
    q2 = jnp.dot(cq, wq_ref[...], preferred_element_type=jnp.float32)
    qm_ref[...] = (q2 * jnp.tile(tq_ref[...], (1, N_HEADS_B))).astype(jnp.bfloat16)
    dilated_group(2)

    ckv = _rms(proj[:, Q_LORA_RANK:Q_LORA_RANK + KV_LORA_RANK], kvg_ref[...])
    hi = proj[:, Q_LORA_RANK + KV_LORA_RANK:]
    lane = lax.broadcasted_iota(jnp.int32, hi.shape, 1)
    hi = jnp.where(lane == LANES - 1, 1.0, hi)
    lat = jnp.concatenate([ckv, hi], axis=-1).astype(jnp.bfloat16)
    k2 = jnp.dot(lat, wk_ref[...], preferred_element_type=jnp.float32)
    rope = (k2[:, QK_WIDTH_B:QK_WIDTH_B + HEAD_SLOT] * ck_ref[...]
            + k2[:, QK_WIDTH_B + HEAD_SLOT:] * sk_ref[...])
    nope_lane = lax.broadcasted_iota(jnp.int32, rope.shape, 1) < QK_NOPE_DIM
    for h in range(N_HEADS_B):
        slot = slice(h * HEAD_SLOT, (h + 1) * HEAD_SLOT)
        km_ref[:, slot] = jnp.where(nope_lane, k2[:, slot], rope).astype(jnp.bfloat16)
    vt = lax.dot_general(wvt_ref[...], lat, (((1,), (1,)), ((), ())),
                         preferred_element_type=jnp.float32)
    vt_ref[0, 0] = vt.astype(jnp.bfloat16)


def _proj_call(x2, g, win, qg, wq, kvg, wk, wvt, tq, ck, sk, batch, seq):
    tokens = x2.shape[0]
    tm = PROJ_TM
    assert tokens // tm >= PROJ_X_SLOTS - 1, "the first grid step starts that many x blocks"
    sblk = seq // tm
    row = lambda i: (i, 0)
    const = lambda i: (0, 0)
    pos = lambda i: (i % sblk, 0)

    def wspec(shape):
        return pl.BlockSpec(shape, const, pipeline_mode=pl.Buffered(1))

    strides = [d for (_, d) in DILATED_PATTERNS if d > 1]

    def strided_spec(d):
        return pl.BlockSpec((1, d, tm // d, WIDTH_A), lambda i: (i // sblk, 0, i % sblk, 0))

    bf = jnp.bfloat16
    return pl.pallas_call(
        _proj_kernel,
        grid=(tokens // tm,),
        in_specs=[
            pl.BlockSpec(memory_space=pl.ANY),
            wspec((1, D_MODEL)),
            wspec((D_MODEL, IN_COLS_PAD)),
            wspec((1, Q_LORA_RANK)),
            wspec((Q_LORA_RANK, QK_WIDTH_B)),
            wspec((1, KV_LORA_RANK)),
            wspec((LAT_WIDTH, QK_WIDTH_B + 2 * HEAD_SLOT)),
            wspec((N_HEADS_B * VT_ROWS, LAT_WIDTH)),
            pl.BlockSpec((tm, HEAD_SLOT), pos),
            pl.BlockSpec((tm, HEAD_SLOT), pos),
            pl.BlockSpec((tm, HEAD_SLOT), pos),
        ],
        out_specs=[
            pl.BlockSpec((tm, WIDTH_A), row),
            pl.BlockSpec((tm, WIDTH_A), row),
            pl.BlockSpec((tm, WIDTH_A), row),
            *[strided_spec(d) for d in strides for _ in range(3)],
            pl.BlockSpec((tm, QK_WIDTH_B), row),
            pl.BlockSpec((tm, QK_WIDTH_B), row),
            pl.BlockSpec((1, 1, N_HEADS_B * VT_ROWS, tm), lambda i: (i // sblk, i % sblk, 0, 0)),
        ],
        out_shape=[
            jax.ShapeDtypeStruct((tokens, WIDTH_A), bf),
            jax.ShapeDtypeStruct((tokens, WIDTH_A), bf),
            jax.ShapeDtypeStruct((tokens, WIDTH_A), bf),
            *[jax.ShapeDtypeStruct((batch, d, seq // d, WIDTH_A), bf) for d in strides for _ in range(3)],
            jax.ShapeDtypeStruct((tokens, QK_WIDTH_B), bf),
            jax.ShapeDtypeStruct((tokens, QK_WIDTH_B), bf),
            jax.ShapeDtypeStruct((batch, sblk, N_HEADS_B * VT_ROWS, tm), bf),
        ],
        scratch_shapes=[
            pltpu.VMEM((3 * WIDTH_A // LANES, tm, LANES), jnp.float32),
            pltpu.VMEM((3 * WIDTH_A // LANES, strides[0], tm // strides[0], LANES), jnp.float32),
            pltpu.VMEM((PROJ_X_SLOTS, tm, D_MODEL), jnp.float32),
            pltpu.SemaphoreType.DMA((PROJ_X_SLOTS,)),
        ],
        compiler_params=pltpu.CompilerParams(
            dimension_semantics=("arbitrary",), vmem_limit_bytes=VMEM_LIMIT_BYTES),
        name="proj",
    )(x2, g, win, qg, wq, kvg, wk, wvt, tq, ck, sk)


def _dilated_kernel(*refs, dilation, final, q_blocks, n_buf):
    if final:
        q_ref, k_ref, v_ref, bias_ref, o2_ref, l2_ref, o3_ref, l3_ref, out_ref = refs[:9]
    else:
        q_ref, k_ref, v_ref, bias_ref, o_ref, l_ref = refs[:6]
    bias_tab = refs[-2 * n_buf - 1]
    scratch = refs[-2 * n_buf:]
    z_bufs, m_bufs = scratch[:n_buf], scratch[n_buf:]
    n_pairs = N_HEADS_A // 2
    assert n_pairs % n_buf == 0 and DIL_LOOKAHEAD < n_buf, "buffer rotation must restart per item"
    assert dilation & (dilation - 1) == 0, "work items are decoded with shifts"
    j = pl.program_id(1)
    sub_len = k_ref.shape[2]
    nblk = sub_len // Q_BLK_A
    n_items = dilation * q_blocks
    lane = lax.broadcasted_iota(jnp.int32, (Q_BLK_A, LANES), 1)
    low_half = lane < HEAD_DIM

    @pl.when((pl.program_id(0) == 0) & (j == 0))
    def _():
        for h in range(N_HEADS_A):
            base = jnp.broadcast_to(bias_ref[h:h + 1, :], (Q_BLK_A, BIAS_SPAN))
            for variant in range(3):
                skew = pltpu.roll(base, variant * RADIUS, 1, stride=1, stride_axis=0)
                bias_tab[variant, h] = skew[:, :K_WIN_A]

    def coords(item):
        if isinstance(item, int):
            return (item // dilation) * Q_BLK_A, item % dilation
        qb = lax.shift_right_logical(item, int(math.log2(dilation)))
        return pl.multiple_of(qb * Q_BLK_A, Q_BLK_A), lax.bitwise_and(item, dilation - 1)

    def window(q0):
        jb = j * q_blocks + q0 // Q_BLK_A
        w0 = pl.multiple_of(jnp.clip(jb * Q_BLK_A - RADIUS, 0, sub_len - K_WIN_A), RADIUS)
        variant = jnp.where(jb == 0, 0, jnp.where(jb == nblk - 1, 2, 1))
        return pl.ds(w0, K_WIN_A), variant

    def scores(item, g):
        q0, r = coords(item)
        win, variant = window(q0)
        cols = slice(g * LANES, (g + 1) * LANES)
        qg = q_ref[0, r, pl.ds(q0, Q_BLK_A), cols]
        zero = jnp.zeros_like(qg)
        q2 = jnp.concatenate([jnp.where(low_half, qg, zero), jnp.where(low_half, zero, qg)], axis=0)
        s = lax.dot_general(q2, k_ref[0, r, win, cols], (((1,), (1,)), ((), ())),
                            preferred_element_type=jnp.float32)
        s = s + bias_tab[variant, 2 * g:2 * g + 2].reshape(2 * Q_BLK_A, K_WIN_A)
        m = jnp.max(jnp.maximum(s[:, :LANES], s[:, LANES:]), axis=-1, keepdims=True)
        z_bufs[g % n_buf][...] = s - m
        m_bufs[g % n_buf][...] = jnp.broadcast_to(m, (2 * Q_BLK_A, LANES))

    def attend(item, g):
        q0, r = coords(item)
        win, _ = window(q0)
        cols = slice(g * LANES, (g + 1) * LANES)
        p = jnp.exp2(z_bufs[g % n_buf][...])
        den = jnp.sum(p[:, :LANES] + p[:, LANES:], axis=-1, keepdims=True)
        pv = jnp.dot(p.astype(jnp.bfloat16), v_ref[0, r, win, cols],
                     preferred_element_type=jnp.float32)
        m = m_bufs[g % n_buf][...]
        den_pair = jnp.where(low_half, den[:Q_BLK_A], den[Q_BLK_A:])
        o_pair = jnp.where(low_half, pv[:Q_BLK_A], pv[Q_BLK_A:]) / den_pair
        l_pair = jnp.where(low_half, m[:Q_BLK_A], m[Q_BLK_A:]) + jnp.log2(den_pair)
        return o_pair, l_pair

    def work_item(item, carry):
        q0, r = coords(item)
        item_next = jnp.minimum(item + 1, n_items - 1)
        if dilation == 1:
            rows = pl.ds(q0, Q_BLK_A)
        else:
            rows = pl.ds(q0 * dilation + r, Q_BLK_A, stride=dilation)
        for g in range(n_pairs):
            ahead = g + DIL_LOOKAHEAD
            if ahead < n_pairs:
                scores(item, ahead)
            else:
                scores(item_next, ahead - n_pairs)
            o_pair, l_pair = attend(item, g)
            if final:
                l2, l3 = l2_ref[0, g, rows, :], l3_ref[0, g, rows, :]
                top = jnp.maximum(jnp.maximum(l_pair, l2), l3)
                w1, w2, w3 = jnp.exp2(l_pair - top), jnp.exp2(l2 - top), jnp.exp2(l3 - top)
                o2, o3 = (pltpu.unpack_elementwise(
                    ref[0, g // 2, rows, :], index=g % 2, packed_dtype=jnp.bfloat16,
                    unpacked_dtype=jnp.float32) for ref in (o2_ref, o3_ref))
                merged = (w1 * o_pair + w2 * o2 + w3 * o3) / (w1 + w2 + w3)
                out_ref[0, rows, g * LANES:(g + 1) * LANES] = merged.astype(out_ref.dtype)
            else:
                l_ref[0, g, rows, :] = l_pair
                if g % 2 == 0:
                    o_even = o_pair
                else:
                    o_ref[0, g // 2, rows, :] = pltpu.pack_elementwise(
                        [o_even, o_pair], packed_dtype=jnp.bfloat16)
        return carry

    for g in range(DIL_LOOKAHEAD):
        scores(0, g)
    lax.fori_loop(0, n_items, work_item, 0)


def _dilated_call(q, k, v, bias, q_blocks=1, prev=None):
    batch, dilation, sub_len, _ = q.shape
    seq = sub_len * dilation
    final = prev is not None
    tokens = Q_BLK_A * dilation * q_blocks
    n_pairs = N_HEADS_A // 2
    whole = pl.BlockSpec((1, dilation, sub_len, WIDTH_A), lambda b, j: (b, 0, 0, 0))
    in_specs = [
        pl.BlockSpec((1, dilation, q_blocks * Q_BLK_A, WIDTH_A), lambda b, j: (b, 0, j, 0)),
        whole,
        whole,
        pl.BlockSpec((N_HEADS_A, BIAS_SPAN), lambda b, j: (0, 0), pipeline_mode=pl.Buffered(1)),
    ]
    args = [q, k, v, bias]
    def slab_spec(n):
        return pl.BlockSpec((1, n, tokens, LANES), lambda b, j: (b, 0, j, 0))

    slab_specs = [slab_spec(n_pairs // 2), slab_spec(n_pairs)]
    if final:
        in_specs += slab_specs * 2
        args += list(prev)
        out_specs = pl.BlockSpec((1, tokens, WIDTH_A), lambda b, j: (b, j, 0))
        out_shape = jax.ShapeDtypeStruct((batch, seq, WIDTH_A), jnp.bfloat16)
    else:
        out_specs = slab_specs
        out_shape = [jax.ShapeDtypeStruct((batch, n_pairs // 2, seq, LANES), jnp.uint32),
                     jax.ShapeDtypeStruct((batch, n_pairs, seq, LANES), jnp.float32)]
    return pl.pallas_call(
        functools.partial(_dilated_kernel, dilation=dilation, final=final, q_blocks=q_blocks,
                          n_buf=DIL_BUFFERS),
        grid=(batch, sub_len // (q_blocks * Q_BLK_A)),
        in_specs=in_specs,
        out_specs=out_specs,
        out_shape=out_shape,
        scratch_shapes=(
            [pltpu.VMEM((3, N_HEADS_A, Q_BLK_A, K_WIN_A), jnp.float32)]
            + [pltpu.VMEM((2 * Q_BLK_A, K_WIN_A), jnp.float32)] * DIL_BUFFERS
            + [pltpu.VMEM((2 * Q_BLK_A, LANES), jnp.float32)] * DIL_BUFFERS),
        compiler_params=pltpu.CompilerParams(
            dimension_semantics=("arbitrary", "arbitrary"),
            vmem_limit_bytes=VMEM_LIMIT_BYTES),
        name=f"dilated_d{dilation}",
    )(*args)


def _t5_buckets(rel):
    nb = N_BUCKETS // 2
    max_exact = nb // 2
    ret = (rel > 0).astype(np.int32) * nb
    n = np.abs(rel)
    large = max_exact + (np.log(np.maximum(n, 1) / max_exact)
                         / np.log(MAX_DISTANCE / max_exact) * (nb - max_exact)).astype(np.int32)
    large = np.minimum(large, nb - 1)
    return (ret + np.where(n < max_exact, n, large)).astype(np.int32)


def _dilated_biases(rel_bias):
    idx = np.arange(BIAS_SPAN)
    rel = np.where(idx < BIAS_SPAN // 2, idx, idx - BIAS_SPAN)
    valid = np.abs(rel) <= RADIUS
    buckets = np.stack([_t5_buckets(rel * d) for (_, d) in DILATED_PATTERNS])
    onehot = jnp.asarray((buckets[None] == np.arange(N_BUCKETS)[:, None, None]).astype(np.float32))
    b = jnp.einsum('nh,npj->phj', rel_bias.astype(jnp.float32), onehot,
                   precision=lax.Precision.HIGHEST)
    vectors = jnp.where(jnp.asarray(valid)[None, None, :], b * math.log2(math.e), NEG_INF)
    return [vectors[p] for p in range(len(DILATED_PATTERNS))]


def _mla_kernel(q_ref, k_ref, vt_ref, o_ref, s_even, s_odd, p_even, p_odd, m_even, m_odd):
    seq = k_ref.shape[1]
    nk = seq // MLA_TK
    nq = seq // MLA_TQ
    heads = range(2)
    s_bufs, p_bufs, m_bufs = (s_even, s_odd), (p_even, p_odd), (m_even, m_odd)
    assert nq % 2 == 0 and nq >= 4

    def q_rows(t):
        start = t * MLA_TQ
        return pl.ds(start if isinstance(t, int) else pl.multiple_of(start, MLA_TQ), MLA_TQ)

    def keys(c):
        return slice(c * MLA_TK, (c + 1) * MLA_TK)

    def stage(score_t=None, exp_t=None, pv_t=None):
        run_max = [None, None]
        for c in range(nk):
            if score_t is not None:
                t, par = score_t
                for hh in heads:
                    lanes = slice(hh * HEAD_SLOT, (hh + 1) * HEAD_SLOT)
                    s = lax.dot_general(k_ref[0, keys(c), lanes], q_ref[0, q_rows(t), lanes],
                                        (((1,), (1,)), ((), ())),
                                        preferred_element_type=jnp.float32)
                    s_bufs[par][hh, keys(c), :] = s
                    mc = jnp.max(s, axis=0, keepdims=True)
                    run_max[hh] = mc if c == 0 else jnp.maximum(run_max[hh], mc)
            if exp_t is not None:
                _, par = exp_t
                for hh in heads:
                    z = s_bufs[par][hh, keys(c), :] - m_bufs[par][hh]
                    p_bufs[par][hh, keys(c), :] = jnp.exp2(z).astype(jnp.bfloat16)
        if score_t is not None:
            for hh in heads:
                m_bufs[score_t[1]][hh] = run_max[hh]
        if pv_t is not None:
            _, par = pv_t
            n_blk = vt_ref.shape[1]
            outs = []
            for hh in heads:
                vt = jnp.concatenate([vt_ref[0, b, hh * VT_ROWS:(hh + 1) * VT_ROWS, :]
                                      for b in range(n_blk)], axis=1)
                acc = jnp.dot(vt, p_bufs[par][hh], preferred_element_type=jnp.float32)
                outs.append(acc[:V_HEAD_DIM] / acc[VT_ONES_ROW:VT_ONES_ROW + 1])
            o_ref[0, q_rows(pv_t[0]), :] = jnp.concatenate(outs, axis=0).T.astype(o_ref.dtype)

    stage(score_t=(0, 0))
    stage(score_t=(1, 1), exp_t=(0, 0))

    def stage_pair(i, carry):
        t = 2 * i + 1
        stage(score_t=(t + 1, 0), exp_t=(t, 1), pv_t=(t - 1, 0))
        stage(score_t=(t + 2, 1), exp_t=(t + 1, 0), pv_t=(t, 1))
        return carry

    lax.fori_loop(0, nq // 2 - 1, stage_pair, 0)
    stage(exp_t=(nq - 1, 1), pv_t=(nq - 2, 0))
    stage(pv_t=(nq - 1, 1))


def _mla_call(qm, km, vt, seq):
    batch, n_blk, _, blk_keys = vt.shape
    assert blk_keys % MLA_TK == 0, "a key chunk must not straddle two V^T blocks"
    pair = lambda b, g: (b, 0, g)
    return pl.pallas_call(
        _mla_kernel,
        grid=(batch, N_HEADS_B // 2),
        in_specs=[
            pl.BlockSpec((1, seq, 2 * HEAD_SLOT), pair),
            pl.BlockSpec((1, seq, 2 * HEAD_SLOT), pair),
            pl.BlockSpec((1, n_blk, 2 * VT_ROWS, blk_keys), lambda b, g: (b, 0, g, 0)),
        ],
        out_specs=pl.BlockSpec((1, seq, 2 * V_HEAD_DIM), pair),
        out_shape=jax.ShapeDtypeStruct((batch, seq, WIDTH_B), jnp.bfloat16),
        scratch_shapes=(
            [pltpu.VMEM((2, seq, MLA_TQ), jnp.float32)] * 2
            + [pltpu.VMEM((2, seq, MLA_TQ), jnp.bfloat16)] * 2
            + [pltpu.VMEM((2, 1, MLA_TQ), jnp.float32)] * 2),
        compiler_params=pltpu.CompilerParams(
            dimension_semantics=("arbitrary", "arbitrary"),
            vmem_limit_bytes=VMEM_LIMIT_BYTES),
        name="mla",
    )(qm, km, vt)


def _mlp_kernel(x_ref, oa_ref, ob_ref, wout_ref, g_ref, wup_ref, wdown_ref, gf_ref, o_ref):
    o_cat = jnp.concatenate([oa_ref[...], ob_ref[...]], axis=-1)
    h = x_ref[...] + jnp.dot(o_cat, wout_ref[...], preferred_element_type=jnp.float32)
    u = _rms(h, g_ref[...]).astype(jnp.bfloat16)
    acc = h
    for c in range(D_FF // FF_CHUNK):
        a = jnp.dot(u, wup_ref[:, c * FF_CHUNK:(c + 1) * FF_CHUNK],
                    preferred_element_type=jnp.float32)
        a = jnp.square(jnp.maximum(a, 0.0)).astype(jnp.bfloat16)
        acc = acc + jnp.dot(a, wdown_ref[c * FF_CHUNK:(c + 1) * FF_CHUNK, :],
                            preferred_element_type=jnp.float32)
    o_ref[...] = _rms(acc, gf_ref[...])


def _mlp_call(x2, oa, ob, wout, g, wup, wdown, gf):
    tokens = x2.shape[0]
    tm = MLP_TM
    row = lambda i: (i, 0)
    const = lambda i: (0, 0)

    def wspec(shape):
        return pl.BlockSpec(shape, const, pipeline_mode=pl.Buffered(1))

    return pl.pallas_call(
        _mlp_kernel,
        grid=(tokens // tm,),
        in_specs=[
            pl.BlockSpec((tm, D_MODEL), row),
            pl.BlockSpec((tm, WIDTH_A), row),
            pl.BlockSpec((tm, WIDTH_B), row),
            wspec((WIDTH_A + WIDTH_B, D_MODEL)),
            wspec((1, D_MODEL)),
            wspec((D_MODEL, D_FF)),
            wspec((D_FF, D_MODEL)),
            wspec((1, D_MODEL)),
        ],
        out_specs=pl.BlockSpec((tm, D_MODEL), row),
        out_shape=jax.ShapeDtypeStruct((tokens, D_MODEL), jnp.float32),
        compiler_params=pltpu.CompilerParams(
            dimension_semantics=("arbitrary",), vmem_limit_bytes=VMEM_LIMIT_BYTES),
        name="mlp",
    )(x2, oa, ob, wout, g, wup, wdown, gf)


def _prep_in_weights(w_in):
    pad = jnp.zeros((D_MODEL, IN_COLS_PAD - w_in.shape[1]), w_in.dtype)
    return jnp.concatenate([w_in, pad], axis=1).astype(jnp.bfloat16)


def _prep_q_weights(w_q_b):
    dqk = QK_NOPE_DIM + QK_ROPE_DIM
    half = QK_ROPE_DIM // 2
    w = w_q_b.reshape(Q_LORA_RANK, N_HEADS_B, dqk)
    nope, rope = w[..., :QK_NOPE_DIM], w[..., QK_NOPE_DIM:]
    slot = jnp.concatenate([nope, rope, -rope[..., half:], rope[..., :half]], axis=-1)
    return slot.reshape(Q_LORA_RANK, QK_WIDTH_B).astype(jnp.bfloat16)


def _prep_kv_weights(w_kv_b):
    half = QK_ROPE_DIM // 2
    w = w_kv_b.reshape(KV_LORA_RANK, N_HEADS_B, QK_NOPE_DIM + V_HEAD_DIM)
    k_nope, v = w[..., :QK_NOPE_DIM], w[..., QK_NOPE_DIM:]
    eye = np.eye(QK_ROPE_DIM, dtype=np.float32)
    rot = np.zeros((QK_ROPE_DIM, QK_ROPE_DIM), np.float32)
    for jcol in range(half):
        rot[half + jcol, jcol] = -1.0
        rot[jcol, half + jcol] = 1.0

    nope_cols = jnp.concatenate(
        [k_nope, jnp.zeros((KV_LORA_RANK, N_HEADS_B, HEAD_SLOT - QK_NOPE_DIM), k_nope.dtype)], axis=-1)
    nope_cols = jnp.concatenate(
        [nope_cols.reshape(KV_LORA_RANK, QK_WIDTH_B),
         jnp.zeros((LAT_WIDTH - KV_LORA_RANK, QK_WIDTH_B), k_nope.dtype)], axis=0)
    rope_cols = np.zeros((LAT_WIDTH, 2, HEAD_SLOT), np.float32)
    for which, rope_block in enumerate((eye, rot)):
        for lo in (QK_NOPE_DIM, QK_NOPE_DIM + QK_ROPE_DIM):
            rope_cols[KV_LORA_RANK:KV_LORA_RANK + QK_ROPE_DIM, which, lo:lo + QK_ROPE_DIM] = rope_block
    wk = jnp.concatenate([nope_cols, jnp.asarray(rope_cols.reshape(LAT_WIDTH, 2 * HEAD_SLOT))], axis=1)

    vt_top = jnp.transpose(v, (1, 2, 0))
    vt_top = jnp.concatenate(
        [vt_top, jnp.zeros((N_HEADS_B, VT_ROWS - V_HEAD_DIM, KV_LORA_RANK), v.dtype)], axis=1)
    ones_sel = np.zeros((N_HEADS_B, VT_ROWS, LAT_WIDTH - KV_LORA_RANK), np.float32)
    ones_sel[:, VT_ONES_ROW, -1] = 1.0
    wvt = jnp.concatenate([vt_top, jnp.asarray(ones_sel)], axis=2)
    return wk.astype(jnp.bfloat16), wvt.reshape(N_HEADS_B * VT_ROWS, LAT_WIDTH).astype(jnp.bfloat16)


def _rope_lane_tables(seq):
    inv_freq = ROPE_THETA ** (-np.arange(0, QK_ROPE_DIM, 2, dtype=np.float64) / QK_ROPE_DIM)
    freqs = np.arange(seq, dtype=np.float64)[:, None] * inv_freq[None, :]
    cos, sin = np.cos(freqs), np.sin(freqs)
    cos2, sin2 = np.concatenate([cos, cos], axis=1), np.concatenate([sin, sin], axis=1)
    ones = np.ones((seq, QK_NOPE_DIM))
    q_scale = (QK_NOPE_DIM + QK_ROPE_DIM) ** -0.5 * math.log2(math.e)
    q_table = np.concatenate([ones, cos2, sin2], axis=1) * q_scale
    k_cos = np.concatenate([ones, cos2, cos2], axis=1)
    k_sin = np.concatenate([np.zeros_like(ones), sin2, sin2], axis=1)
    return tuple(jnp.asarray(t, jnp.float32) for t in (q_table, k_cos, k_sin))


def kernel(x, mix_norm_g, w_in, q_norm_g, w_q_b, kv_norm_g, w_kv_b, w_out,
           mlp_norm_g, w_up, w_down, rel_bias, final_norm_g):
    batch, seq, _ = x.shape
    depth = w_in.shape[0]
    assert depth == 1, "the final norm is fused into the single layer's MLP kernel"
    tq, ck, sk = _rope_lane_tables(seq)
    biases = _dilated_biases(rel_bias)
    layer = 0
    x2 = x.reshape(batch * seq, D_MODEL)
    wk, wvt = _prep_kv_weights(w_kv_b[layer])
    qa, ka, va, qa4, ka4, va4, qa16, ka16, va16, qm, km, vt = _proj_call(
        x2, mix_norm_g[layer][None], _prep_in_weights(w_in[layer]),
        q_norm_g[layer][None], _prep_q_weights(w_q_b[layer]),
        kv_norm_g[layer][None], wk, wvt, tq, ck, sk, batch, seq)

    shape_a = (batch, 1, seq, WIDTH_A)
    q_blocks = dict(zip([d for (_, d) in DILATED_PATTERNS], DIL_Q_BLOCKS))
    o16, l16 = _dilated_call(qa16, ka16, va16, biases[2], q_blocks=q_blocks[16])
    o4, l4 = _dilated_call(qa4, ka4, va4, biases[1], q_blocks=q_blocks[4])
    oa = _dilated_call(qa.reshape(shape_a), ka.reshape(shape_a), va.reshape(shape_a), biases[0],
                       q_blocks=q_blocks[1], prev=(o4, l4, o16, l16))

    ob = _mla_call(qm.reshape(batch, seq, QK_WIDTH_B), km.reshape(batch, seq, QK_WIDTH_B), vt, seq)

    out = _mlp_call(
        x2, oa.reshape(batch * seq, WIDTH_A), ob.reshape(batch * seq, WIDTH_B),
        w_out[layer].astype(jnp.bfloat16), mlp_norm_g[layer][None],
        w_up[layer].astype(jnp.bfloat16), w_down[layer].astype(jnp.bfloat16),
        final_norm_g[None])
    return out.reshape(batch, seq, D_MODEL)
```

```python
import functools
import math

import jax
import jax.numpy as jnp
import numpy as np
from jax import lax
from jax.experimental import pallas as pl
from jax.experimental.pallas import tpu as pltpu

D_MODEL = 1024
HEAD_DIM = 64
N_HEADS_A = 8
DILATED_PATTERNS = ((128, 1), (512, 4), (2048, 16))
N_HEADS_B = 8
Q_LORA_RANK = 256
KV_LORA_RANK = 128
QK_NOPE_DIM = 64
QK_ROPE_DIM = 32
V_HEAD_DIM = 64
ROPE_THETA = 10000.0
N_BUCKETS = 32
MAX_DISTANCE = 1024
D_FF = 4 * D_MODEL
NORM_EPS = 1e-6
NEG_INF = -1e30
WIDTH_A = N_HEADS_A * HEAD_DIM
WIDTH_B = N_HEADS_B * V_HEAD_DIM

LANES = 128
VMEM_LIMIT_BYTES = 56 * 1024 * 1024

HEAD_SLOT = LANES
QK_WIDTH_B = N_HEADS_B * HEAD_SLOT
VT_ROWS = 80
VT_ONES_ROW = V_HEAD_DIM
IN_COLS_PAD = 2048
CQ_OFF = 3 * WIDTH_A
LAT_WIDTH = 2 * LANES
RADIUS = 64
Q_BLK_A = 2 * RADIUS
K_WIN_A = 4 * RADIUS
BIAS_SPAN = 2 * K_WIN_A

PROJ_TM = 512
MLP_TM = 1024
MLA_TQ = 256
DIL_LOOKAHEAD = 3
DIL_BUFFERS = 4
DIL_Q_BLOCKS = (8, 4, 1)
MLA_TK = 512
FF_CHUNK = 1024


def _rms(xf, g):
    return xf * lax.rsqrt(jnp.mean(xf * xf, axis=-1, keepdims=True) + NORM_EPS) * g


def _proj_kernel(x_ref, g_ref, win_ref, qg_ref, wq_ref, kvg_ref, wk_ref, wvt_ref,
                 tq_ref, ck_ref, sk_ref,
                 nat_ref, ref4, ref16, qm_ref, km_ref, vt_ref, slab_scr, part_scr):
    x = x_ref[...]
    u = _rms(x, g_ref[...]).astype(jnp.bfloat16)

    def in_proj(lo, hi):
        return jnp.dot(u, win_ref[:, lo:hi], preferred_element_type=jnp.float32)

    tm = x.shape[0]
    n_slabs = WIDTH_A // LANES
    scales = (HEAD_DIM ** -0.5 * math.log2(math.e), 1.0, 1.0)
    d4, d16 = ref4.shape[1], ref16.shape[1]
    assert d16 == d4 * d4, "the second copy is a stride-d4 pass over the first"

    def dilated_group(a):
        t = in_proj(a * WIDTH_A, (a + 1) * WIDTH_A) * scales[a]
        nat_ref[:, a * WIDTH_A:(a + 1) * WIDTH_A] = t.astype(jnp.bfloat16)
        for g in range(n_slabs):
            cols = slice(g * LANES, (g + 1) * LANES)
            out_cols = slice(a * WIDTH_A + g * LANES, a * WIDTH_A + (g + 1) * LANES)
            slab_scr[a * n_slabs + g] = t[:, cols]
            for r in range(d4):
                part = slab_scr[a * n_slabs + g, pl.ds(r, tm // d4, stride=d4), :]
                ref4[0, r, :, out_cols] = part.astype(jnp.bfloat16)
                part_scr[a * n_slabs + g, r] = part
            for r in range(d16):
                part = part_scr[a * n_slabs + g, r % d4, pl.ds(r // d4, tm // d16, stride=d4), :]
                ref16[0, r, :, out_cols] = part.astype(jnp.bfloat16)

    dilated_group(0)
    proj = in_proj(CQ_OFF, IN_COLS_PAD)
    dilated_group(1)

    cq = _rms(proj[:, :Q_LORA_RANK], qg_ref[...]).astype(jnp.bfloat16)
    q2 = jnp.dot(cq, wq_ref[...], preferred_element_type=jnp.float32)
    qm_ref[...] = (q2 * jnp.tile(tq_ref[...], (1, N_HEADS_B))).astype(jnp.bfloat16)
    dilated_group(2)

    ckv = _rms(proj[:, Q_LORA_RANK:Q_LORA_RANK + KV_LORA_RANK], kvg_ref[...])
    hi = proj[:, Q_LORA_RANK + KV_LORA_RANK:]
    lane = lax.broadcasted_iota(jnp.int32, hi.shape, 1)
    hi = jnp.where(lane == LANES - 1, 1.0, hi)
    lat = jnp.concatenate([ckv, hi], axis=-1).astype(jnp.bfloat16)
    k2 = jnp.dot(lat, wk_ref[...], preferred_element_type=jnp.float32)
    rope = (k2[:, QK_WIDTH_B:QK_WIDTH_B + HEAD_SLOT] * ck_ref[...]
            + k2[:, QK_WIDTH_B + HEAD_SLOT:] * sk_ref[...])
    nope_lane = lax.broadcasted_iota(jnp.int32, rope.shape, 1) < QK_NOPE_DIM
    for h in range(N_HEADS_B):
        slot = slice(h * HEAD_SLOT, (h + 1) * HEAD_SLOT)
        km_ref[:, slot] = jnp.where(nope_lane, k2[:, slot], rope).astype(jnp.bfloat16)
    vt = lax.dot_general(wvt_ref[...], lat, (((1,), (1,)), ((), ())),
                         preferred_element_type=jnp.float32)
    vt_ref[0, 0] = vt.astype(jnp.bfloat16)


def _proj_call(x2, g, win, qg, wq, kvg, wk, wvt, tq, ck, sk, batch, seq):
    tokens = x2.shape[0]
    tm = PROJ_TM
    sblk = seq // tm
    row = lambda i: (i, 0)
    const = lambda i: (0, 0)
    pos = lambda i: (i % sblk, 0)

    def wspec(shape):
        return pl.BlockSpec(shape, const, pipeline_mode=pl.Buffered(1))

    strides = [d for (_, d) in DILATED_PATTERNS if d > 1]

    def strided_spec(d):
        return pl.BlockSpec((1, d, tm // d, 3 * WIDTH_A), lambda i: (i // sblk, 0, i % sblk, 0))

    bf = jnp.bfloat16
    return pl.pallas_call(
        _proj_kernel,
        grid=(tokens // tm,),
        in_specs=[
            pl.BlockSpec((tm, D_MODEL), row),
            wspec((1, D_MODEL)),
            wspec((D_MODEL, IN_COLS_PAD)),
            wspec((1, Q_LORA_RANK)),
            wspec((Q_LORA_RANK, QK_WIDTH_B)),
            wspec((1, KV_LORA_RANK)),
            wspec((LAT_WIDTH, QK_WIDTH_B + 2 * HEAD_SLOT)),
            wspec((N_HEADS_B * VT_ROWS, LAT_WIDTH)),
            pl.BlockSpec((tm, HEAD_SLOT), pos),
            pl.BlockSpec((tm, HEAD_SLOT), pos),
            pl.BlockSpec((tm, HEAD_SLOT), pos),
        ],
        out_specs=[
            pl.BlockSpec((tm, 3 * WIDTH_A), row),
            *[strided_spec(d) for d in strides],
            pl.BlockSpec((tm, QK_WIDTH_B), row),
            pl.BlockSpec((tm, QK_WIDTH_B), row),
            pl.BlockSpec((1, 1, N_HEADS_B * VT_ROWS, tm), lambda i: (i // sblk, i % sblk, 0, 0)),
        ],
        out_shape=[
            jax.ShapeDtypeStruct((tokens, 3 * WIDTH_A), bf),
            *[jax.ShapeDtypeStruct((batch, d, seq // d, 3 * WIDTH_A), bf) for d in strides],
            jax.ShapeDtypeStruct((tokens, QK_WIDTH_B), bf),
            jax.ShapeDtypeStruct((tokens, QK_WIDTH_B), bf),
            jax.ShapeDtypeStruct((batch, sblk, N_HEADS_B * VT_ROWS, tm), bf),
        ],
        scratch_shapes=[
            pltpu.VMEM((3 * WIDTH_A // LANES, tm, LANES), jnp.float32),
            pltpu.VMEM((3 * WIDTH_A // LANES, strides[0], tm // strides[0], LANES), jnp.float32),
        ],
        compiler_params=pltpu.CompilerParams(
            dimension_semantics=("arbitrary",), vmem_limit_bytes=VMEM_LIMIT_BYTES),
        name="proj",
    )(x2, g, win, qg, wq, kvg, wk, wvt, tq, ck, sk)


def _dilated_kernel(*refs, dilation, final, q_blocks, n_buf):
    if final:
        q_ref, k_ref, v_ref, bias_ref, o2_ref, l2_ref, o3_ref, l3_ref, out_ref = refs[:9]
    else:
        q_ref, k_ref, v_ref, bias_ref, o_ref, l_ref = refs[:6]
    bias_tab = refs[-2 * n_buf - 1]
    scratch = refs[-2 * n_buf:]
    z_bufs, m_bufs = scratch[:n_buf], scratch[n_buf:]
    n_pairs = N_HEADS_A // 2
    assert n_pairs % n_buf == 0 and DIL_LOOKAHEAD < n_buf, "buffer rotation must restart per item"
    assert dilation & (dilation - 1) == 0, "work items are decoded with shifts"
    j = pl.program_id(1)
    sub_len = k_ref.shape[2]
    nblk = sub_len // Q_BLK_A
    n_items = dilation * q_blocks
    lane = lax.broadcasted_iota(jnp.int32, (Q_BLK_A, LANES), 1)
    low_half = lane < HEAD_DIM

    @pl.when((pl.program_id(0) == 0) & (j == 0))
    def _():
        for h in range(N_HEADS_A):
            base = jnp.broadcast_to(bias_ref[h:h + 1, :], (Q_BLK_A, BIAS_SPAN))
            for variant in range(3):
                skew = pltpu.roll(base, variant * RADIUS, 1, stride=1, stride_axis=0)
                bias_tab[variant, h] = skew[:, :K_WIN_A]

    def coords(item):
        if isinstance(item, int):
            return (item // dilation) * Q_BLK_A, item % dilation
        qb = lax.shift_right_logical(item, int(math.log2(dilation)))
        return pl.multiple_of(qb * Q_BLK_A, Q_BLK_A), lax.bitwise_and(item, dilation - 1)

    def window(q0):
        jb = j * q_blocks + q0 // Q_BLK_A
        w0 = pl.multiple_of(jnp.clip(jb * Q_BLK_A - RADIUS, 0, sub_len - K_WIN_A), RADIUS)
        variant = jnp.where(jb == 0, 0, jnp.where(jb == nblk - 1, 2, 1))
        return pl.ds(w0, K_WIN_A), variant

    def scores(item, g):
        q0, r = coords(item)
        win, variant = window(q0)
        cols = slice(g * LANES, (g + 1) * LANES)
        qg = q_ref[0, r, pl.ds(q0, Q_BLK_A), cols]
        zero = jnp.zeros_like(qg)
        q2 = jnp.concatenate([jnp.where(low_half, qg, zero), jnp.where(low_half, zero, qg)], axis=0)
        s = lax.dot_general(q2, k_ref[0, r, win, cols], (((1,), (1,)), ((), ())),
                            preferred_element_type=jnp.float32)
        s = s + bias_tab[variant, 2 * g:2 * g + 2].reshape(2 * Q_BLK_A, K_WIN_A)
        m = jnp.max(jnp.maximum(s[:, :LANES], s[:, LANES:]), axis=-1, keepdims=True)
        z_bufs[g % n_buf][...] = s - m
        m_bufs[g % n_buf][...] = jnp.broadcast_to(m, (2 * Q_BLK_A, LANES))

    def attend(item, g):
        q0, r = coords(item)
        win, _ = window(q0)
        cols = slice(g * LANES, (g + 1) * LANES)
        p = jnp.exp2(z_bufs[g % n_buf][...])
        den = jnp.sum(p[:, :LANES] + p[:, LANES:], axis=-1, keepdims=True)
        pv = jnp.dot(p.astype(jnp.bfloat16), v_ref[0, r, win, cols],
                     preferred_element_type=jnp.float32)
        m = m_bufs[g % n_buf][...]
        den_pair = jnp.where(low_half, den[:Q_BLK_A], den[Q_BLK_A:])
        o_pair = jnp.where(low_half, pv[:Q_BLK_A], pv[Q_BLK_A:]) / den_pair
        l_pair = jnp.where(low_half, m[:Q_BLK_A], m[Q_BLK_A:]) + jnp.log2(den_pair)
        return o_pair, l_pair

    def work_item(item, carry):
        q0, r = coords(item)
        item_next = jnp.minimum(item + 1, n_items - 1)
        if dilation == 1:
            rows = pl.ds(q0, Q_BLK_A)
        else:
            rows = pl.ds(q0 * dilation + r, Q_BLK_A, stride=dilation)
        for g in range(n_pairs):
            ahead = g + DIL_LOOKAHEAD
            if ahead < n_pairs:
                scores(item, ahead)
            else:
                scores(item_next, ahead - n_pairs)
            o_pair, l_pair = attend(item, g)
            if final:
                l2, l3 = l2_ref[0, g, rows, :], l3_ref[0, g, rows, :]
                top = jnp.maximum(jnp.maximum(l_pair, l2), l3)
                w1, w2, w3 = jnp.exp2(l_pair - top), jnp.exp2(l2 - top), jnp.exp2(l3 - top)
                o2, o3 = (pltpu.unpack_elementwise(
                    ref[0, g // 2, rows, :], index=g % 2, packed_dtype=jnp.bfloat16,
                    unpacked_dtype=jnp.float32) for ref in (o2_ref, o3_ref))
                merged = (w1 * o_pair + w2 * o2 + w3 * o3) / (w1 + w2 + w3)
                out_ref[0, rows, g * LANES:(g + 1) * LANES] = merged.astype(out_ref.dtype)
            else:
                l_ref[0, g, rows, :] = l_pair
                if g % 2 == 0:
                    o_even = o_pair
                else:
                    o_ref[0, g // 2, rows, :] = pltpu.pack_elementwise(
                        [o_even, o_pair], packed_dtype=jnp.bfloat16)
        return carry

    for g in range(DIL_LOOKAHEAD):
        scores(0, g)
    lax.fori_loop(0, n_items, work_item, 0)


def _dilated_call(qkv, bias, q_blocks=1, prev=None):
    batch, dilation, sub_len, _ = qkv.shape
    seq = sub_len * dilation
    final = prev is not None
    tokens = Q_BLK_A * dilation * q_blocks
    n_pairs = N_HEADS_A // 2
    def whole(col_block):
        return pl.BlockSpec((1, dilation, sub_len, WIDTH_A), lambda b, j: (b, 0, 0, col_block))

    in_specs = [
        pl.BlockSpec((1, dilation, q_blocks * Q_BLK_A, WIDTH_A), lambda b, j: (b, 0, j, 0)),
        whole(1),
        whole(2),
        pl.BlockSpec((N_HEADS_A, BIAS_SPAN), lambda b, j: (0, 0), pipeline_mode=pl.Buffered(1)),
    ]
    args = [qkv, qkv, qkv, bias]
    def slab_spec(n):
        return pl.BlockSpec((1, n, tokens, LANES), lambda b, j: (b, 0, j, 0))

    slab_specs = [slab_spec(n_pairs // 2), slab_spec(n_pairs)]
    if final:
        in_specs += slab_specs * 2
        args += list(prev)
        out_specs = pl.BlockSpec((1, tokens, WIDTH_A), lambda b, j: (b, j, 0))
        out_shape = jax.ShapeDtypeStruct((batch, seq, WIDTH_A), jnp.bfloat16)
    else:
        out_specs = slab_specs
        out_shape = [jax.ShapeDtypeStruct((batch, n_pairs // 2, seq, LANES), jnp.uint32),
                     jax.ShapeDtypeStruct((batch, n_pairs, seq, LANES), jnp.float32)]
    return pl.pallas_call(
        functools.partial(_dilated_kernel, dilation=dilation, final=final, q_blocks=q_blocks,
                          n_buf=DIL_BUFFERS),
        grid=(batch, sub_len // (q_blocks * Q_BLK_A)),
        in_specs=in_specs,
        out_specs=out_specs,
        out_shape=out_shape,
        scratch_shapes=(
            [pltpu.VMEM((3, N_HEADS_A, Q_BLK_A, K_WIN_A), jnp.float32)]
            + [pltpu.VMEM((2 * Q_BLK_A, K_WIN_A), jnp.float32)] * DIL_BUFFERS
            + [pltpu.VMEM((2 * Q_BLK_A, LANES), jnp.float32)] * DIL_BUFFERS),
        compiler_params=pltpu.CompilerParams(
            dimension_semantics=("arbitrary", "arbitrary"),
            vmem_limit_bytes=VMEM_LIMIT_BYTES),
        name=f"dilated_d{dilation}",
    )(*args)


def _t5_buckets(rel):
    nb = N_BUCKETS // 2
    max_exact = nb // 2
    ret = (rel > 0).astype(np.int32) * nb
    n = np.abs(rel)
    large = max_exact + (np.log(np.maximum(n, 1) / max_exact)
                         / np.log(MAX_DISTANCE / max_exact) * (nb - max_exact)).astype(np.int32)
    large = np.minimum(large, nb - 1)
    return (ret + np.where(n < max_exact, n, large)).astype(np.int32)


def _dilated_biases(rel_bias):
    idx = np.arange(BIAS_SPAN)
    rel = np.where(idx < BIAS_SPAN // 2, idx, idx - BIAS_SPAN)
    valid = np.abs(rel) <= RADIUS
    buckets = np.stack([_t5_buckets(rel * d) for (_, d) in DILATED_PATTERNS])
    onehot = jnp.asarray((buckets[None] == np.arange(N_BUCKETS)[:, None, None]).astype(np.float32))
    b = jnp.einsum('nh,npj->phj', rel_bias.astype(jnp.float32), onehot,
                   precision=lax.Precision.HIGHEST)
    vectors = jnp.where(jnp.asarray(valid)[None, None, :], b * math.log2(math.e), NEG_INF)
    return [vectors[p] for p in range(len(DILATED_PATTERNS))]


def _mla_kernel(q_ref, k_ref, vt_ref, o_ref, s_even, s_odd, p_even, p_odd, m_even, m_odd):
    seq = k_ref.shape[1]
    nk = seq // MLA_TK
    nq = seq // MLA_TQ
    heads = range(2)
    s_bufs, p_bufs, m_bufs = (s_even, s_odd), (p_even, p_odd), (m_even, m_odd)
    assert nq % 2 == 0 and nq >= 4

    def q_rows(t):
        start = t * MLA_TQ
        return pl.ds(start if isinstance(t, int) else pl.multiple_of(start, MLA_TQ), MLA_TQ)

    def keys(c):
        return slice(c * MLA_TK, (c + 1) * MLA_TK)

    def stage(score_t=None, exp_t=None, pv_t=None):
        run_max = [None, None]
        for c in range(nk):
            if score_t is not None:
                t, par = score_t
                for hh in heads:
                    lanes = slice(hh * HEAD_SLOT, (hh + 1) * HEAD_SLOT)
                    s = lax.dot_general(k_ref[0, keys(c), lanes], q_ref[0, q_rows(t), lanes],
                                        (((1,), (1,)), ((), ())),
                                        preferred_element_type=jnp.float32)
                    s_bufs[par][hh, keys(c), :] = s
                    mc = jnp.max(s, axis=0, keepdims=True)
                    run_max[hh] = mc if c == 0 else jnp.maximum(run_max[hh], mc)
            if exp_t is not None:
                _, par = exp_t
                for hh in heads:
                    z = s_bufs[par][hh, keys(c), :] - m_bufs[par][hh]
                    p_bufs[par][hh, keys(c), :] = jnp.exp2(z).astype(jnp.bfloat16)
        if score_t is not None:
            for hh in heads:
                m_bufs[score_t[1]][hh] = run_max[hh]
        if pv_t is not None:
            _, par = pv_t
            n_blk = vt_ref.shape[1]
            outs = []
            for hh in heads:
                vt = jnp.concatenate([vt_ref[0, b, hh * VT_ROWS:(hh + 1) * VT_ROWS, :]
                                      for b in range(n_blk)], axis=1)
                acc = jnp.dot(vt, p_bufs[par][hh], preferred_element_type=jnp.float32)
                outs.append(acc[:V_HEAD_DIM] / acc[VT_ONES_ROW:VT_ONES_ROW + 1])
            o_ref[0, q_rows(pv_t[0]), :] = jnp.concatenate(outs, axis=0).T.astype(o_ref.dtype)

    stage(score_t=(0, 0))
    stage(score_t=(1, 1), exp_t=(0, 0))

    def stage_pair(i, carry):
        t = 2 * i + 1
        stage(score_t=(t + 1, 0), exp_t=(t, 1), pv_t=(t - 1, 0))
        stage(score_t=(t + 2, 1), exp_t=(t + 1, 0), pv_t=(t, 1))
        return carry

    lax.fori_loop(0, nq // 2 - 1, stage_pair, 0)
    stage(exp_t=(nq - 1, 1), pv_t=(nq - 2, 0))
    stage(pv_t=(nq - 1, 1))


def _mla_call(qm, km, vt, seq):
    batch, n_blk, _, blk_keys = vt.shape
    assert blk_keys % MLA_TK == 0, "a key chunk must not straddle two V^T blocks"
    pair = lambda b, g: (b, 0, g)
    return pl.pallas_call(
        _mla_kernel,
        grid=(batch, N_HEADS_B // 2),
        in_specs=[
            pl.BlockSpec((1, seq, 2 * HEAD_SLOT), pair),
            pl.BlockSpec((1, seq, 2 * HEAD_SLOT), pair),
            pl.BlockSpec((1, n_blk, 2 * VT_ROWS, blk_keys), lambda b, g: (b, 0, g, 0)),
        ],
        out_specs=pl.BlockSpec((1, seq, 2 * V_HEAD_DIM), pair),
        out_shape=jax.ShapeDtypeStruct((batch, seq, WIDTH_B), jnp.bfloat16),
        scratch_shapes=(
            [pltpu.VMEM((2, seq, MLA_TQ), jnp.float32)] * 2
            + [pltpu.VMEM((2, seq, MLA_TQ), jnp.bfloat16)] * 2
            + [pltpu.VMEM((2, 1, MLA_TQ), jnp.float32)] * 2),
        compiler_params=pltpu.CompilerParams(
            dimension_semantics=("arbitrary", "arbitrary"),
            vmem_limit_bytes=VMEM_LIMIT_BYTES),
        name="mla",
    )(qm, km, vt)


def _mlp_kernel(x_ref, oa_ref, ob_ref, wout_ref, g_ref, wup_ref, wdown_ref, gf_ref, o_ref):
    o_cat = jnp.concatenate([oa_ref[...], ob_ref[...]], axis=-1)
    h = x_ref[...] + jnp.dot(o_cat, wout_ref[...], preferred_element_type=jnp.float32)
    u = _rms(h, g_ref[...]).astype(jnp.bfloat16)
    acc = h
    for c in range(D_FF // FF_CHUNK):
        a = jnp.dot(u, wup_ref[:, c * FF_CHUNK:(c + 1) * FF_CHUNK],
                    preferred_element_type=jnp.float32)
        a = jnp.square(jnp.maximum(a, 0.0)).astype(jnp.bfloat16)
        acc = acc + jnp.dot(a, wdown_ref[c * FF_CHUNK:(c + 1) * FF_CHUNK, :],
                            preferred_element_type=jnp.float32)
    o_ref[...] = _rms(acc, gf_ref[...])


def _mlp_call(x2, oa, ob, wout, g, wup, wdown, gf):
    tokens = x2.shape[0]
    tm = MLP_TM
    row = lambda i: (i, 0)
    const = lambda i: (0, 0)

    def wspec(shape):
        return pl.BlockSpec(shape, const, pipeline_mode=pl.Buffered(1))

    return pl.pallas_call(
        _mlp_kernel,
        grid=(tokens // tm,),
        in_specs=[
            pl.BlockSpec((tm, D_MODEL), row),
            pl.BlockSpec((tm, WIDTH_A), row),
            pl.BlockSpec((tm, WIDTH_B), row),
            wspec((WIDTH_A + WIDTH_B, D_MODEL)),
            wspec((1, D_MODEL)),
            wspec((D_MODEL, D_FF)),
            wspec((D_FF, D_MODEL)),
            wspec((1, D_MODEL)),
        ],
        out_specs=pl.BlockSpec((tm, D_MODEL), row),
        out_shape=jax.ShapeDtypeStruct((tokens, D_MODEL), jnp.float32),
        compiler_params=pltpu.CompilerParams(
            dimension_semantics=("arbitrary",), vmem_limit_bytes=VMEM_LIMIT_BYTES),
        name="mlp",
    )(x2, oa, ob, wout, g, wup, wdown, gf)


def _prep_in_weights(w_in):
    pad = jnp.zeros((D_MODEL, IN_COLS_PAD - w_in.shape[1]), w_in.dtype)
    return jnp.concatenate([w_in, pad], axis=1).astype(jnp.bfloat16)


def _prep_q_weights(w_q_b):
    dqk = QK_NOPE_DIM + QK_ROPE_DIM
    half = QK_ROPE_DIM // 2
    w = w_q_b.reshape(Q_LORA_RANK, N_HEADS_B, dqk)
    nope, rope = w[..., :QK_NOPE_DIM], w[..., QK_NOPE_DIM:]
    slot = jnp.concatenate([nope, rope, -rope[..., half:], rope[..., :half]], axis=-1)
    return slot.reshape(Q_LORA_RANK, QK_WIDTH_B).astype(jnp.bfloat16)


def _prep_kv_weights(w_kv_b):
    half = QK_ROPE_DIM // 2
    w = w_kv_b.reshape(KV_LORA_RANK, N_HEADS_B, QK_NOPE_DIM + V_HEAD_DIM)
    k_nope, v = w[..., :QK_NOPE_DIM], w[..., QK_NOPE_DIM:]
    eye = np.eye(QK_ROPE_DIM, dtype=np.float32)
    rot = np.zeros((QK_ROPE_DIM, QK_ROPE_DIM), np.float32)
    for jcol in range(half):
        rot[half + jcol, jcol] = -1.0
        rot[jcol, half + jcol] = 1.0

    nope_cols = jnp.concatenate(
        [k_nope, jnp.zeros((KV_LORA_RANK, N_HEADS_B, HEAD_SLOT - QK_NOPE_DIM), k_nope.dtype)], axis=-1)
    nope_cols = jnp.concatenate(
        [nope_cols.reshape(KV_LORA_RANK, QK_WIDTH_B),
         jnp.zeros((LAT_WIDTH - KV_LORA_RANK, QK_WIDTH_B), k_nope.dtype)], axis=0)
    rope_cols = np.zeros((LAT_WIDTH, 2, HEAD_SLOT), np.float32)
    for which, rope_block in enumerate((eye, rot)):
        for lo in (QK_NOPE_DIM, QK_NOPE_DIM + QK_ROPE_DIM):
            rope_cols[KV_LORA_RANK:KV_LORA_RANK + QK_ROPE_DIM, which, lo:lo + QK_ROPE_DIM] = rope_block
    wk = jnp.concatenate([nope_cols, jnp.asarray(rope_cols.reshape(LAT_WIDTH, 2 * HEAD_SLOT))], axis=1)

    vt_top = jnp.transpose(v, (1, 2, 0))
    vt_top = jnp.concatenate(
        [vt_top, jnp.zeros((N_HEADS_B, VT_ROWS - V_HEAD_DIM, KV_LORA_RANK), v.dtype)], axis=1)
    ones_sel = np.zeros((N_HEADS_B, VT_ROWS, LAT_WIDTH - KV_LORA_RANK), np.float32)
    ones_sel[:, VT_ONES_ROW, -1] = 1.0
    wvt = jnp.concatenate([vt_top, jnp.asarray(ones_sel)], axis=2)
    return wk.astype(jnp.bfloat16), wvt.reshape(N_HEADS_B * VT_ROWS, LAT_WIDTH).astype(jnp.bfloat16)


def _rope_lane_tables(seq):
    inv_freq = ROPE_THETA ** (-np.arange(0, QK_ROPE_DIM, 2, dtype=np.float64) / QK_ROPE_DIM)
    freqs = np.arange(seq, dtype=np.float64)[:, None] * inv_freq[None, :]
    cos, sin = np.cos(freqs), np.sin(freqs)
    cos2, sin2 = np.concatenate([cos, cos], axis=1), np.concatenate([sin, sin], axis=1)
    ones = np.ones((seq, QK_NOPE_DIM))
    q_scale = (QK_NOPE_DIM + QK_ROPE_DIM) ** -0.5 * math.log2(math.e)
    q_table = np.concatenate([ones, cos2, sin2], axis=1) * q_scale
    k_cos = np.concatenate([ones, cos2, cos2], axis=1)
    k_sin = np.concatenate([np.zeros_like(ones), sin2, sin2], axis=1)
    return tuple(jnp.asarray(t, jnp.float32) for t in (q_table, k_cos, k_sin))


def kernel(x, mix_norm_g, w_in, q_norm_g, w_q_b, kv_norm_g, w_kv_b, w_out,
           mlp_norm_g, w_up, w_down, rel_bias, final_norm_g):
    batch, seq, _ = x.shape
    depth = w_in.shape[0]
    assert depth == 1, "the final norm is fused into the single layer's MLP kernel"
    tq, ck, sk = _rope_lane_tables(seq)
    biases = _dilated_biases(rel_bias)
    layer = 0
    x2 = x.reshape(batch * seq, D_MODEL)
    wk, wvt = _prep_kv_weights(w_kv_b[layer])
    qkv, qkv4, qkv16, qm, km, vt = _proj_call(
        x2, mix_norm_g[layer][None], _prep_in_weights(w_in[layer]),
        q_norm_g[layer][None], _prep_q_weights(w_q_b[layer]),
        kv_norm_g[layer][None], wk, wvt, tq, ck, sk, batch, seq)

    q_blocks = dict(zip([d for (_, d) in DILATED_PATTERNS], DIL_Q_BLOCKS))
    o16, l16 = _dilated_call(qkv16, biases[2], q_blocks=q_blocks[16])
    o4, l4 = _dilated_call(qkv4, biases[1], q_blocks=q_blocks[4])
    oa = _dilated_call(qkv.reshape(batch, 1, seq, 3 * WIDTH_A), biases[0],
                       q_blocks=q_blocks[1], prev=(o4, l4, o16, l16))

    ob = _mla_call(qm.reshape(batch, seq, QK_WIDTH_B), km.reshape(batch, seq, QK_WIDTH_B), vt, seq)

    out = _mlp_call(
        x2, oa.reshape(batch * seq, WIDTH_A), ob.reshape(batch * seq, WIDTH_B),
        w_out[layer].astype(jnp.bfloat16), mlp_norm_g[layer][None],
        w_up[layer].astype(jnp.bfloat16), w_down[layer].astype(jnp.bfloat16),
        final_norm_g[None])
    return out.reshape(batch, seq, D_MODEL)
```
